```python
import math, functools
import jax, jax.numpy as jnp
from jax import lax
import numpy as np

D_MODEL = 1024
BATCH = 16
SEQ = 256
DEPTH = 4
DEC_BATCH = 4
DEC_SEQ = 1024
PAST_LEN = 256

GRID_W = 64
D_RNN = D_MODEL
RNN_BLOCKS = 16
RNN_BW = D_RNN // RNN_BLOCKS
CONV_W = 4
CONV_LEFT = (CONV_W - 1) // 2
RG_C = 8.0
N_HEADS = 16
QK_NOPE = 64
QK_ROPE = 32
V_HEAD = 64
Q_LORA = 256
KV_LORA = 128
ROPE_AXIS = QK_ROPE // 2
ROPE_THETA = 10000.0
Q_BLOCK = 128
D_IN = 2 * D_RNN + Q_LORA + KV_LORA + QK_ROPE + 2 * D_MODEL
D_FF = 2816
N_EXPERTS = 8
TOP_K = 2
D_EXPERT = 1408
N_DENSE = (DEPTH + 1) // 2
N_MOE = DEPTH // 2
ALPHA = (2 * DEPTH) ** 0.25
BETA = (8 * DEPTH) ** -0.25
LN_EPS = 1e-5
RMS_EPS = 1e-6

kernel_name = 'hybrid_dit_rglru_mla_step'


def layer_norm(x, g, b):
    xf = x.astype(jnp.float32)
    mu = jnp.mean(xf, axis=-1, keepdims=True)
    var = jnp.mean(jnp.square(xf - mu), axis=-1, keepdims=True)
    return ((xf - mu) * lax.rsqrt(var + LN_EPS) * g + b).astype(x.dtype)


def rms_norm(x, g):
    xf = x.astype(jnp.float32)
    return (xf * lax.rsqrt(jnp.mean(xf * xf, axis=-1, keepdims=True) + RMS_EPS) * g).astype(x.dtype)


def grid_rope_tables(rows, dtype):
    r = jnp.repeat(jnp.arange(rows, dtype=jnp.float32), GRID_W)
    col = jnp.tile(jnp.arange(GRID_W, dtype=jnp.float32), rows)
    freqs = ROPE_THETA ** (-jnp.arange(0, ROPE_AXIS, 2, dtype=jnp.float32) / ROPE_AXIS)
    ar = r[:, None] * freqs
    ac = col[:, None] * freqs
    ang = jnp.concatenate([ar, ar, ac, ac], axis=-1)
    return jnp.cos(ang).astype(dtype), jnp.sin(ang).astype(dtype)


def rope_2d(x, cos, sin):
    xs = x.reshape(x.shape[:-1] + (2, 2, ROPE_AXIS // 2))
    rot = jnp.stack([-xs[..., 1, :], xs[..., 0, :]], axis=-2).reshape(x.shape)
    return x * cos + rot * sin


def dw_conv(x, w, b):
    L = x.shape[1]
    xp = jnp.pad(x, ((0, 0), (CONV_LEFT, CONV_W - 1 - CONV_LEFT), (0, 0)))
    out = b
    for k in range(CONV_W):
        out = out + xp[:, k:k + L] * w[k]
    return out


def rg_lru(x, wa, ba, wx, bx, lam, h0, reverse):
    B, L, _ = x.shape
    xb = x.reshape(B, L, RNN_BLOCKS, RNN_BW)
    gr = jnp.einsum('blnj,njk->blnk', xb, wa).reshape(B, L, D_RNN) + ba
    gi = jnp.einsum('blnj,njk->blnk', xb, wx).reshape(B, L, D_RNN) + bx
    r = jax.nn.sigmoid(gr.astype(jnp.float32))
    i = jax.nn.sigmoid(gi.astype(jnp.float32))
    log_a = -RG_C * r * jax.nn.softplus(-lam.astype(jnp.float32))
    a = jnp.exp(log_a)
    b = jnp.sqrt(-jnp.expm1(2.0 * log_a)) * (i * x.astype(jnp.float32))

    def step(h, ab):
        a_t, b_t = ab
        h = a_t * h + b_t
        return h, h

    h_last, hs = lax.scan(step, h0.astype(jnp.float32),
                          (jnp.swapaxes(a, 0, 1), jnp.swapaxes(b, 0, 1)), reverse=reverse)
    return jnp.swapaxes(hs, 0, 1).astype(x.dtype), h_last.astype(x.dtype)


def mla_attend(q_nope, q_rope, k_nope, k_rope, v):
    B, S, H, _ = q_nope.shape
    blk = Q_BLOCK if S % Q_BLOCK == 0 else S
    nb = S // blk
    scale = 1.0 / math.sqrt(QK_NOPE + QK_ROPE)

    def to_blocks(t):
        return jnp.moveaxis(t.reshape(B, nb, blk, H, t.shape[-1]), 1, 0)

    def one_block(qs):
        qn, qr = qs
        s = jnp.einsum('bqhd,bkhd->bhqk', qn, k_nope) + jnp.einsum('bqhd,bkd->bhqk', qr, k_rope)
        p = jax.nn.softmax(s.astype(jnp.float32) * scale, axis=-1).astype(v.dtype)
        return jnp.einsum('bhqk,bkhd->bqhd', p, v)

    o = lax.map(one_block, (to_blocks(q_nope), to_blocks(q_rope)))
    return jnp.moveaxis(o, 0, 1).reshape(B, S, H, V_HEAD)


def swiglu(h, w1, w3, w2):
    return (jax.nn.silu(h @ w1) * (h @ w3)) @ w2


def moe_swiglu(h, w_router, b_router, w1, w3, w2):
    logits = (h @ w_router).astype(jnp.float32) + b_router
    top_v, top_i = lax.top_k(logits, TOP_K)
    top_w = jax.nn.softmax(top_v, axis=-1)
    gates = jnp.sum(jax.nn.one_hot(top_i, N_EXPERTS, dtype=jnp.float32) * top_w[..., None], axis=-2)
    h1 = jnp.einsum('bld,edf->blef', h, w1)
    h3 = jnp.einsum('bld,edf->blef', h, w3)
    act = jax.nn.silu(h1) * h3 * gates[..., None].astype(h.dtype)
    return jnp.einsum('blef,efd->bld', act, w2)


def ada_mods(cond, w, b):
    m = jax.nn.silu(cond) @ w + b
    return jnp.split(m[..., None, :], 6, axis=-1)


def mixer(h, lp, ctx, rope):
    B, L, _ = h.shape
    cuts = [D_RNN, 2 * D_RNN, 2 * D_RNN + Q_LORA, 2 * D_RNN + Q_LORA + KV_LORA,
            2 * D_RNN + Q_LORA + KV_LORA + QK_ROPE, 2 * D_RNN + Q_LORA + KV_LORA + QK_ROPE + D_MODEL]
    xr, yr, cq, ckv, kr, g_rnn, g_att = jnp.split(h @ lp['w_in'], cuts, axis=-1)

    if ctx is None:
        h0f = jnp.zeros((B, D_RNN), h.dtype)
        h0b = jnp.zeros((B, D_RNN), h.dtype)
    else:
        h0f, h0b = ctx[2], ctx[3]
    xc = dw_conv(xr, lp['conv_w'], lp['conv_b'])
    hf, hf_last = rg_lru(xc, lp['rg_wa'][0], lp['rg_ba'][0], lp['rg_wx'][0], lp['rg_bx'][0],
                         lp['rg_lambda'][0], h0f, False)
    hb, hb_last = rg_lru(xc, lp['rg_wa'][1], lp['rg_ba'][1], lp['rg_wx'][1], lp['rg_bx'][1],
                         lp['rg_lambda'][1], h0b, True)
    rnn_out = ((hf + hb) * jax.nn.gelu(yr)) @ lp['w_rnn_out']

    q = (rms_norm(cq, lp['q_norm_g']) @ lp['w_qb']).reshape(B, L, N_HEADS, QK_NOPE + QK_ROPE)
    q_nope, q_rope = q[..., :QK_NOPE], q[..., QK_NOPE:]
    ckv = rms_norm(ckv, lp['kv_norm_g'])
    if ctx is None:
        ckv_all, kr_all = ckv, kr
    else:
        cos, sin = rope
        q_rope = rope_2d(q_rope, cos[:, None, :], sin[:, None, :])
        kr_lat = rope_2d(kr, cos, sin)
        ckv_all = jnp.concatenate([ctx[0], ckv], axis=1)
        kr_all = jnp.concatenate([ctx[1], kr_lat], axis=1)
    T = ckv_all.shape[1]
    kv = (ckv_all @ lp['w_kvb']).reshape(B, T, N_HEADS, QK_NOPE + V_HEAD)
    attn = mla_attend(q_nope, q_rope, kv[..., :QK_NOPE], kr_all, kv[..., QK_NOPE:])
    att_out = attn.reshape(B, L, N_HEADS * V_HEAD) @ lp['w_attn_out']

    merged = jax.nn.sigmoid(g_rnn) * rnn_out + jax.nn.sigmoid(g_att) * att_out
    return merged @ lp['w_out'], (ckv, kr, hf_last, hb_last)


def trunk_layer(x, mods, lp, ffn, ctx, rope):
    sh1, sc1, g1, sh2, sc2, g2 = mods
    m, ctx_out = mixer(x * (1 + sc1) + sh1, lp, ctx, rope)
    x = layer_norm(ALPHA * x + g1 * m, lp['ln1_g'], lp['ln1_b'])
    f = ffn(x * (1 + sc2) + sh2)
    x = layer_norm(ALPHA * x + g2 * f, lp['ln2_g'], lp['ln2_b'])
    return x, ctx_out


def setup_inputs(seed: int = 0) -> dict:
    key = jax.random.key(seed)
    ks = iter(jax.random.split(key, 48))
    f32 = jnp.float32

    def nrm(shape, scale):
        return jax.random.normal(next(ks), shape, f32) * scale

    def gain(shape):
        return 1.0 + nrm(shape, 0.02)

    u = jax.random.uniform(next(ks), (DEPTH, 2, D_RNN), f32, 0.9, 0.999)
    p = u ** (1.0 / RG_C)
    return {
        'x_prompt': nrm((BATCH, SEQ, D_MODEL), 1.0),
        'x_sample': nrm((DEC_BATCH, DEC_SEQ, D_MODEL), 1.0),
        'cache_ckv': nrm((DEC_BATCH, DEPTH, PAST_LEN, KV_LORA), 1.0),
        'cache_krope': nrm((DEC_BATCH, DEPTH, PAST_LEN, QK_ROPE), 1.0),
        'state_rnn': nrm((DEC_BATCH, DEPTH, 2, D_RNN), 0.5),
        'c': nrm((DEC_BATCH, D_MODEL), 1.0),
        'c_ctx': nrm((D_MODEL,), 1.0),
        'w_ada': nrm((DEPTH, D_MODEL, 6 * D_MODEL), 0.5 * D_MODEL ** -0.5),
        'b_ada': nrm((DEPTH, 6 * D_MODEL), 0.02),
        'w_in': nrm((DEPTH, D_MODEL, D_IN), D_MODEL ** -0.5),
        'conv_w': nrm((DEPTH, CONV_W, D_RNN), CONV_W ** -0.5),
        'conv_b': nrm((DEPTH, D_RNN), 0.02),
        'rg_wa': nrm((DEPTH, 2, RNN_BLOCKS, RNN_BW, RNN_BW), RNN_BW ** -0.5),
        'rg_ba': nrm((DEPTH, 2, D_RNN), 0.02),
        'rg_wx': nrm((DEPTH, 2, RNN_BLOCKS, RNN_BW, RNN_BW), RNN_BW ** -0.5),
        'rg_bx': nrm((DEPTH, 2, D_RNN), 0.02),
        'rg_lambda': jnp.log(p) - jnp.log1p(-p),
        'w_rnn_out': nrm((DEPTH, D_RNN, D_MODEL), BETA * D_RNN ** -0.5),
        'q_norm_g': gain((DEPTH, Q_LORA)),
        'w_qb': nrm((DEPTH, Q_LORA, N_HEADS * (QK_NOPE + QK_ROPE)), Q_LORA ** -0.5),
        'kv_norm_g': gain((DEPTH, KV_LORA)),
        'w_kvb': nrm((DEPTH, KV_LORA, N_HEADS * (QK_NOPE + V_HEAD)), KV_LORA ** -0.5),
        'w_attn_out': nrm((DEPTH, N_HEADS * V_HEAD, D_MODEL), BETA * (N_HEADS * V_HEAD) ** -0.5),
        'w_out': nrm((DEPTH, D_MODEL, D_MODEL), BETA * D_MODEL ** -0.5),
        'ln1_g': gain((DEPTH, D_MODEL)),
        'ln1_b': nrm((DEPTH, D_MODEL), 0.02),
        'ln2_g': gain((DEPTH, D_MODEL)),
        'ln2_b': nrm((DEPTH, D_MODEL), 0.02),
        'w1_dense': nrm((N_DENSE, D_MODEL, D_FF), D_MODEL ** -0.5),
        'w3_dense': nrm((N_DENSE, D_MODEL, D_FF), D_MODEL ** -0.5),
        'w2_dense': nrm((N_DENSE, D_FF, D_MODEL), BETA * D_FF ** -0.5),
        'w_router': nrm((N_MOE, D_MODEL, N_EXPERTS), D_MODEL ** -0.5),
        'b_router': nrm((N_MOE, N_EXPERTS), 0.01),
        'w1_exp': nrm((N_MOE, N_EXPERTS, D_MODEL, D_EXPERT), D_MODEL ** -0.5),
        'w3_exp': nrm((N_MOE, N_EXPERTS, D_MODEL, D_EXPERT), D_MODEL ** -0.5),
        'w2_exp': nrm((N_MOE, N_EXPERTS, D_EXPERT, D_MODEL), BETA * D_EXPERT ** -0.5),
    }


def reference(x_prompt, x_sample, cache_ckv, cache_krope, state_rnn, c, c_ctx,
              w_ada, b_ada, w_in, conv_w, conv_b, rg_wa, rg_ba, rg_wx, rg_bx, rg_lambda,
              w_rnn_out, q_norm_g, w_qb, kv_norm_g, w_kvb, w_attn_out, w_out,
              ln1_g, ln1_b, ln2_g, ln2_b, w1_dense, w3_dense, w2_dense,
              w_router, b_router, w1_exp, w3_exp, w2_exp):
    rows = x_sample.shape[1] // GRID_W
    rope = grid_rope_tables(rows, x_sample.dtype)
    xp, xs = x_prompt, x_sample
    new_ckv, new_kr, new_h = [], [], []
    for l in range(DEPTH):
        lp = {'w_in': w_in[l], 'conv_w': conv_w[l], 'conv_b': conv_b[l],
              'rg_wa': rg_wa[l], 'rg_ba': rg_ba[l], 'rg_wx': rg_wx[l], 'rg_bx': rg_bx[l],
              'rg_lambda': rg_lambda[l], 'w_rnn_out': w_rnn_out[l],
              'q_norm_g': q_norm_g[l], 'w_qb': w_qb[l], 'kv_norm_g': kv_norm_g[l], 'w_kvb': w_kvb[l],
              'w_attn_out': w_attn_out[l], 'w_out': w_out[l],
              'ln1_g': ln1_g[l], 'ln1_b': ln1_b[l], 'ln2_g': ln2_g[l], 'ln2_b': ln2_b[l]}
        j = l // 2
        if l % 2 == 0:
            ffn = functools.partial(swiglu, w1=w1_dense[j], w3=w3_dense[j], w2=w2_dense[j])
        else:
            ffn = functools.partial(moe_swiglu, w_router=w_router[j], b_router=b_router[j],
                                    w1=w1_exp[j], w3=w3_exp[j], w2=w2_exp[j])
        xp, (ckv, kr, hf_last, hb_last) = trunk_layer(xp, ada_mods(c_ctx, w_ada[l], b_ada[l]), lp, ffn, None, None)
        new_ckv.append(ckv)
        new_kr.append(kr)
        new_h.append(jnp.stack([hf_last, hb_last], axis=1))
        ctx = (cache_ckv[:, l], cache_krope[:, l], state_rnn[:, l, 0], state_rnn[:, l, 1])
        xs, _ = trunk_layer(xs, ada_mods(c, w_ada[l], b_ada[l]), lp, ffn, ctx, rope)
    return (xp, xs, jnp.stack(new_ckv, axis=1), jnp.stack(new_kr, axis=1), jnp.stack(new_h, axis=1))
```

```python
import functools
import math

import jax
import jax.numpy as jnp
from jax import lax
from jax.experimental import pallas as pl
from jax.experimental.pallas import tpu as pltpu

F32 = jnp.float32
BF16 = jnp.bfloat16

D_MODEL = 1024
DEPTH = 4
GRID_W = 64
D_RNN = D_MODEL
RNN_BLOCKS = 16
RNN_BW = D_RNN // RNN_BLOCKS
CONV_W = 4
RG_C = 8.0
N_HEADS = 16
QK_NOPE = 64
QK_ROPE = 32
V_HEAD = 64
Q_LORA = 256
KV_LORA = 128
ROPE_AXIS = QK_ROPE // 2
ROPE_THETA = 10000.0
D_FF = 2816
N_EXPERTS = 8
D_EXPERT = 1408
ALPHA = (2 * DEPTH) ** 0.25
LN_EPS = 1e-5
RMS_EPS = 1e-6

LANE = 128
HEAD_PAD = LANE
ROPE_OFF = QK_NOPE
RNN_GROUP = 256
VMEM_LIMIT = 56 * 1024 * 1024

_ROW_TILE = 512
_MOE_ROW_TILE = 512
_ATT_Q_TILE = 512
_FF_CHUNK = 1408


def _dot(a, b):
    return jnp.dot(a, b, preferred_element_type=F32)


def _layer_norm(y, g, b):
    mu = jnp.mean(y, axis=-1, keepdims=True)
    d = y - mu
    var = jnp.mean(d * d, axis=-1, keepdims=True)
    return d * lax.rsqrt(var + LN_EPS) * g + b


def _rms_norm(y, g):
    return y * lax.rsqrt(jnp.mean(y * y, axis=-1, keepdims=True) + RMS_EPS) * g


def _params(*sem):
    return pltpu.CompilerParams(dimension_semantics=sem, vmem_limit_bytes=VMEM_LIMIT)


def _const_spec(shape):
    nd = len(shape)
    return pl.BlockSpec(shape, lambda *_: (0,) * nd, pipeline_mode=pl.Buffered(1))


def _mod_spec(tm, n_ctx_rows, rows_per_latent):
    n_ctx_tiles = n_ctx_rows // tm
    per = rows_per_latent // tm

    def index(i, *_):
        return (jnp.where(i < n_ctx_tiles, 0, 1 + (i - n_ctx_tiles) // per), 0, 0)

    return pl.BlockSpec((None, 6, D_MODEL), index)


def _mods_kernel(c_ref, w_ref, b_ref, o_ref):
    s = jax.nn.silu(c_ref[...]).astype(BF16)
    o_ref[...] = _dot(s, w_ref[...].astype(BF16)) + b_ref[...]


def _ada_mods(cond8, w_ada, b_ada):
    out = pl.pallas_call(
        _mods_kernel,
        grid=(DEPTH, 6),
        in_specs=[
            pl.BlockSpec((8, D_MODEL), lambda l, j: (0, 0)),
            pl.BlockSpec((None, D_MODEL, D_MODEL), lambda l, j: (l, 0, j)),
            pl.BlockSpec((None, 1, D_MODEL), lambda l, j: (l, 0, j)),
        ],
        out_specs=pl.BlockSpec((None, 8, D_MODEL), lambda l, j: (l, 0, j)),
        out_shape=jax.ShapeDtypeStruct((DEPTH, 8, 6 * D_MODEL), F32),
        compiler_params=_params("arbitrary", "arbitrary"),
        name="ada_mods",
    )(cond8, w_ada, b_ada.reshape(DEPTH, 1, 6 * D_MODEL))
    return out.reshape(DEPTH, 8, 6, D_MODEL)


_S_COLS = Q_LORA + KV_LORA + 2 * HEAD_PAD


def _inproj_kernel(x_ref, m_ref, wx_ref, ws_ref, qg_ref, kvg_ref,
                   xr_ref, cq_ref, ckv_ref, kr_ref, krr_ref):
    h = (x_ref[...] * (1.0 + m_ref[1:2, :]) + m_ref[0:1, :]).astype(BF16)
    xr_ref[...] = _dot(h, wx_ref[...])
    s = _dot(h, ws_ref[...])
    cq_ref[...] = _rms_norm(s[:, :Q_LORA], qg_ref[...]).astype(BF16)
    ckv_ref[...] = _rms_norm(s[:, Q_LORA:Q_LORA + KV_LORA], kvg_ref[...])
    kr_ref[...] = s[:, Q_LORA + KV_LORA:Q_LORA + KV_LORA + HEAD_PAD]
    krr_ref[...] = s[:, Q_LORA + KV_LORA + HEAD_PAD:]


def _in_proj(x, mods_l, w_x, w_s, q_g, kv_g, n_ctx_rows, rows_per_latent):
    n_tok = x.shape[0]
    tm = _ROW_TILE
    row = lambda w: pl.BlockSpec((tm, w), lambda i: (i, 0))
    return pl.pallas_call(
        _inproj_kernel,
        grid=(n_tok // tm,),
        in_specs=[
            row(D_MODEL),
            _mod_spec(tm, n_ctx_rows, rows_per_latent),
            _const_spec((D_MODEL, D_RNN)),
            _const_spec((D_MODEL, _S_COLS)),
            _const_spec((1, Q_LORA)),
            _const_spec((1, KV_LORA)),
        ],
        out_specs=[row(D_RNN), row(Q_LORA), row(KV_LORA), row(HEAD_PAD), row(HEAD_PAD)],
        out_shape=[
            jax.ShapeDtypeStruct((n_tok, D_RNN), F32),
            jax.ShapeDtypeStruct((n_tok, Q_LORA), BF16),
            jax.ShapeDtypeStruct((n_tok, KV_LORA), F32),
            jax.ShapeDtypeStruct((n_tok, HEAD_PAD), F32),
            jax.ShapeDtypeStruct((n_tok, HEAD_PAD), F32),
        ],
        compiler_params=_params("arbitrary"),
        name="in_proj",
    )(x, mods_l, w_x, w_s, q_g, kv_g)


def _rglru_kernel(*refs, seq, bt, has_h0, has_prev, emit_last):
    it = iter(refs)
    x_ref, cw_ref, cb_ref, wg_ref, bg_ref, lam_ref = (next(it) for _ in range(6))
    h0_ref = next(it) if has_h0 else None
    if has_prev:
        next(it)
    hs_ref = next(it)
    hl_ref = next(it) if emit_last else None
    af_ref, bf_ref, ab_ref, bb_ref, hb_ref = (next(it) for _ in range(5))
    dg = RNN_GROUP
    a_refs, b_refs = (af_ref, ab_ref), (bf_ref, bb_ref)

    row = lax.broadcasted_iota(jnp.int32, (seq, dg), 0)
    neg_softplus = [jax.nn.softplus(-lam_ref[d:d + 1, :]) for d in range(2)]
    for b in range(bt):
        rows = slice(b * seq, (b + 1) * seq)
        x = x_ref[rows, :]
        xm1 = jnp.where(row >= 1, pltpu.roll(x, 1, 0), 0.0)
        xp1 = jnp.where(row < seq - 1, pltpu.roll(x, seq - 1, 0), 0.0)
        xp2 = jnp.where(row < seq - 2, pltpu.roll(x, seq - 2, 0), 0.0)
        xc = cb_ref[...] + xm1 * cw_ref[0:1, :]
        xc = xc + x * cw_ref[1:2, :]
        xc = xc + xp1 * cw_ref[2:3, :]
        xc = xc + xp2 * cw_ref[3:4, :]
        gates = _dot(xc.astype(BF16), wg_ref[...]) + bg_ref[...]
        for d in range(2):
            r = jax.nn.sigmoid(gates[:, (2 * d) * dg:(2 * d + 1) * dg])
            i = jax.nn.sigmoid(gates[:, (2 * d + 1) * dg:(2 * d + 2) * dg])
            log_a = -RG_C * r * neg_softplus[d]
            a = jnp.exp(log_a)
            a_refs[d][rows, :] = a
            b_refs[d][rows, :] = jnp.sqrt(jnp.tanh(-log_a) * (a * a + 1.0)) * (i * xc)

    if has_h0:
        init_f = tuple(h0_ref[b, 0:1, :] for b in range(bt))
        init_b = tuple(h0_ref[b, 1:2, :] for b in range(bt))
    else:
        zero = jnp.zeros((1, dg), F32)
        init_f = tuple(zero for _ in range(bt))
        init_b = tuple(zero for _ in range(bt))

    def step(k, carry):
        hf, hb = carry
        new_f, new_b = [], []
        for b in range(bt):
            tf = pl.ds(b * seq + k, 1)
            tb = pl.ds(b * seq + seq - 1 - k, 1)
            f = af_ref[tf, :] * hf[b] + bf_ref[tf, :]
            g = ab_ref[tb, :] * hb[b] + bb_ref[tb, :]
            hs_ref[tf, :] = f
            hb_ref[tb, :] = g
            new_f.append(f)
            new_b.append(g)
        return tuple(new_f), tuple(new_b)

    last_f, last_b = lax.fori_loop(0, seq, step, (init_f, init_b), unroll=8)
    hs_ref[...] = hs_ref[...] + hb_ref[...]
    if emit_last:
        for b in range(bt):
            hl_ref[b, 0:1, :] = last_f[b]
            hl_ref[b, 1:2, :] = last_b[b]


def _rglru(xr, row_start, nseq, seq, conv_w, conv_b, w_gate, b_gate, lam, h0=None, prev=None,
           emit_last=False):
    n_tok = xr.shape[0]
    bt = 4
    dg = RNN_GROUP
    ng = D_RNN // dg
    rows = bt * seq
    assert nseq % bt == 0 and row_start % rows == 0
    off = row_start // rows
    in_specs = [
        pl.BlockSpec((rows, dg), lambda bi, g: (off + bi, g)),
        pl.BlockSpec((CONV_W, dg), lambda bi, g: (0, g)),
        pl.BlockSpec((1, dg), lambda bi, g: (0, g)),
        pl.BlockSpec((None, dg, 4 * dg), lambda bi, g: (g, 0, 0)),
        pl.BlockSpec((None, 1, 4 * dg), lambda bi, g: (g, 0, 0)),
        pl.BlockSpec((2, dg), lambda bi, g: (0, g)),
    ]
    args = [xr, conv_w, conv_b, w_gate, b_gate, lam]
    if h0 is not None:
        in_specs.append(pl.BlockSpec((bt, 2, dg), lambda bi, g: (bi, 0, g)))
        args.append(h0)
    aliases = {}
    if prev is not None:
        aliases = {len(args): 0}
        in_specs.append(pl.BlockSpec(memory_space=pl.ANY))
        args.append(prev)
    out_specs = [pl.BlockSpec((rows, dg), lambda bi, g: (off + bi, g))]
    out_shape = [jax.ShapeDtypeStruct((n_tok, D_RNN), F32)]
    if emit_last:
        out_specs.append(pl.BlockSpec((bt, 2, dg), lambda bi, g: (bi, 0, g)))
        out_shape.append(jax.ShapeDtypeStruct((nseq, 2, D_RNN), F32))
    return pl.pallas_call(
        functools.partial(_rglru_kernel, seq=seq, bt=bt, has_h0=h0 is not None,
                          has_prev=prev is not None, emit_last=emit_last),
        grid=(nseq // bt, ng),
        in_specs=in_specs,
        out_specs=out_specs,
        out_shape=out_shape,
        input_output_aliases=aliases,
        scratch_shapes=[pltpu.VMEM((rows, dg), F32) for _ in range(5)],
        compiler_params=_params("arbitrary", "arbitrary"),
        name=f"rglru_{seq}",
    )(*args)


def _attn_kernel(*refs, past, tq, rope, has_prev):
    it = iter(refs)
    cq_ref, ckv_ref, kr_ref = (next(it) for _ in range(3))
    if rope:
        krr_ref, pckv_ref, pkr_ref, cos_ref, sin_ref = (next(it) for _ in range(5))
    wq_ref = next(it)
    wqr_ref = next(it) if rope else None
    wk_ref, wv_ref = next(it), next(it)
    if has_prev:
        next(it)
    o_ref = next(it)
    k_scr, v_scr = next(it), next(it)
    qi = pl.program_id(1)
    scale = 1.0 / math.sqrt(QK_NOPE + QK_ROPE)

    def fill(rows, ckv, k_rope):
        c = ckv.astype(BF16)
        k = _dot(c, wk_ref[...]) + jnp.tile(k_rope, (1, N_HEADS))
        k_scr[rows, :] = k.astype(BF16)
        v_scr[rows, :] = _dot(c, wv_ref[...]).astype(BF16)

    @pl.when(qi == 0)
    def _():
        seq = ckv_ref.shape[0]
        if rope:
            fill(pl.ds(0, past), pckv_ref[...], pkr_ref[...])
            fill(pl.ds(past, seq), ckv_ref[...],
                 kr_ref[...] * cos_ref[...] + krr_ref[...] * sin_ref[...])
        else:
            fill(pl.ds(0, seq), ckv_ref[...], kr_ref[...])

    cq = cq_ref[...]
    qa = _dot(cq, wq_ref[...])
    if rope:
        qb = _dot(cq, wqr_ref[...])
        q0 = pl.multiple_of(qi * tq, tq)
        cos_t = cos_ref[pl.ds(q0, tq), :]
        sin_t = sin_ref[pl.ds(q0, tq), :]
    for j in range(N_HEADS // 2):
        pair = None
        for h in (2 * j, 2 * j + 1):
            cols = slice(h * HEAD_PAD, (h + 1) * HEAD_PAD)
            qh = qa[:, cols]
            if rope:
                qh = qh * cos_t + qb[:, cols] * sin_t
            s = lax.dot_general(qh.astype(BF16), k_scr[:, cols], (((1,), (1,)), ((), ())),
                                preferred_element_type=F32)
            e = jnp.exp((s - jnp.max(s, axis=-1, keepdims=True)) * scale)
            denom = jnp.sum(e, axis=-1, keepdims=True)
            o = _dot(e.astype(BF16), v_scr[:, cols]) * (1.0 / denom)
            pair = o if pair is None else pair + o
        o_ref[:, j * LANE:(j + 1) * LANE] = pair.astype(BF16)


def _attention(cq, ckv, kr, row_start, nseq, seq, w_q, w_k, w_v, lat=None, prev=None):
    n_tok = cq.shape[0]
    rope = lat is not None
    tq = min(_ATT_Q_TILE, seq)
    assert seq % tq == 0 and row_start % seq == 0
    past = lat[2].shape[2] if rope else 0
    total = past + seq
    hp = N_HEADS * HEAD_PAD
    nq = seq // tq
    s_off, q_off = row_start // seq, row_start // tq
    full = lambda w: pl.BlockSpec((seq, w), lambda b, q: (s_off + b, 0))
    q_rows = lambda w: pl.BlockSpec((tq, w), lambda b, q: (q_off + b * nq + q, 0))
    in_specs = [q_rows(Q_LORA), full(KV_LORA), full(HEAD_PAD)]
    args = [cq, ckv, kr]
    if rope:
        krr, layer, pckv, pkr, cos, sin, w_q_rot = lat
        in_specs += [full(HEAD_PAD),
                     pl.BlockSpec((None, None, past, KV_LORA), lambda b, q: (b, layer, 0, 0)),
                     pl.BlockSpec((None, None, past, HEAD_PAD), lambda b, q: (b, layer, 0, 0)),
                     _const_spec((seq, HEAD_PAD)), _const_spec((seq, HEAD_PAD))]
        args += [krr, pckv, pkr, cos, sin]
    in_specs.append(_const_spec((Q_LORA, hp)))
    args.append(w_q)
    if rope:
        in_specs.append(_const_spec((Q_LORA, hp)))
        args.append(w_q_rot)
    in_specs += [_const_spec((KV_LORA, hp)), _const_spec((KV_LORA, hp))]
    args += [w_k, w_v]
    aliases = {}
    if prev is not None:
        aliases = {len(args): 0}
        in_specs.append(pl.BlockSpec(memory_space=pl.ANY))
        args.append(prev)
    return pl.pallas_call(
        functools.partial(_attn_kernel, past=past, tq=tq, rope=rope, has_prev=prev is not None),
        grid=(nseq, nq),
        in_specs=in_specs,
        out_specs=q_rows(N_HEADS * V_HEAD),
        out_shape=jax.ShapeDtypeStruct((n_tok, N_HEADS * V_HEAD), BF16),
        input_output_aliases=aliases,
        scratch_shapes=[pltpu.VMEM((total, hp), BF16), pltpu.VMEM((total, hp), BF16)],
        compiler_params=_params("arbitrary", "arbitrary"),
        name=f"attn_{seq}",
    )(*args)


def _mix_kernel(x_ref, hs_ref, at_ref, m_ref, wygg_ref, wro_ref, wao_ref, wo_ref, g_ref, b_ref, o_ref):
    x = x_ref[...]
    h = (x * (1.0 + m_ref[1:2, :]) + m_ref[0:1, :]).astype(BF16)
    ygg = _dot(h, wygg_ref[...])
    u = (hs_ref[...] * jax.nn.gelu(ygg[:, :D_RNN])).astype(BF16)
    rnn_out = _dot(u, wro_ref[...])
    att_out = _dot(at_ref[...], wao_ref[...])
    merged = (jax.nn.sigmoid(ygg[:, D_RNN:D_RNN + D_MODEL]) * rnn_out
              + jax.nn.sigmoid(ygg[:, D_RNN + D_MODEL:]) * att_out)
    m = _dot(merged.astype(BF16), wo_ref[...])
    o_ref[...] = _layer_norm(ALPHA * x + m_ref[2:3, :] * m, g_ref[...], b_ref[...])


def _mix(x, hsum, attn, mods_l, w_ygg, w_ro, w_ao, w_o, ln_g, ln_b, n_ctx_rows, rows_per_latent):
    n_tok = x.shape[0]
    tm = _ROW_TILE
    row = pl.BlockSpec((tm, D_MODEL), lambda i: (i, 0))
    return pl.pallas_call(
        _mix_kernel,
        grid=(n_tok // tm,),
        in_specs=[row, row, row, _mod_spec(tm, n_ctx_rows, rows_per_latent),
                  _const_spec((D_MODEL, D_RNN + 2 * D_MODEL)),
                  _const_spec((D_RNN, D_MODEL)), _const_spec((N_HEADS * V_HEAD, D_MODEL)),
                  _const_spec((D_MODEL, D_MODEL)), _const_spec((1, D_MODEL)), _const_spec((1, D_MODEL))],
        out_specs=row,
        out_shape=jax.ShapeDtypeStruct((n_tok, D_MODEL), F32),
        compiler_params=_params("arbitrary"),
        name="mix_out",
    )(x, hsum, attn, mods_l, w_ygg, w_ro, w_ao, w_o, ln_g, ln_b)


def _ffn_kernel(x_ref, m_ref, w1_ref, w3_ref, w2_ref, g_ref, b_ref, o_ref):
    x = x_ref[...]
    h = (x * (1.0 + m_ref[4:5, :]) + m_ref[3:4, :]).astype(BF16)
    f = None
    for c in range(D_FF // _FF_CHUNK):
        cols = slice(c * _FF_CHUNK, (c + 1) * _FF_CHUNK)
        act = (jax.nn.silu(_dot(h, w1_ref[:, cols])) * _dot(h, w3_ref[:, cols])).astype(BF16)
        y = _dot(act, w2_ref[cols, :])
        f = y if f is None else f + y
    o_ref[...] = _layer_norm(ALPHA * x + m_ref[5:6, :] * f, g_ref[...], b_ref[...])


def _ffn_dense(x, mods_l, w1, w3, w2, ln_g, ln_b, n_ctx_rows, rows_per_latent):
    n_tok = x.shape[0]
    tm = _ROW_TILE
    row = pl.BlockSpec((tm, D_MODEL), lambda i: (i, 0))
    return pl.pallas_call(
        _ffn_kernel,
        grid=(n_tok // tm,),
        in_specs=[row, _mod_spec(tm, n_ctx_rows, rows_per_latent),
                  _const_spec((D_MODEL, D_FF)), _const_spec((D_MODEL, D_FF)), _const_spec((D_FF, D_MODEL)),
                  _const_spec((1, D_MODEL)), _const_spec((1, D_MODEL))],
        out_specs=row,
        out_shape=jax.ShapeDtypeStruct((n_tok, D_MODEL), F32),
        compiler_params=_params("arbitrary"),
        name="ffn_dense",
    )(x, mods_l, w1, w3, w2, ln_g, ln_b)


def _split_bf16(a):
    hi = a.astype(BF16)
    return hi, (a - hi.astype(F32)).astype(BF16)


def _top2_gates(logits):
    lane = lax.broadcasted_iota(jnp.int32, logits.shape, 1)
    big = jnp.int32(N_EXPERTS)
    v1 = jnp.max(logits, axis=-1, keepdims=True)
    i1 = jnp.min(jnp.where(logits == v1, lane, big), axis=-1, keepdims=True)
    rest = jnp.where(lane == i1, -jnp.inf, logits)
    v2 = jnp.max(rest, axis=-1, keepdims=True)
    i2 = jnp.min(jnp.where(rest == v2, lane, big), axis=-1, keepdims=True)
    e2 = jnp.exp(v2 - v1)
    w1 = 1.0 / (1.0 + e2)
    w2 = e2 / (1.0 + e2)
    return jnp.where(lane == i1, w1, 0.0) + jnp.where(lane == i2, w2, 0.0)


def _moe_kernel(x_ref, m_ref, wr_ref, br_ref, w1_ref, w3_ref, w2_ref, g_ref, b_ref, o_ref,
                h_scr, gate_scr, acc_scr):
    e = pl.program_id(1)

    @pl.when(e == 0)
    def _():
        h = x_ref[...] * (1.0 + m_ref[4:5, :]) + m_ref[3:4, :]
        h_scr[...] = h.astype(BF16)
        h_hi, h_lo = _split_bf16(h)
        w_hi, w_lo = _split_bf16(wr_ref[...])
        logits = _dot(h_hi, w_hi) + (_dot(h_hi, w_lo) + _dot(h_lo, w_hi)) + br_ref[...]
        gate_scr[...] = _top2_gates(logits)
        acc_scr[...] = jnp.zeros_like(acc_scr)

    h = h_scr[...]
    act = (jax.nn.silu(_dot(h, w1_ref[...])) * _dot(h, w3_ref[...])).astype(BF16)
    y = _dot(act, w2_ref[...])
    lane = lax.broadcasted_iota(jnp.int32, gate_scr.shape, 1)
    gate = jnp.sum(jnp.where(lane == e, gate_scr[...], 0.0), axis=-1, keepdims=True)
    acc_scr[...] += gate * y

    @pl.when(e == N_EXPERTS - 1)
    def _():
        o_ref[...] = _layer_norm(ALPHA * x_ref[...] + m_ref[5:6, :] * acc_scr[...], g_ref[...], b_ref[...])


def _ffn_moe(x, mods_l, w_router, b_router, w1, w3, w2, ln_g, ln_b, n_ctx_rows, rows_per_latent):
    n_tok = x.shape[0]
    tm = _MOE_ROW_TILE
    row = pl.BlockSpec((tm, D_MODEL), lambda i, e: (i, 0))
    return pl.pallas_call(
        _moe_kernel,
        grid=(n_tok // tm, N_EXPERTS),
        in_specs=[row, _mod_spec(tm, n_ctx_rows, rows_per_latent),
                  _const_spec((D_MODEL, N_EXPERTS)), _const_spec((1, N_EXPERTS)),
                  pl.BlockSpec((None, D_MODEL, D_EXPERT), lambda i, e: (e, 0, 0)),
                  pl.BlockSpec((None, D_MODEL, D_EXPERT), lambda i, e: (e, 0, 0)),
                  pl.BlockSpec((None, D_EXPERT, D_MODEL), lambda i, e: (e, 0, 0)),
                  _const_spec((1, D_MODEL)), _const_spec((1, D_MODEL))],
        out_specs=row,
        out_shape=jax.ShapeDtypeStruct((n_tok, D_MODEL), F32),
        scratch_shapes=[pltpu.VMEM((tm, D_MODEL), BF16), pltpu.VMEM((tm, N_EXPERTS), F32),
                        pltpu.VMEM((tm, D_MODEL), F32)],
        compiler_params=_params("arbitrary", "arbitrary"),
        name="ffn_moe",
    )(x, mods_l, w_router, b_router, w1, w3, w2, ln_g, ln_b)


def _rot_cols(w):
    ws = w.reshape(w.shape[:-1] + (2, 2, ROPE_AXIS // 2))
    return jnp.stack([-ws[..., 1, :], ws[..., 0, :]], axis=-2).reshape(w.shape)


def _pad_rope_cols(w):
    return jnp.pad(w, ((0, 0), (ROPE_OFF, HEAD_PAD - ROPE_OFF - QK_ROPE)))


def _rope_tables(rows):
    r = jnp.repeat(jnp.arange(rows, dtype=F32), GRID_W)
    col = jnp.tile(jnp.arange(GRID_W, dtype=F32), rows)
    freqs = ROPE_THETA ** (-jnp.arange(0, ROPE_AXIS, 2, dtype=F32) / ROPE_AXIS)
    ar = r[:, None] * freqs
    ac = col[:, None] * freqs
    ang = jnp.concatenate([ar, ar, ac, ac], axis=-1)
    cos, sin = jnp.cos(ang), jnp.sin(ang)
    n = cos.shape[0]
    cos_p = jnp.concatenate([jnp.ones((n, ROPE_OFF), F32), cos,
                             jnp.zeros((n, HEAD_PAD - ROPE_OFF - QK_ROPE), F32)], axis=-1)
    return cos_p, _pad_rope_cols(sin)


def _layer_weights(l, w_in, rg_wa, rg_ba, rg_wx, rg_bx, w_qb, w_kvb):
    wi = w_in[l]
    o = 0
    parts = []
    for width in (D_RNN, D_RNN, Q_LORA, KV_LORA, QK_ROPE, D_MODEL, D_MODEL):
        parts.append(wi[:, o:o + width])
        o += width
    w_x, w_y, w_cq, w_ckv, w_kr, w_gr, w_ga = parts
    w_s = jnp.concatenate([w_cq, w_ckv, _pad_rope_cols(w_kr), _pad_rope_cols(_rot_cols(w_kr))], axis=1)
    w_ygg = jnp.concatenate([w_y, w_gr, w_ga], axis=1)

    wq = w_qb[l].reshape(Q_LORA, N_HEADS, QK_NOPE + QK_ROPE)
    q_nope, q_rope = wq[..., :QK_NOPE], wq[..., QK_NOPE:]
    tail = jnp.zeros((Q_LORA, N_HEADS, HEAD_PAD - QK_NOPE - QK_ROPE), F32)
    w_q = jnp.concatenate([q_nope, q_rope, tail], axis=-1).reshape(Q_LORA, N_HEADS * HEAD_PAD)
    w_q_rot = jnp.concatenate([jnp.zeros_like(q_nope), _rot_cols(q_rope), tail], axis=-1)
    w_q_rot = w_q_rot.reshape(Q_LORA, N_HEADS * HEAD_PAD)
    wkv = w_kvb[l].reshape(KV_LORA, N_HEADS, QK_NOPE + V_HEAD)
    k_nope, v = wkv[..., :QK_NOPE], wkv[..., QK_NOPE:]
    w_k = jnp.concatenate([k_nope, jnp.zeros((KV_LORA, N_HEADS, HEAD_PAD - QK_NOPE), F32)], axis=-1)
    w_k = w_k.reshape(KV_LORA, N_HEADS * HEAD_PAD)
    vz = jnp.zeros_like(v)
    v_even = jnp.concatenate([v, vz], axis=-1)
    v_odd = jnp.concatenate([vz, v], axis=-1)
    odd = (jnp.arange(N_HEADS) % 2 == 1)[None, :, None]
    w_v = jnp.where(odd, v_odd, v_even).reshape(KV_LORA, N_HEADS * HEAD_PAD)

    nblk = RNN_GROUP // RNN_BW
    ng = D_RNN // RNN_GROUP
    eye = jnp.eye(nblk, dtype=F32)

    def block_diag(w):
        wg = w.reshape(ng, nblk, RNN_BW, RNN_BW)
        return jnp.einsum('gnjk,nm->gnjmk', wg, eye).reshape(ng, RNN_GROUP, RNN_GROUP)

    w_gate = jnp.concatenate([block_diag(rg_wa[l, 0]), block_diag(rg_wx[l, 0]),
                              block_diag(rg_wa[l, 1]), block_diag(rg_wx[l, 1])], axis=-1)
    grp = lambda v_: v_.reshape(ng, 1, RNN_GROUP)
    b_gate = jnp.concatenate([grp(rg_ba[l, 0]), grp(rg_bx[l, 0]), grp(rg_ba[l, 1]), grp(rg_bx[l, 1])], axis=-1)
    bf = lambda a: a.astype(BF16)
    return dict(w_x=bf(w_x), w_s=bf(w_s), w_ygg=bf(w_ygg), w_q=bf(w_q), w_q_rot=bf(w_q_rot),
                w_k=bf(w_k), w_v=bf(w_v), w_gate=bf(w_gate), b_gate=b_gate)


def kernel(x_prompt, x_sample, cache_ckv, cache_krope, state_rnn, c, c_ctx, w_ada, b_ada, w_in, conv_w, conv_b, rg_wa, rg_ba, rg_wx, rg_bx, rg_lambda, w_rnn_out, q_norm_g, w_qb, kv_norm_g, w_kvb, w_attn_out, w_out, ln1_g, ln1_b, ln2_g, ln2_b, w1_dense, w3_dense, w2_dense, w_router, b_router, w1_exp, w3_exp, w2_exp):
    nb, seq, _ = x_prompt.shape
    db, dseq, _ = x_sample.shape
    n_ctx = nb * seq
    n_lat = db * dseq
    assert n_ctx % _ROW_TILE == 0 and dseq % _ROW_TILE == 0 and db + 1 <= 8
    bf = lambda a: a.astype(BF16)

    cond8 = jnp.concatenate([c_ctx[None, :], c, jnp.zeros((8 - 1 - db, D_MODEL), F32)], axis=0)
    mods = _ada_mods(cond8, w_ada, b_ada)
    cos_p, sin_p = _rope_tables(dseq // GRID_W)
    pad_kr = lambda a: jnp.pad(a, ((0, 0),) * (a.ndim - 1) + ((ROPE_OFF, HEAD_PAD - ROPE_OFF - QK_ROPE),))
    past_kr = pad_kr(cache_krope)

    x = jnp.concatenate([x_prompt.reshape(n_ctx, D_MODEL), x_sample.reshape(n_lat, D_MODEL)], axis=0)
    new_ckv, new_kr, new_h = [], [], []
    for l in range(DEPTH):
        lw = _layer_weights(l, w_in, rg_wa, rg_ba, rg_wx, rg_bx, w_qb, w_kvb)
        tiles = (n_ctx, dseq)
        xr, cq, ckv, kr, krr = _in_proj(x, mods[l], lw['w_x'], lw['w_s'], q_norm_g[l][None, :],
                                        kv_norm_g[l][None, :], *tiles)
        new_ckv.append(ckv[:n_ctx].reshape(nb, seq, KV_LORA))
        new_kr.append(kr[:n_ctx, ROPE_OFF:ROPE_OFF + QK_ROPE].reshape(nb, seq, QK_ROPE))

        rnn_args = (conv_w[l], conv_b[l][None, :], lw['w_gate'], lw['b_gate'], rg_lambda[l])
        hsum, h_last = _rglru(xr, 0, nb, seq, *rnn_args, emit_last=True)
        (hsum,) = _rglru(xr, n_ctx, db, dseq, *rnn_args, h0=state_rnn[:, l], prev=hsum)
        new_h.append(h_last)

        att_w = (lw['w_q'], lw['w_k'], lw['w_v'])
        attn = _attention(cq, ckv, kr, 0, nb, seq, *att_w)
        attn = _attention(cq, ckv, kr, n_ctx, db, dseq, *att_w, prev=attn,
                          lat=(krr, l, cache_ckv, past_kr, cos_p, sin_p, lw['w_q_rot']))

        x = _mix(x, hsum, attn, mods[l], lw['w_ygg'], bf(w_rnn_out[l]), bf(w_attn_out[l]), bf(w_out[l]),
                 ln1_g[l][None, :], ln1_b[l][None, :], *tiles)
        j = l // 2
        if l % 2 == 0:
            x = _ffn_dense(x, mods[l], bf(w1_dense[j]), bf(w3_dense[j]), bf(w2_dense[j]),
                           ln2_g[l][None, :], ln2_b[l][None, :], *tiles)
        else:
            x = _ffn_moe(x, mods[l], w_router[j], b_router[j][None, :], bf(w1_exp[j]), bf(w3_exp[j]),
                         bf(w2_exp[j]), ln2_g[l][None, :], ln2_b[l][None, :], *tiles)

    return (x[:n_ctx].reshape(nb, seq, D_MODEL), x[n_ctx:].reshape(db, dseq, D_MODEL),
            jnp.stack(new_ckv, axis=1), jnp.stack(new_kr, axis=1), jnp.stack(new_h, axis=1))
```

```python
import functools
import math

import jax
import jax.numpy as jnp
from jax import lax
from jax.experimental import pallas as pl
from jax.experimental.pallas import tpu as pltpu

F32 = jnp.float32
BF16 = jnp.bfloat16

D_MODEL = 1024
DEPTH = 4
GRID_W = 64
D_RNN = D_MODEL
RNN_BLOCKS = 16
RNN_BW = D_RNN // RNN_BLOCKS
CONV_W = 4
RG_C = 8.0
N_HEADS = 16
QK_NOPE = 64
QK_ROPE = 32
V_HEAD = 64
Q_LORA = 256
KV_LORA = 128
ROPE_AXIS = QK_ROPE // 2
ROPE_THETA = 10000.0
D_FF = 2816
N_EXPERTS = 8
D_EXPERT = 1408
ALPHA = (2 * DEPTH) ** 0.25
LN_EPS = 1e-5
RMS_EPS = 1e-6

LANE = 128
HEAD_PAD = LANE
ROPE_OFF = QK_NOPE
RNN_GROUP = 256
VMEM_LIMIT = 56 * 1024 * 1024

_ROW_TILE = 512
_MOE_ROW_TILE = 512
_ATT_Q_TILE = 512
_FF_CHUNK = 1408

_COL_Q = 2 * D_RNN
_COL_KR = _COL_Q + Q_LORA + KV_LORA
_COL_GATE = _COL_KR + QK_ROPE


def _dot(a, b):
    return jnp.dot(a, b, preferred_element_type=F32)


def _layer_norm(y, g, b):
    mu = jnp.mean(y, axis=-1, keepdims=True)
    d = y - mu
    var = jnp.mean(d * d, axis=-1, keepdims=True)
    return d * lax.rsqrt(var + LN_EPS) * g + b


def _rms_norm(y, g):
    return y * lax.rsqrt(jnp.mean(y * y, axis=-1, keepdims=True) + RMS_EPS) * g


def _params(*sem):
    return pltpu.CompilerParams(dimension_semantics=sem, vmem_limit_bytes=VMEM_LIMIT)


def _fixed_spec(shape, lead=(), tail=None):
    tail = (0,) * len(shape) if tail is None else tuple(tail)
    index = tuple(lead) + tail
    return pl.BlockSpec((None,) * len(lead) + tuple(shape), lambda *_: index,
                        pipeline_mode=pl.Buffered(1))


def _mod_spec(layer, tm, n_ctx_rows, rows_per_latent):
    n_ctx_tiles = n_ctx_rows // tm
    per = rows_per_latent // tm

    def index(i, *_):
        return (layer, jnp.where(i < n_ctx_tiles, 0, 1 + (i - n_ctx_tiles) // per), 0, 0)

    return pl.BlockSpec((None, None, 6, D_MODEL), index)


def _cast_once(pairs):
    @pl.when(pl.program_id(0) == 0)
    def _():
        for src, dst in pairs:
            dst[...] = src[...].astype(BF16)


def _mods_kernel(c_ref, w_ref, b_ref, o_ref):
    s = jax.nn.silu(c_ref[...]).astype(BF16)
    o_ref[...] = _dot(s, w_ref[...].astype(BF16)) + b_ref[...]


def _ada_mods(cond8, w_ada, b_ada):
    out = pl.pallas_call(
        _mods_kernel,
        grid=(DEPTH, 6),
        in_specs=[
            pl.BlockSpec((8, D_MODEL), lambda l, j: (0, 0)),
            pl.BlockSpec((None, D_MODEL, D_MODEL), lambda l, j: (l, 0, j)),
            pl.BlockSpec((None, 1, D_MODEL), lambda l, j: (l, 0, j)),
        ],
        out_specs=pl.BlockSpec((None, 8, D_MODEL), lambda l, j: (l, 0, j)),
        out_shape=jax.ShapeDtypeStruct((DEPTH, 8, 6 * D_MODEL), F32),
        compiler_params=_params("arbitrary", "arbitrary"),
        name="ada_mods",
    )(cond8, w_ada, b_ada.reshape(DEPTH, 1, 6 * D_MODEL))
    return out.reshape(DEPTH, 8, 6, D_MODEL)


_S_COLS = Q_LORA + KV_LORA + 2 * HEAD_PAD


def _inproj_kernel(x_ref, m_ref, wx_ref, ws_ref, qg_ref, kvg_ref,
                   xr_ref, cq_ref, ckv_ref, kr_ref, krr_ref, wx_bf):
    _cast_once([(wx_ref, wx_bf)])
    h = (x_ref[...] * (1.0 + m_ref[1:2, :]) + m_ref[0:1, :]).astype(BF16)
    xr_ref[...] = _dot(h, wx_bf[...])
    s = _dot(h, ws_ref[...])
    cq_ref[...] = _rms_norm(s[:, :Q_LORA], qg_ref[...]).astype(BF16)
    ckv_ref[...] = _rms_norm(s[:, Q_LORA:Q_LORA + KV_LORA], kvg_ref[...])
    kr_ref[...] = s[:, Q_LORA + KV_LORA:Q_LORA + KV_LORA + HEAD_PAD]
    krr_ref[...] = s[:, Q_LORA + KV_LORA + HEAD_PAD:]


def _in_proj(layer, x, mods, w_in, w_s, q_g, kv_g, n_ctx_rows, rows_per_latent):
    n_tok = x.shape[0]
    tm = _ROW_TILE
    row = lambda w: pl.BlockSpec((tm, w), lambda i: (i, 0))
    return pl.pallas_call(
        _inproj_kernel,
        grid=(n_tok // tm,),
        in_specs=[
            row(D_MODEL),
            _mod_spec(layer, tm, n_ctx_rows, rows_per_latent),
            _fixed_spec((D_MODEL, D_RNN), (layer,)),
            _fixed_spec((D_MODEL, _S_COLS), (layer,)),
            _fixed_spec((1, Q_LORA), (layer,)),
            _fixed_spec((1, KV_LORA), (layer,)),
        ],
        out_specs=[row(D_RNN), row(Q_LORA), row(KV_LORA), row(HEAD_PAD), row(HEAD_PAD)],
        out_shape=[
            jax.ShapeDtypeStruct((n_tok, D_RNN), F32),
            jax.ShapeDtypeStruct((n_tok, Q_LORA), BF16),
            jax.ShapeDtypeStruct((n_tok, KV_LORA), F32),
            jax.ShapeDtypeStruct((n_tok, HEAD_PAD), F32),
            jax.ShapeDtypeStruct((n_tok, HEAD_PAD), F32),
        ],
        scratch_shapes=[pltpu.VMEM((D_MODEL, D_RNN), BF16)],
        compiler_params=_params("arbitrary"),
        name="in_proj",
    )(x, mods, w_in, w_s, q_g, kv_g)


def _rglru_kernel(*refs, seq, bt, has_h0, has_prev, emit_last):
    it = iter(refs)
    x_ref, cw_ref, cb_ref, wg_ref, bg_ref, lam_ref = (next(it) for _ in range(6))
    h0_ref = next(it) if has_h0 else None
    if has_prev:
        next(it)
    hs_ref = next(it)
    hl_ref = next(it) if emit_last else None
    af_ref, bf_ref, ab_ref, bb_ref, hb_ref = (next(it) for _ in range(5))
    dg = RNN_GROUP
    a_refs, b_refs = (af_ref, ab_ref), (bf_ref, bb_ref)

    row = lax.broadcasted_iota(jnp.int32, (seq, dg), 0)
    neg_softplus = [jax.nn.softplus(-lam_ref[d:d + 1, :]) for d in range(2)]
    for b in range(bt):
        rows = slice(b * seq, (b + 1) * seq)
        x = x_ref[rows, :]
        xm1 = jnp.where(row >= 1, pltpu.roll(x, 1, 0), 0.0)
        xp1 = jnp.where(row < seq - 1, pltpu.roll(x, seq - 1, 0), 0.0)
        xp2 = jnp.where(row < seq - 2, pltpu.roll(x, seq - 2, 0), 0.0)
        xc = cb_ref[...] + xm1 * cw_ref[0:1, :]
        xc = xc + x * cw_ref[1:2, :]
        xc = xc + xp1 * cw_ref[2:3, :]
        xc = xc + xp2 * cw_ref[3:4, :]
        gates = _dot(xc.astype(BF16), wg_ref[...]) + bg_ref[...]
        for d in range(2):
            r = jax.nn.sigmoid(gates[:, (2 * d) * dg:(2 * d + 1) * dg])
            i = jax.nn.sigmoid(gates[:, (2 * d + 1) * dg:(2 * d + 2) * dg])
            log_a = -RG_C * r * neg_softplus[d]
            a = jnp.exp(log_a)
            a_refs[d][rows, :] = a
            b_refs[d][rows, :] = jnp.sqrt(jnp.tanh(-log_a) * (a * a + 1.0)) * (i * xc)

    if has_h0:
        init_f = tuple(h0_ref[b, 0:1, :] for b in range(bt))
        init_b = tuple(h0_ref[b, 1:2, :] for b in range(bt))
    else:
        zero = jnp.zeros((1, dg), F32)
        init_f = tuple(zero for _ in range(bt))
        init_b = tuple(zero for _ in range(bt))

    def step(k, carry):
        hf, hb = carry
        new_f, new_b = [], []
        for b in range(bt):
            tf = pl.ds(b * seq + k, 1)
            tb = pl.ds(b * seq + seq - 1 - k, 1)
            f = af_ref[tf, :] * hf[b] + bf_ref[tf, :]
            g = ab_ref[tb, :] * hb[b] + bb_ref[tb, :]
            hs_ref[tf, :] = f
            hb_ref[tb, :] = g
            new_f.append(f)
            new_b.append(g)
        return tuple(new_f), tuple(new_b)

    last_f, last_b = lax.fori_loop(0, seq, step, (init_f, init_b), unroll=8)
    hs_ref[...] = hs_ref[...] + hb_ref[...]
    if emit_last:
        for b in range(bt):
            hl_ref[b, 0:1, :] = last_f[b]
            hl_ref[b, 1:2, :] = last_b[b]


def _rglru(layer, xr, row_start, nseq, seq, conv_w, conv_b, w_gate, b_gate, lam, h0=None, prev=None,
           emit_last=False):
    n_tok = xr.shape[0]
    bt = 4
    dg = RNN_GROUP
    ng = D_RNN // dg
    rows = bt * seq
    assert nseq % bt == 0 and row_start % rows == 0
    off = row_start // rows
    in_specs = [
        pl.BlockSpec((rows, dg), lambda bi, g: (off + bi, g)),
        pl.BlockSpec((None, CONV_W, dg), lambda bi, g: (layer, 0, g)),
        pl.BlockSpec((None, 1, dg), lambda bi, g: (layer, 0, g)),
        pl.BlockSpec((None, None, dg, 4 * dg), lambda bi, g: (layer, g, 0, 0)),
        pl.BlockSpec((None, None, 1, 4 * dg), lambda bi, g: (layer, g, 0, 0)),
        pl.BlockSpec((None, 2, dg), lambda bi, g: (layer, 0, g)),
    ]
    args = [xr, conv_w, conv_b, w_gate, b_gate, lam]
    if h0 is not None:
        in_specs.append(pl.BlockSpec((bt, None, 2, dg), lambda bi, g: (bi, layer, 0, g)))
        args.append(h0)
    aliases = {}
    if prev is not None:
        aliases = {len(args): 0}
        in_specs.append(pl.BlockSpec(memory_space=pl.ANY))
        args.append(prev)
    out_specs = [pl.BlockSpec((rows, dg), lambda bi, g: (off + bi, g))]
    out_shape = [jax.ShapeDtypeStruct((n_tok, D_RNN), F32)]
    if emit_last:
        out_specs.append(pl.BlockSpec((bt, 2, dg), lambda bi, g: (bi, 0, g)))
        out_shape.append(jax.ShapeDtypeStruct((nseq, 2, D_RNN), F32))
    return pl.pallas_call(
        functools.partial(_rglru_kernel, seq=seq, bt=bt, has_h0=h0 is not None,
                          has_prev=prev is not None, emit_last=emit_last),
        grid=(nseq // bt, ng),
        in_specs=in_specs,
        out_specs=out_specs,
        out_shape=out_shape,
        input_output_aliases=aliases,
        scratch_shapes=[pltpu.VMEM((rows, dg), F32) for _ in range(5)],
        compiler_params=_params("arbitrary", "arbitrary"),
        name=f"rglru_{seq}",
    )(*args)


def _attn_kernel(*refs, past, tq, rope, has_prev):
    it = iter(refs)
    cq_ref, ckv_ref, kr_ref = (next(it) for _ in range(3))
    if rope:
        krr_ref, pckv_ref, pkr_ref, cos_ref, sin_ref = (next(it) for _ in range(5))
    wq_ref = next(it)
    wqr_ref = next(it) if rope else None
    wk_ref, wv_ref = next(it), next(it)
    if has_prev:
        next(it)
    o_ref = next(it)
    k_scr, v_scr = next(it), next(it)
    qi = pl.program_id(1)
    scale = 1.0 / math.sqrt(QK_NOPE + QK_ROPE)

    def fill(rows, ckv, k_rope):
        c = ckv.astype(BF16)
        k = _dot(c, wk_ref[...]) + jnp.tile(k_rope, (1, N_HEADS))
        k_scr[rows, :] = k.astype(BF16)
        v_scr[rows, :] = _dot(c, wv_ref[...]).astype(BF16)

    @pl.when(qi == 0)
    def _():
        seq = ckv_ref.shape[0]
        if rope:
            fill(pl.ds(0, past), pckv_ref[...], pkr_ref[...])
            fill(pl.ds(past, seq), ckv_ref[...],
                 kr_ref[...] * cos_ref[...] + krr_ref[...] * sin_ref[...])
        else:
            fill(pl.ds(0, seq), ckv_ref[...], kr_ref[...])

    cq = cq_ref[...]
    qa = _dot(cq, wq_ref[...])
    if rope:
        qb = _dot(cq, wqr_ref[...])
        q0 = pl.multiple_of(qi * tq, tq)
        cos_t = cos_ref[pl.ds(q0, tq), :]
        sin_t = sin_ref[pl.ds(q0, tq), :]
    for j in range(N_HEADS // 2):
        pair = None
        for h in (2 * j, 2 * j + 1):
            cols = slice(h * HEAD_PAD, (h + 1) * HEAD_PAD)
            qh = qa[:, cols]
            if rope:
                qh = qh * cos_t + qb[:, cols] * sin_t
            s = lax.dot_general(qh.astype(BF16), k_scr[:, cols], (((1,), (1,)), ((), ())),
                                preferred_element_type=F32)
            e = jnp.exp((s - jnp.max(s, axis=-1, keepdims=True)) * scale)
            denom = jnp.sum(e, axis=-1, keepdims=True)
            o = _dot(e.astype(BF16), v_scr[:, cols]) * (1.0 / denom)
            pair = o if pair is None else pair + o
        o_ref[:, j * LANE:(j + 1) * LANE] = pair.astype(BF16)


def _attention(layer, cq, ckv, kr, row_start, nseq, seq, w_q, w_k, w_v, lat=None, prev=None):
    n_tok = cq.shape[0]
    rope = lat is not None
    tq = min(_ATT_Q_TILE, seq)
    assert seq % tq == 0 and row_start % seq == 0
    past = lat[1].shape[2] if rope else 0
    total = past + seq
    hp = N_HEADS * HEAD_PAD
    nq = seq // tq
    s_off, q_off = row_start // seq, row_start // tq
    full = lambda w: pl.BlockSpec((seq, w), lambda b, q: (s_off + b, 0))
    q_rows = lambda w: pl.BlockSpec((tq, w), lambda b, q: (q_off + b * nq + q, 0))
    in_specs = [q_rows(Q_LORA), full(KV_LORA), full(HEAD_PAD)]
    args = [cq, ckv, kr]
    if rope:
        krr, pckv, pkr, cos, sin, w_q_rot = lat
        in_specs += [full(HEAD_PAD),
                     pl.BlockSpec((None, None, past, KV_LORA), lambda b, q: (b, layer, 0, 0)),
                     pl.BlockSpec((None, None, past, HEAD_PAD), lambda b, q: (b, layer, 0, 0)),
                     _fixed_spec((seq, HEAD_PAD)), _fixed_spec((seq, HEAD_PAD))]
        args += [krr, pckv, pkr, cos, sin]
    in_specs.append(_fixed_spec((Q_LORA, hp), (layer,)))
    args.append(w_q)
    if rope:
        in_specs.append(_fixed_spec((Q_LORA, hp), (layer,)))
        args.append(w_q_rot)
    in_specs += [_fixed_spec((KV_LORA, hp), (layer,)), _fixed_spec((KV_LORA, hp), (layer,))]
    args += [w_k, w_v]
    aliases = {}
    if prev is not None:
        aliases = {len(args): 0}
        in_specs.append(pl.BlockSpec(memory_space=pl.ANY))
        args.append(prev)
    return pl.pallas_call(
        functools.partial(_attn_kernel, past=past, tq=tq, rope=rope, has_prev=prev is not None),
        grid=(nseq, nq),
        in_specs=in_specs,
        out_specs=q_rows(N_HEADS * V_HEAD),
        out_shape=jax.ShapeDtypeStruct((n_tok, N_HEADS * V_HEAD), BF16),
        input_output_aliases=aliases,
        scratch_shapes=[pltpu.VMEM((total, hp), BF16), pltpu.VMEM((total, hp), BF16)],
        compiler_params=_params("arbitrary", "arbitrary"),
        name=f"attn_{seq}",
    )(*args)


def _mix_kernel(x_ref, hs_ref, at_ref, m_ref, wy_ref, wgg_ref, wro_ref, wao_ref, wo_ref, g_ref, b_ref,
                o_ref, wy_bf, wro_bf, wao_bf, wo_bf):
    _cast_once([(wy_ref, wy_bf), (wro_ref, wro_bf), (wao_ref, wao_bf), (wo_ref, wo_bf)])
    x = x_ref[...]
    h = (x * (1.0 + m_ref[1:2, :]) + m_ref[0:1, :]).astype(BF16)
    u = (hs_ref[...] * jax.nn.gelu(_dot(h, wy_bf[...]))).astype(BF16)
    rnn_out = _dot(u, wro_bf[...])
    att_out = _dot(at_ref[...], wao_bf[...])
    gg = _dot(h, wgg_ref[...])
    merged = jax.nn.sigmoid(gg[:, :D_MODEL]) * rnn_out + jax.nn.sigmoid(gg[:, D_MODEL:]) * att_out
    m = _dot(merged.astype(BF16), wo_bf[...])
    o_ref[...] = _layer_norm(ALPHA * x + m_ref[2:3, :] * m, g_ref[...], b_ref[...])


def _mix(layer, x, hsum, attn, mods, w_in, w_gg, w_ro, w_ao, w_o, ln_g, ln_b, n_ctx_rows,
         rows_per_latent):
    n_tok = x.shape[0]
    tm = _ROW_TILE
    row = pl.BlockSpec((tm, D_MODEL), lambda i: (i, 0))
    sq = (D_MODEL, D_MODEL)
    return pl.pallas_call(
        _mix_kernel,
        grid=(n_tok // tm,),
        in_specs=[row, row, row, _mod_spec(layer, tm, n_ctx_rows, rows_per_latent),
                  _fixed_spec((D_MODEL, D_RNN), (layer,), (0, 1)),
                  _fixed_spec((D_MODEL, 2 * D_MODEL), (layer,)),
                  _fixed_spec(sq, (layer,)), _fixed_spec(sq, (layer,)), _fixed_spec(sq, (layer,)),
                  _fixed_spec((1, D_MODEL), (layer,)), _fixed_spec((1, D_MODEL), (layer,))],
        out_specs=row,
        out_shape=jax.ShapeDtypeStruct((n_tok, D_MODEL), F32),
        scratch_shapes=[pltpu.VMEM(sq, BF16) for _ in range(4)],
        compiler_params=_params("arbitrary"),
        name="mix_out",
    )(x, hsum, attn, mods, w_in, w_gg, w_ro, w_ao, w_o, ln_g, ln_b)


def _ffn_kernel(x_ref, m_ref, w1_ref, w3_ref, w2_ref, g_ref, b_ref, o_ref):
    x = x_ref[...]
    h = (x * (1.0 + m_ref[4:5, :]) + m_ref[3:4, :]).astype(BF16)
    f = None
    for c in range(D_FF // _FF_CHUNK):
        cols = slice(c * _FF_CHUNK, (c + 1) * _FF_CHUNK)
        act = (jax.nn.silu(_dot(h, w1_ref[:, cols])) * _dot(h, w3_ref[:, cols])).astype(BF16)
        y = _dot(act, w2_ref[cols, :])
        f = y if f is None else f + y
    o_ref[...] = _layer_norm(ALPHA * x + m_ref[5:6, :] * f, g_ref[...], b_ref[...])


def _ffn_dense(layer, x, mods, w1, w3, w2, ln_g, ln_b, n_ctx_rows, rows_per_latent):
    n_tok = x.shape[0]
    tm = _ROW_TILE
    j = layer // 2
    row = pl.BlockSpec((tm, D_MODEL), lambda i: (i, 0))
    return pl.pallas_call(
        _ffn_kernel,
        grid=(n_tok // tm,),
        in_specs=[row, _mod_spec(layer, tm, n_ctx_rows, rows_per_latent),
                  _fixed_spec((D_MODEL, D_FF), (j,)), _fixed_spec((D_MODEL, D_FF), (j,)),
                  _fixed_spec((D_FF, D_MODEL), (j,)),
                  _fixed_spec((1, D_MODEL), (layer,)), _fixed_spec((1, D_MODEL), (layer,))],
        out_specs=row,
        out_shape=jax.ShapeDtypeStruct((n_tok, D_MODEL), F32),
        compiler_params=_params("arbitrary"),
        name="ffn_dense",
    )(x, mods, w1, w3, w2, ln_g, ln_b)


def _split_bf16(a):
    hi = a.astype(BF16)
    return hi, (a - hi.astype(F32)).astype(BF16)


def _top2_gates(logits):
    lane = lax.broadcasted_iota(jnp.int32, logits.shape, 1)
    big = jnp.int32(N_EXPERTS)
    v1 = jnp.max(logits, axis=-1, keepdims=True)
    i1 = jnp.min(jnp.where(logits == v1, lane, big), axis=-1, keepdims=True)
    rest = jnp.where(lane == i1, -jnp.inf, logits)
    v2 = jnp.max(rest, axis=-1, keepdims=True)
    i2 = jnp.min(jnp.where(rest == v2, lane, big), axis=-1, keepdims=True)
    e2 = jnp.exp(v2 - v1)
    w1 = 1.0 / (1.0 + e2)
    w2 = e2 / (1.0 + e2)
    return jnp.where(lane == i1, w1, 0.0) + jnp.where(lane == i2, w2, 0.0)


def _moe_kernel(x_ref, m_ref, wr_ref, br_ref, w1_ref, w3_ref, w2_ref, g_ref, b_ref, o_ref,
                h_scr, gate_scr, acc_scr):
    e = pl.program_id(1)

    @pl.when(e == 0)
    def _():
        h = x_ref[...] * (1.0 + m_ref[4:5, :]) + m_ref[3:4, :]
        h_scr[...] = h.astype(BF16)
        h_hi, h_lo = _split_bf16(h)
        w_hi, w_lo = _split_bf16(wr_ref[...])
        logits = _dot(h_hi, w_hi) + (_dot(h_hi, w_lo) + _dot(h_lo, w_hi)) + br_ref[...]
        gate_scr[...] = _top2_gates(logits)
        acc_scr[...] = jnp.zeros_like(acc_scr)

    h = h_scr[...]
    act = (jax.nn.silu(_dot(h, w1_ref[...])) * _dot(h, w3_ref[...])).astype(BF16)
    y = _dot(act, w2_ref[...])
    lane = lax.broadcasted_iota(jnp.int32, gate_scr.shape, 1)
    gate = jnp.sum(jnp.where(lane == e, gate_scr[...], 0.0), axis=-1, keepdims=True)
    acc_scr[...] += gate * y

    @pl.when(e == N_EXPERTS - 1)
    def _():
        o_ref[...] = _layer_norm(ALPHA * x_ref[...] + m_ref[5:6, :] * acc_scr[...], g_ref[...], b_ref[...])


def _ffn_moe(layer, x, mods, w_router, b_router, w1, w3, w2, ln_g, ln_b, n_ctx_rows, rows_per_latent):
    n_tok = x.shape[0]
    tm = _MOE_ROW_TILE
    j = layer // 2
    row = pl.BlockSpec((tm, D_MODEL), lambda i, e: (i, 0))
    up = pl.BlockSpec((None, None, D_MODEL, D_EXPERT), lambda i, e: (j, e, 0, 0))
    return pl.pallas_call(
        _moe_kernel,
        grid=(n_tok // tm, N_EXPERTS),
        in_specs=[row, _mod_spec(layer, tm, n_ctx_rows, rows_per_latent),
                  _fixed_spec((D_MODEL, N_EXPERTS), (j,)), _fixed_spec((1, N_EXPERTS), (j,)),
                  up, up, pl.BlockSpec((None, None, D_EXPERT, D_MODEL), lambda i, e: (j, e, 0, 0)),
                  _fixed_spec((1, D_MODEL), (layer,)), _fixed_spec((1, D_MODEL), (layer,))],
        out_specs=row,
        out_shape=jax.ShapeDtypeStruct((n_tok, D_MODEL), F32),
        scratch_shapes=[pltpu.VMEM((tm, D_MODEL), BF16), pltpu.VMEM((tm, N_EXPERTS), F32),
                        pltpu.VMEM((tm, D_MODEL), F32)],
        compiler_params=_params("arbitrary", "arbitrary"),
        name="ffn_moe",
    )(x, mods, w_router, b_router, w1, w3, w2, ln_g, ln_b)


def _rot_cols(w):
    ws = w.reshape(w.shape[:-1] + (2, 2, ROPE_AXIS // 2))
    return jnp.stack([-ws[..., 1, :], ws[..., 0, :]], axis=-2).reshape(w.shape)


def _pad_rope_cols(w):
    return jnp.pad(w, ((0, 0),) * (w.ndim - 1) + ((ROPE_OFF, HEAD_PAD - ROPE_OFF - QK_ROPE),))


def _rope_tables(rows):
    r = jnp.repeat(jnp.arange(rows, dtype=F32), GRID_W)
    col = jnp.tile(jnp.arange(GRID_W, dtype=F32), rows)
    freqs = ROPE_THETA ** (-jnp.arange(0, ROPE_AXIS, 2, dtype=F32) / ROPE_AXIS)
    ar = r[:, None] * freqs
    ac = col[:, None] * freqs
    ang = jnp.concatenate([ar, ar, ac, ac], axis=-1)
    cos, sin = jnp.cos(ang), jnp.sin(ang)
    n = cos.shape[0]
    cos_p = jnp.concatenate([jnp.ones((n, ROPE_OFF), F32), cos,
                             jnp.zeros((n, HEAD_PAD - ROPE_OFF - QK_ROPE), F32)], axis=-1)
    return cos_p, _pad_rope_cols(sin)


def _prepared_weights(w_in, rg_wa, rg_ba, rg_wx, rg_bx, w_qb, w_kvb):
    bf = lambda a: a.astype(BF16)
    w_kr = w_in[:, :, _COL_KR:_COL_KR + QK_ROPE]
    w_s = jnp.concatenate([w_in[:, :, _COL_Q:_COL_KR], _pad_rope_cols(w_kr),
                           _pad_rope_cols(_rot_cols(w_kr))], axis=-1)
    w_gg = w_in[:, :, _COL_GATE:]

    wq = w_qb.reshape(DEPTH, Q_LORA, N_HEADS, QK_NOPE + QK_ROPE)
    q_nope, q_rope = wq[..., :QK_NOPE], wq[..., QK_NOPE:]
    tail = jnp.zeros((DEPTH, Q_LORA, N_HEADS, HEAD_PAD - QK_NOPE - QK_ROPE), F32)
    hp = N_HEADS * HEAD_PAD
    w_q = jnp.concatenate([q_nope, q_rope, tail], axis=-1).reshape(DEPTH, Q_LORA, hp)
    w_q_rot = jnp.concatenate([jnp.zeros_like(q_nope), _rot_cols(q_rope), tail], axis=-1)
    w_q_rot = w_q_rot.reshape(DEPTH, Q_LORA, hp)
    wkv = w_kvb.reshape(DEPTH, KV_LORA, N_HEADS, QK_NOPE + V_HEAD)
    k_nope, v = wkv[..., :QK_NOPE], wkv[..., QK_NOPE:]
    w_k = jnp.concatenate([k_nope, jnp.zeros((DEPTH, KV_LORA, N_HEADS, HEAD_PAD - QK_NOPE), F32)], axis=-1)
    w_k = w_k.reshape(DEPTH, KV_LORA, hp)
    vz = jnp.zeros_like(v)
    odd = (jnp.arange(N_HEADS) % 2 == 1)[None, None, :, None]
    w_v = jnp.where(odd, jnp.concatenate([vz, v], axis=-1), jnp.concatenate([v, vz], axis=-1))
    w_v = w_v.reshape(DEPTH, KV_LORA, hp)

    nblk = RNN_GROUP // RNN_BW
    ng = D_RNN // RNN_GROUP
    eye = jnp.eye(nblk, dtype=F32)

    def block_diag(w):
        wg = w.reshape(DEPTH, ng, nblk, RNN_BW, RNN_BW)
        return jnp.einsum('lgnjk,nm->lgnjmk', wg, eye).reshape(DEPTH, ng, RNN_GROUP, RNN_GROUP)

    w_gate = jnp.concatenate([block_diag(rg_wa[:, 0]), block_diag(rg_wx[:, 0]),
                              block_diag(rg_wa[:, 1]), block_diag(rg_wx[:, 1])], axis=-1)
    grp = lambda b: b.reshape(DEPTH, ng, 1, RNN_GROUP)
    b_gate = jnp.concatenate([grp(rg_ba[:, 0]), grp(rg_bx[:, 0]), grp(rg_ba[:, 1]), grp(rg_bx[:, 1])],
                             axis=-1)
    return dict(w_s=bf(w_s), w_gg=bf(w_gg), w_q=bf(w_q), w_q_rot=bf(w_q_rot), w_k=bf(w_k), w_v=bf(w_v),
                w_gate=bf(w_gate), b_gate=b_gate)


def kernel(x_prompt, x_sample, cache_ckv, cache_krope, state_rnn, c, c_ctx, w_ada, b_ada, w_in, conv_w, conv_b, rg_wa, rg_ba, rg_wx, rg_bx, rg_lambda, w_rnn_out, q_norm_g, w_qb, kv_norm_g, w_kvb, w_attn_out, w_out, ln1_g, ln1_b, ln2_g, ln2_b, w1_dense, w3_dense, w2_dense, w_router, b_router, w1_exp, w3_exp, w2_exp):
    nb, seq, _ = x_prompt.shape
    db, dseq, _ = x_sample.shape
    n_ctx = nb * seq
    n_lat = db * dseq
    assert n_ctx % _ROW_TILE == 0 and dseq % _ROW_TILE == 0 and db + 1 <= 8
    bf = lambda a: a.astype(BF16)
    vec = lambda a: a.reshape(a.shape[0], 1, a.shape[-1])

    cond8 = jnp.concatenate([c_ctx[None, :], c, jnp.zeros((8 - 1 - db, D_MODEL), F32)], axis=0)
    mods = _ada_mods(cond8, w_ada, b_ada)
    cos_p, sin_p = _rope_tables(dseq // GRID_W)
    past_kr = _pad_rope_cols(cache_krope)
    pw = _prepared_weights(w_in, rg_wa, rg_ba, rg_wx, rg_bx, w_qb, w_kvb)
    w1d, w3d, w2d = bf(w1_dense), bf(w3_dense), bf(w2_dense)
    w1e, w3e, w2e = bf(w1_exp), bf(w3_exp), bf(w2_exp)
    qg, kvg, cb = vec(q_norm_g), vec(kv_norm_g), vec(conv_b)
    l1g, l1b, l2g, l2b = vec(ln1_g), vec(ln1_b), vec(ln2_g), vec(ln2_b)
    br = vec(b_router)
    tiles = (n_ctx, dseq)

    x = jnp.concatenate([x_prompt.reshape(n_ctx, D_MODEL), x_sample.reshape(n_lat, D_MODEL)], axis=0)
    new_ckv, new_kr, new_h = [], [], []
    for l in range(DEPTH):
        xr, cq, ckv, kr, krr = _in_proj(l, x, mods, w_in, pw['w_s'], qg, kvg, *tiles)
        new_ckv.append(ckv[:n_ctx].reshape(nb, seq, KV_LORA))
        new_kr.append(kr[:n_ctx, ROPE_OFF:ROPE_OFF + QK_ROPE].reshape(nb, seq, QK_ROPE))

        rnn_args = (conv_w, cb, pw['w_gate'], pw['b_gate'], rg_lambda)
        hsum, h_last = _rglru(l, xr, 0, nb, seq, *rnn_args, emit_last=True)
        (hsum,) = _rglru(l, xr, n_ctx, db, dseq, *rnn_args, h0=state_rnn, prev=hsum)
        new_h.append(h_last)

        att_w = (pw['w_q'], pw['w_k'], pw['w_v'])
        attn = _attention(l, cq, ckv, kr, 0, nb, seq, *att_w)
        attn = _attention(l, cq, ckv, kr, n_ctx, db, dseq, *att_w, prev=attn,
                          lat=(krr, cache_ckv, past_kr, cos_p, sin_p, pw['w_q_rot']))

        x = _mix(l, x, hsum, attn, mods, w_in, pw['w_gg'], w_rnn_out, w_attn_out, w_out, l1g, l1b, *tiles)
        if l % 2 == 0:
            x = _ffn_dense(l, x, mods, w1d, w3d, w2d, l2g, l2b, *tiles)
        else:
            x = _ffn_moe(l, x, mods, w_router, br, w1e, w3e, w2e, l2g, l2b, *tiles)

    return (x[:n_ctx].reshape(nb, seq, D_MODEL), x[n_ctx:].reshape(db, dseq, D_MODEL),
            jnp.stack(new_ckv, axis=1), jnp.stack(new_kr, axis=1), jnp.stack(new_h, axis=1))
```

```python
import functools
import math

import jax
import jax.numpy as jnp
from jax import lax
from jax.experimental import pallas as pl
from jax.experimental.pallas import tpu as pltpu

F32 = jnp.float32
BF16 = jnp.bfloat16

D_MODEL = 1024
DEPTH = 4
GRID_W = 64
D_RNN = D_MODEL
RNN_BLOCKS = 16
RNN_BW = D_RNN // RNN_BLOCKS
CONV_W = 4
RG_C = 8.0
N_HEADS = 16
QK_NOPE = 64
QK_ROPE = 32
V_HEAD = 64
Q_LORA = 256
KV_LORA = 128
ROPE_AXIS = QK_ROPE // 2
ROPE_THETA = 10000.0
D_FF = 2816
N_EXPERTS = 8
D_EXPERT = 1408
ALPHA = (2 * DEPTH) ** 0.25
LN_EPS = 1e-5
RMS_EPS = 1e-6

LANE = 128
SUBLANE = 8
HEAD_PAD = LANE
ROPE_OFF = QK_NOPE
RNN_GROUP = 256
VMEM_LIMIT = 56 * 1024 * 1024

_ROW_TILE = 512
_MOE_ROW_TILE = 512
_ATT_Q_TILE = 512
_FF_CHUNK = 1408

_COL_Q = 2 * D_RNN
_COL_KR = _COL_Q + Q_LORA + KV_LORA
_COL_GATE = _COL_KR + QK_ROPE


def _dot(a, b):
    return jnp.dot(a, b, preferred_element_type=F32)


def _layer_norm(y, g, b):
    mu = jnp.mean(y, axis=-1, keepdims=True)
    d = y - mu
    var = jnp.mean(d * d, axis=-1, keepdims=True)
    return d * lax.rsqrt(var + LN_EPS) * g + b


def _rms_norm(y, g):
    return y * lax.rsqrt(jnp.mean(y * y, axis=-1, keepdims=True) + RMS_EPS) * g


def _params(*sem):
    return pltpu.CompilerParams(dimension_semantics=sem, vmem_limit_bytes=VMEM_LIMIT)


def _fixed_spec(shape, lead=(), tail=None):
    tail = (0,) * len(shape) if tail is None else tuple(tail)
    index = tuple(lead) + tail
    return pl.BlockSpec((None,) * len(lead) + tuple(shape), lambda *_: index,
                        pipeline_mode=pl.Buffered(1))


def _mod_spec(layer, tm, n_ctx_rows, rows_per_latent):
    n_ctx_tiles = n_ctx_rows // tm
    per = rows_per_latent // tm

    def index(i, *_):
        return (layer, jnp.where(i < n_ctx_tiles, 0, 1 + (i - n_ctx_tiles) // per), 0, 0)

    return pl.BlockSpec((None, None, 6, D_MODEL), index)


def _cast_once(pairs):
    @pl.when(pl.program_id(0) == 0)
    def _():
        for src, dst in pairs:
            dst[...] = src[...].astype(BF16)


def _mods_kernel(c_ref, w_ref, b_ref, o_ref):
    s = jax.nn.silu(c_ref[...]).astype(BF16)
    o_ref[...] = _dot(s, w_ref[...].astype(BF16)) + b_ref[...]


def _ada_mods(cond8, w_ada, b_ada):
    out = pl.pallas_call(
        _mods_kernel,
        grid=(DEPTH, 6),
        in_specs=[
            pl.BlockSpec((8, D_MODEL), lambda l, j: (0, 0)),
            pl.BlockSpec((None, D_MODEL, D_MODEL), lambda l, j: (l, 0, j)),
            pl.BlockSpec((None, 1, D_MODEL), lambda l, j: (l, 0, j)),
        ],
        out_specs=pl.BlockSpec((None, 8, D_MODEL), lambda l, j: (l, 0, j)),
        out_shape=jax.ShapeDtypeStruct((DEPTH, 8, 6 * D_MODEL), F32),
        compiler_params=_params("arbitrary", "arbitrary"),
        name="ada_mods",
    )(cond8, w_ada, b_ada.reshape(DEPTH, 1, 6 * D_MODEL))
    return out.reshape(DEPTH, 8, 6, D_MODEL)


_S_COLS = Q_LORA + KV_LORA + 2 * HEAD_PAD


def _inproj_kernel(x_ref, m_ref, wx_ref, ws_ref, qg_ref, kvg_ref,
                   xr_ref, cq_ref, ckv_ref, kr_ref, krr_ref):
    h = (x_ref[...] * (1.0 + m_ref[1:2, :]) + m_ref[0:1, :]).astype(BF16)
    xr_ref[...] = _dot(h, wx_ref[...])
    s = _dot(h, ws_ref[...])
    cq_ref[...] = _rms_norm(s[:, :Q_LORA], qg_ref[...]).astype(BF16)
    ckv_ref[...] = _rms_norm(s[:, Q_LORA:Q_LORA + KV_LORA], kvg_ref[...])
    kr_ref[...] = s[:, Q_LORA + KV_LORA:Q_LORA + KV_LORA + HEAD_PAD]
    krr_ref[...] = s[:, Q_LORA + KV_LORA + HEAD_PAD:]


def _in_proj(layer, x, mods, w_x, w_s, q_g, kv_g, n_ctx_rows, rows_per_latent):
    n_tok = x.shape[0]
    tm = _ROW_TILE
    row = lambda w: pl.BlockSpec((tm, w), lambda i: (i, 0))
    return pl.pallas_call(
        _inproj_kernel,
        grid=(n_tok // tm,),
        in_specs=[
            row(D_MODEL),
            _mod_spec(layer, tm, n_ctx_rows, rows_per_latent),
            _fixed_spec((D_MODEL, D_RNN), (layer,)),
            _fixed_spec((D_MODEL, _S_COLS), (layer,)),
            _fixed_spec((1, Q_LORA), (layer,)),
            _fixed_spec((1, KV_LORA), (layer,)),
        ],
        out_specs=[row(D_RNN), row(Q_LORA), row(KV_LORA), row(HEAD_PAD), row(HEAD_PAD)],
        out_shape=[
            jax.ShapeDtypeStruct((n_tok, D_RNN), F32),
            jax.ShapeDtypeStruct((n_tok, Q_LORA), BF16),
            jax.ShapeDtypeStruct((n_tok, KV_LORA), F32),
            jax.ShapeDtypeStruct((n_tok, HEAD_PAD), F32),
            jax.ShapeDtypeStruct((n_tok, HEAD_PAD), F32),
        ],
        compiler_params=_params("arbitrary"),
        name="in_proj",
    )(x, mods, w_x, w_s, q_g, kv_g)


def _segment_pitch(seg):
    assert seg % SUBLANE == 0
    return seg if (seg // SUBLANE) % 2 == 1 else seg + SUBLANE


def _rglru_kernel(*refs, seq, bt, has_h0, emit_last):
    it = iter(refs)
    x_ref, cw_ref, cb_ref, wg_ref, bg_ref, lam_ref = (next(it) for _ in range(6))
    h0_ref = next(it) if has_h0 else None
    hs_ref = next(it)
    hl_ref = next(it) if emit_last else None
    a_refs = (next(it), next(it))
    b_refs = (next(it), next(it))
    dg = RNN_GROUP
    nslab = dg // LANE
    seg = seq // SUBLANE
    pitch = _segment_pitch(seg)

    row = lax.broadcasted_iota(jnp.int32, (seq, dg), 0)
    neg_softplus = [jax.nn.softplus(-lam_ref[d:d + 1, :]) for d in range(2)]
    for b in range(bt):
        rows = slice(b * seq, (b + 1) * seq)
        x = x_ref[rows, :]
        xm1 = jnp.where(row >= 1, pltpu.roll(x, 1, 0), 0.0)
        xp1 = jnp.where(row < seq - 1, pltpu.roll(x, seq - 1, 0), 0.0)
        xp2 = jnp.where(row < seq - 2, pltpu.roll(x, seq - 2, 0), 0.0)
        xc = cb_ref[...] + xm1 * cw_ref[0:1, :]
        xc = xc + x * cw_ref[1:2, :]
        xc = xc + xp1 * cw_ref[2:3, :]
        xc = xc + xp2 * cw_ref[3:4, :]
        gates = _dot(xc.astype(BF16), wg_ref[...]) + bg_ref[...]
        for d in range(2):
            r = jax.nn.sigmoid(gates[:, (2 * d) * dg:(2 * d + 1) * dg])
            i = jax.nn.sigmoid(gates[:, (2 * d + 1) * dg:(2 * d + 2) * dg])
            log_a = -RG_C * r * neg_softplus[d]
            a = jnp.exp(log_a)
            bq = jnp.sqrt(jnp.tanh(-log_a) * (a * a + 1.0)) * (i * xc)
            for s in range(SUBLANE):
                for c in range(nslab):
                    src = (slice(s * seg, (s + 1) * seg), slice(c * LANE, (c + 1) * LANE))
                    dst = slice(s * pitch, s * pitch + seg)
                    a_refs[d][b * nslab + c, dst, :] = a[src]
                    b_refs[d][b * nslab + c, dst, :] = bq[src]

    nch = bt * nslab
    zeros = tuple(jnp.zeros((SUBLANE, LANE), F32) for _ in range(nch))
    ones = tuple(jnp.ones((SUBLANE, LANE), F32) for _ in range(nch))

    def step(j, carry):
        out = []
        for d, t in ((0, j), (1, seg - 1 - j)):
            h_prev, p_prev = carry[2 * d], carry[2 * d + 1]
            h_new, p_new = [], []
            idx = pl.ds(t, SUBLANE, stride=pitch)
            for ch in range(nch):
                a = a_refs[d].at[ch][idx, :]
                h = a * h_prev[ch] + b_refs[d].at[ch][idx, :]
                p = a * p_prev[ch]
                b_refs[d].at[ch][idx, :] = h
                a_refs[d].at[ch][idx, :] = p
                h_new.append(h)
                p_new.append(p)
            out += [tuple(h_new), tuple(p_new)]
        return tuple(out)

    h_f, p_f, h_b, p_b = lax.fori_loop(0, seg, step, (zeros, ones, zeros, ones), unroll=2)

    for b in range(bt):
        for c in range(nslab):
            ch = b * nslab + c
            lanes = slice(c * LANE, (c + 1) * LANE)
            row_of = lambda v, s: v[s:s + 1, :]
            if has_h0:
                enter_f, enter_b = h0_ref[b, 0:1, lanes], h0_ref[b, 1:2, lanes]
            else:
                enter_f = enter_b = jnp.zeros((1, LANE), F32)
            ent_f, ent_b = [None] * SUBLANE, [None] * SUBLANE
            for s in range(SUBLANE):
                ent_f[s] = enter_f
                enter_f = row_of(h_f[ch], s) + row_of(p_f[ch], s) * enter_f
            for s in reversed(range(SUBLANE)):
                ent_b[s] = enter_b
                enter_b = row_of(h_b[ch], s) + row_of(p_b[ch], s) * enter_b
            if emit_last:
                hl_ref[b, 0:1, lanes] = enter_f
                hl_ref[b, 1:2, lanes] = enter_b
            for s in range(SUBLANE):
                src = slice(s * pitch, s * pitch + seg)
                fwd = b_refs[0][ch, src, :] + a_refs[0][ch, src, :] * ent_f[s]
                bwd = b_refs[1][ch, src, :] + a_refs[1][ch, src, :] * ent_b[s]
                hs_ref[b * seq + s * seg:b * seq + (s + 1) * seg, lanes] = fwd + bwd


def _rglru(layer, xr, row_start, nseq, seq, conv_w, conv_b, w_gate, b_gate, lam, h0=None, emit_last=False):
    bt = 4
    dg = RNN_GROUP
    ng = D_RNN // dg
    rows = bt * seq
    assert nseq % bt == 0 and row_start % rows == 0
    off = row_start // rows
    in_specs = [
        pl.BlockSpec((rows, dg), lambda bi, g: (off + bi, g)),
        pl.BlockSpec((None, CONV_W, dg), lambda bi, g: (layer, 0, g)),
        pl.BlockSpec((None, 1, dg), lambda bi, g: (layer, 0, g)),
        pl.BlockSpec((None, None, dg, 4 * dg), lambda bi, g: (layer, g, 0, 0)),
        pl.BlockSpec((None, None, 1, 4 * dg), lambda bi, g: (layer, g, 0, 0)),
        pl.BlockSpec((None, 2, dg), lambda bi, g: (layer, 0, g)),
    ]
    args = [xr, conv_w, conv_b, w_gate, b_gate, lam]
    if h0 is not None:
        in_specs.append(pl.BlockSpec((bt, None, 2, dg), lambda bi, g: (bi, layer, 0, g)))
        args.append(h0)
    out_specs = [pl.BlockSpec((rows, dg), lambda bi, g: (bi, g))]
    out_shape = [jax.ShapeDtypeStruct((nseq * seq, D_RNN), F32)]
    if emit_last:
        out_specs.append(pl.BlockSpec((bt, 2, dg), lambda bi, g: (bi, 0, g)))
        out_shape.append(jax.ShapeDtypeStruct((nseq, 2, D_RNN), F32))
    return pl.pallas_call(
        functools.partial(_rglru_kernel, seq=seq, bt=bt, has_h0=h0 is not None, emit_last=emit_last),
        grid=(nseq // bt, ng),
        in_specs=in_specs,
        out_specs=out_specs,
        out_shape=out_shape,
        scratch_shapes=[pltpu.VMEM((bt * (dg // LANE), SUBLANE * _segment_pitch(seq // SUBLANE), LANE), F32)
                        for _ in range(4)],
        compiler_params=_params("arbitrary", "arbitrary"),
        name=f"rglru_{seq}",
    )(*args)


def _attn_kernel(*refs, past, tq, rope):
    it = iter(refs)
    cq_ref, ckv_ref, kr_ref = (next(it) for _ in range(3))
    if rope:
        krr_ref, pckv_ref, pkr_ref, cos_ref, sin_ref = (next(it) for _ in range(5))
    wq_ref = next(it)
    wqr_ref = next(it) if rope else None
    wk_ref, wv_ref = next(it), next(it)
    o_ref = next(it)
    k_scr, v_scr = next(it), next(it)
    qi = pl.program_id(1)
    scale = 1.0 / math.sqrt(QK_NOPE + QK_ROPE)

    def fill(rows, ckv, k_rope):
        c = ckv.astype(BF16)
        k = _dot(c, wk_ref[...]) + jnp.tile(k_rope, (1, N_HEADS))
        k_scr[rows, :] = k.astype(BF16)
        v_scr[rows, :] = _dot(c, wv_ref[...]).astype(BF16)

    @pl.when(qi == 0)
    def _():
        seq = ckv_ref.shape[0]
        if rope:
            fill(pl.ds(0, past), pckv_ref[...], pkr_ref[...])
            fill(pl.ds(past, seq), ckv_ref[...],
                 kr_ref[...] * cos_ref[...] + krr_ref[...] * sin_ref[...])
        else:
            fill(pl.ds(0, seq), ckv_ref[...], kr_ref[...])

    cq = cq_ref[...]
    qa = _dot(cq, wq_ref[...])
    if rope:
        qb = _dot(cq, wqr_ref[...])
        q0 = pl.multiple_of(qi * tq, tq)
        cos_t = cos_ref[pl.ds(q0, tq), :]
        sin_t = sin_ref[pl.ds(q0, tq), :]
    for j in range(N_HEADS // 2):
        pair = None
        for h in (2 * j, 2 * j + 1):
            cols = slice(h * HEAD_PAD, (h + 1) * HEAD_PAD)
            qh = qa[:, cols]
            if rope:
                qh = qh * cos_t + qb[:, cols] * sin_t
            s = lax.dot_general(qh.astype(BF16), k_scr[:, cols], (((1,), (1,)), ((), ())),
                                preferred_element_type=F32)
            e = jnp.exp((s - jnp.max(s, axis=-1, keepdims=True)) * scale)
            denom = jnp.sum(e, axis=-1, keepdims=True)
            o = _dot(e.astype(BF16), v_scr[:, cols]) * (1.0 / denom)
            pair = o if pair is None else pair + o
        o_ref[:, j * LANE:(j + 1) * LANE] = pair.astype(BF16)


def _attention(layer, cq, ckv, kr, row_start, nseq, seq, w_q, w_k, w_v, lat=None):
    rope = lat is not None
    tq = min(_ATT_Q_TILE, seq)
    assert seq % tq == 0 and row_start % seq == 0
    past = lat[1].shape[2] if rope else 0
    total = past + seq
    hp = N_HEADS * HEAD_PAD
    nq = seq // tq
    s_off, q_off = row_start // seq, row_start // tq
    full = lambda w: pl.BlockSpec((seq, w), lambda b, q: (s_off + b, 0))
    q_rows = lambda w: pl.BlockSpec((tq, w), lambda b, q: (q_off + b * nq + q, 0))
    in_specs = [q_rows(Q_LORA), full(KV_LORA), full(HEAD_PAD)]
    args = [cq, ckv, kr]
    if rope:
        krr, pckv, pkr, cos, sin, w_q_rot = lat
        in_specs += [full(HEAD_PAD),
                     pl.BlockSpec((None, None, past, KV_LORA), lambda b, q: (b, layer, 0, 0)),
                     pl.BlockSpec((None, None, past, HEAD_PAD), lambda b, q: (b, layer, 0, 0)),
                     _fixed_spec((seq, HEAD_PAD)), _fixed_spec((seq, HEAD_PAD))]
        args += [krr, pckv, pkr, cos, sin]
    in_specs.append(_fixed_spec((Q_LORA, hp), (layer,)))
    args.append(w_q)
    if rope:
        in_specs.append(_fixed_spec((Q_LORA, hp), (layer,)))
        args.append(w_q_rot)
    in_specs += [_fixed_spec((KV_LORA, hp), (layer,)), _fixed_spec((KV_LORA, hp), (layer,))]
    args += [w_k, w_v]
    return pl.pallas_call(
        functools.partial(_attn_kernel, past=past, tq=tq, rope=rope),
        grid=(nseq, nq),
        in_specs=in_specs,
        out_specs=pl.BlockSpec((tq, N_HEADS * V_HEAD), lambda b, q: (b * nq + q, 0)),
        out_shape=jax.ShapeDtypeStruct((nseq * seq, N_HEADS * V_HEAD), BF16),
        scratch_shapes=[pltpu.VMEM((total, hp), BF16), pltpu.VMEM((total, hp), BF16)],
        compiler_params=_params("arbitrary", "arbitrary"),
        name=f"attn_{seq}",
    )(*args)


def _mix_kernel(x_ref, hsc_ref, hsl_ref, atc_ref, atl_ref, m_ref, wygg_ref, wro_ref, wao_ref, wo_ref,
                g_ref, b_ref, o_ref, wro_bf, wao_bf, wo_bf, *, n_ctx_tiles):
    _cast_once([(wro_ref, wro_bf), (wao_ref, wao_bf), (wo_ref, wo_bf)])
    is_ctx = pl.program_id(0) < n_ctx_tiles
    hs = jnp.where(is_ctx, hsc_ref[...], hsl_ref[...])
    at = jnp.where(is_ctx, atc_ref[...], atl_ref[...])
    x = x_ref[...]
    h = (x * (1.0 + m_ref[1:2, :]) + m_ref[0:1, :]).astype(BF16)
    ygg = _dot(h, wygg_ref[...])
    u = (hs * jax.nn.gelu(ygg[:, :D_RNN])).astype(BF16)
    rnn_out = _dot(u, wro_bf[...])
    att_out = _dot(at, wao_bf[...])
    merged = (jax.nn.sigmoid(ygg[:, D_RNN:D_RNN + D_MODEL]) * rnn_out
              + jax.nn.sigmoid(ygg[:, D_RNN + D_MODEL:]) * att_out)
    m = _dot(merged.astype(BF16), wo_bf[...])
    o_ref[...] = _layer_norm(ALPHA * x + m_ref[2:3, :] * m, g_ref[...], b_ref[...])


def _mix(layer, x, hs_ctx, hs_lat, at_ctx, at_lat, mods, w_ygg, w_ro, w_ao, w_o, ln_g, ln_b, n_ctx_rows,
         rows_per_latent):
    n_tok = x.shape[0]
    tm = _ROW_TILE
    nct = n_ctx_rows // tm
    row = pl.BlockSpec((tm, D_MODEL), lambda i: (i, 0))
    ctx_row = pl.BlockSpec((tm, D_MODEL), lambda i: (jnp.minimum(i, nct - 1), 0))
    lat_row = pl.BlockSpec((tm, D_MODEL), lambda i: (jnp.maximum(i - nct, 0), 0))
    sq = (D_MODEL, D_MODEL)
    return pl.pallas_call(
        functools.partial(_mix_kernel, n_ctx_tiles=nct),
        grid=(n_tok // tm,),
        in_specs=[row, ctx_row, lat_row, ctx_row, lat_row,
                  _mod_spec(layer, tm, n_ctx_rows, rows_per_latent),
                  _fixed_spec((D_MODEL, D_RNN + 2 * D_MODEL), (layer,)),
                  _fixed_spec(sq, (layer,)), _fixed_spec(sq, (layer,)), _fixed_spec(sq, (layer,)),
                  _fixed_spec((1, D_MODEL), (layer,)), _fixed_spec((1, D_MODEL), (layer,))],
        out_specs=row,
        out_shape=jax.ShapeDtypeStruct((n_tok, D_MODEL), F32),
        scratch_shapes=[pltpu.VMEM(sq, BF16) for _ in range(3)],
        compiler_params=_params("arbitrary"),
        name="mix_out",
    )(x, hs_ctx, hs_lat, at_ctx, at_lat, mods, w_ygg, w_ro, w_ao, w_o, ln_g, ln_b)


def _ffn_kernel(x_ref, m_ref, w1_ref, w3_ref, w2_ref, g_ref, b_ref, o_ref):
    x = x_ref[...]
    h = (x * (1.0 + m_ref[4:5, :]) + m_ref[3:4, :]).astype(BF16)
    f = None
    for c in range(D_FF // _FF_CHUNK):
        cols = slice(c * _FF_CHUNK, (c + 1) * _FF_CHUNK)
        act = (jax.nn.silu(_dot(h, w1_ref[:, cols])) * _dot(h, w3_ref[:, cols])).astype(BF16)
        y = _dot(act, w2_ref[cols, :])
        f = y if f is None else f + y
    o_ref[...] = _layer_norm(ALPHA * x + m_ref[5:6, :] * f, g_ref[...], b_ref[...])


def _ffn_dense(layer, x, mods, w1, w3, w2, ln_g, ln_b, n_ctx_rows, rows_per_latent):
    n_tok = x.shape[0]
    tm = _ROW_TILE
    j = layer // 2
    row = pl.BlockSpec((tm, D_MODEL), lambda i: (i, 0))
    return pl.pallas_call(
        _ffn_kernel,
        grid=(n_tok // tm,),
        in_specs=[row, _mod_spec(layer, tm, n_ctx_rows, rows_per_latent),
                  _fixed_spec((D_MODEL, D_FF), (j,)), _fixed_spec((D_MODEL, D_FF), (j,)),
                  _fixed_spec((D_FF, D_MODEL), (j,)),
                  _fixed_spec((1, D_MODEL), (layer,)), _fixed_spec((1, D_MODEL), (layer,))],
        out_specs=row,
        out_shape=jax.ShapeDtypeStruct((n_tok, D_MODEL), F32),
        compiler_params=_params("arbitrary"),
        name="ffn_dense",
    )(x, mods, w1, w3, w2, ln_g, ln_b)


def _split_bf16(a):
    hi = a.astype(BF16)
    return hi, (a - hi.astype(F32)).astype(BF16)


def _top2_gates(logits):
    lane = lax.broadcasted_iota(jnp.int32, logits.shape, 1)
    big = jnp.int32(N_EXPERTS)
    v1 = jnp.max(logits, axis=-1, keepdims=True)
    i1 = jnp.min(jnp.where(logits == v1, lane, big), axis=-1, keepdims=True)
    rest = jnp.where(lane == i1, -jnp.inf, logits)
    v2 = jnp.max(rest, axis=-1, keepdims=True)
    i2 = jnp.min(jnp.where(rest == v2, lane, big), axis=-1, keepdims=True)
    e2 = jnp.exp(v2 - v1)
    w1 = 1.0 / (1.0 + e2)
    w2 = e2 / (1.0 + e2)
    return jnp.where(lane == i1, w1, 0.0) + jnp.where(lane == i2, w2, 0.0)


def _moe_kernel(x_ref, m_ref, wr_ref, br_ref, w1_ref, w3_ref, w2_ref, g_ref, b_ref, o_ref,
                h_scr, gate_scr, acc_scr):
    e = pl.program_id(1)

    @pl.when(e == 0)
    def _():
        h = x_ref[...] * (1.0 + m_ref[4:5, :]) + m_ref[3:4, :]
        h_scr[...] = h.astype(BF16)
        h_hi, h_lo = _split_bf16(h)
        w_hi, w_lo = _split_bf16(wr_ref[...])
        logits = _dot(h_hi, w_hi) + (_dot(h_hi, w_lo) + _dot(h_lo, w_hi)) + br_ref[...]
        gate_scr[...] = _top2_gates(logits)
        acc_scr[...] = jnp.zeros_like(acc_scr)

    h = h_scr[...]
    act = (jax.nn.silu(_dot(h, w1_ref[...])) * _dot(h, w3_ref[...])).astype(BF16)
    y = _dot(act, w2_ref[...])
    lane = lax.broadcasted_iota(jnp.int32, gate_scr.shape, 1)
    gate = jnp.sum(jnp.where(lane == e, gate_scr[...], 0.0), axis=-1, keepdims=True)
    acc_scr[...] += gate * y

    @pl.when(e == N_EXPERTS - 1)
    def _():
        o_ref[...] = _layer_norm(ALPHA * x_ref[...] + m_ref[5:6, :] * acc_scr[...], g_ref[...], b_ref[...])


def _ffn_moe(layer, x, mods, w_router, b_router, w1, w3, w2, ln_g, ln_b, n_ctx_rows, rows_per_latent):
    n_tok = x.shape[0]
    tm = _MOE_ROW_TILE
    j = layer // 2
    row = pl.BlockSpec((tm, D_MODEL), lambda i, e: (i, 0))
    up = pl.BlockSpec((None, None, D_MODEL, D_EXPERT), lambda i, e: (j, e, 0, 0))
    return pl.pallas_call(
        _moe_kernel,
        grid=(n_tok // tm, N_EXPERTS),
        in_specs=[row, _mod_spec(layer, tm, n_ctx_rows, rows_per_latent),
                  _fixed_spec((D_MODEL, N_EXPERTS), (j,)), _fixed_spec((1, N_EXPERTS), (j,)),
                  up, up, pl.BlockSpec((None, None, D_EXPERT, D_MODEL), lambda i, e: (j, e, 0, 0)),
                  _fixed_spec((1, D_MODEL), (layer,)), _fixed_spec((1, D_MODEL), (layer,))],
        out_specs=row,
        out_shape=jax.ShapeDtypeStruct((n_tok, D_MODEL), F32),
        scratch_shapes=[pltpu.VMEM((tm, D_MODEL), BF16), pltpu.VMEM((tm, N_EXPERTS), F32),
                        pltpu.VMEM((tm, D_MODEL), F32)],
        compiler_params=_params("arbitrary", "arbitrary"),
        name="ffn_moe",
    )(x, mods, w_router, b_router, w1, w3, w2, ln_g, ln_b)


def _rot_cols(w):
    ws = w.reshape(w.shape[:-1] + (2, 2, ROPE_AXIS // 2))
    return jnp.stack([-ws[..., 1, :], ws[..., 0, :]], axis=-2).reshape(w.shape)


def _pad_rope_cols(w):
    return jnp.pad(w, ((0, 0),) * (w.ndim - 1) + ((ROPE_OFF, HEAD_PAD - ROPE_OFF - QK_ROPE),))


def _rope_tables(rows):
    r = jnp.repeat(jnp.arange(rows, dtype=F32), GRID_W)
    col = jnp.tile(jnp.arange(GRID_W, dtype=F32), rows)
    freqs = ROPE_THETA ** (-jnp.arange(0, ROPE_AXIS, 2, dtype=F32) / ROPE_AXIS)
    ar = r[:, None] * freqs
    ac = col[:, None] * freqs
    ang = jnp.concatenate([ar, ar, ac, ac], axis=-1)
    cos, sin = jnp.cos(ang), jnp.sin(ang)
    n = cos.shape[0]
    cos_p = jnp.concatenate([jnp.ones((n, ROPE_OFF), F32), cos,
                             jnp.zeros((n, HEAD_PAD - ROPE_OFF - QK_ROPE), F32)], axis=-1)
    return cos_p, _pad_rope_cols(sin)


def _prepared_weights(w_in, rg_wa, rg_ba, rg_wx, rg_bx, w_qb, w_kvb):
    bf = lambda a: a.astype(BF16)
    w_kr = w_in[:, :, _COL_KR:_COL_KR + QK_ROPE]
    w_s = jnp.concatenate([w_in[:, :, _COL_Q:_COL_KR], _pad_rope_cols(w_kr),
                           _pad_rope_cols(_rot_cols(w_kr))], axis=-1)
    w_x = w_in[:, :, :D_RNN]
    w_ygg = jnp.concatenate([w_in[:, :, D_RNN:2 * D_RNN], w_in[:, :, _COL_GATE:]], axis=-1)

    wq = w_qb.reshape(DEPTH, Q_LORA, N_HEADS, QK_NOPE + QK_ROPE)
    q_nope, q_rope = wq[..., :QK_NOPE], wq[..., QK_NOPE:]
    tail = jnp.zeros((DEPTH, Q_LORA, N_HEADS, HEAD_PAD - QK_NOPE - QK_ROPE), F32)
    hp = N_HEADS * HEAD_PAD
    w_q = jnp.concatenate([q_nope, q_rope, tail], axis=-1).reshape(DEPTH, Q_LORA, hp)
    w_q_rot = jnp.concatenate([jnp.zeros_like(q_nope), _rot_cols(q_rope), tail], axis=-1)
    w_q_rot = w_q_rot.reshape(DEPTH, Q_LORA, hp)
    wkv = w_kvb.reshape(DEPTH, KV_LORA, N_HEADS, QK_NOPE + V_HEAD)
    k_nope, v = wkv[..., :QK_NOPE], wkv[..., QK_NOPE:]
    w_k = jnp.concatenate([k_nope, jnp.zeros((DEPTH, KV_LORA, N_HEADS, HEAD_PAD - QK_NOPE), F32)], axis=-1)
    w_k = w_k.reshape(DEPTH, KV_LORA, hp)
    vz = jnp.zeros_like(v)
    odd = (jnp.arange(N_HEADS) % 2 == 1)[None, None, :, None]
    w_v = jnp.where(odd, jnp.concatenate([vz, v], axis=-1), jnp.concatenate([v, vz], axis=-1))
    w_v = w_v.reshape(DEPTH, KV_LORA, hp)

    nblk = RNN_GROUP // RNN_BW
    ng = D_RNN // RNN_GROUP
    eye = jnp.eye(nblk, dtype=F32)

    def block_diag(w):
        wg = w.reshape(DEPTH, ng, nblk, RNN_BW, RNN_BW)
        return jnp.einsum('lgnjk,nm->lgnjmk', wg, eye).reshape(DEPTH, ng, RNN_GROUP, RNN_GROUP)

    w_gate = jnp.concatenate([block_diag(rg_wa[:, 0]), block_diag(rg_wx[:, 0]),
                              block_diag(rg_wa[:, 1]), block_diag(rg_wx[:, 1])], axis=-1)
    grp = lambda b: b.reshape(DEPTH, ng, 1, RNN_GROUP)
    b_gate = jnp.concatenate([grp(rg_ba[:, 0]), grp(rg_bx[:, 0]), grp(rg_ba[:, 1]), grp(rg_bx[:, 1])],
                             axis=-1)
    return dict(w_x=bf(w_x), w_s=bf(w_s), w_ygg=bf(w_ygg), w_q=bf(w_q), w_q_rot=bf(w_q_rot), w_k=bf(w_k), w_v=bf(w_v),
                w_gate=bf(w_gate), b_gate=b_gate)


def kernel(x_prompt, x_sample, cache_ckv, cache_krope, state_rnn, c, c_ctx, w_ada, b_ada, w_in, conv_w, conv_b, rg_wa, rg_ba, rg_wx, rg_bx, rg_lambda, w_rnn_out, q_norm_g, w_qb, kv_norm_g, w_kvb, w_attn_out, w_out, ln1_g, ln1_b, ln2_g, ln2_b, w1_dense, w3_dense, w2_dense, w_router, b_router, w1_exp, w3_exp, w2_exp):
    nb, seq, _ = x_prompt.shape
    db, dseq, _ = x_sample.shape
    n_ctx = nb * seq
    n_lat = db * dseq
    assert n_ctx % _ROW_TILE == 0 and dseq % _ROW_TILE == 0 and db + 1 <= 8
    bf = lambda a: a.astype(BF16)
    vec = lambda a: a.reshape(a.shape[0], 1, a.shape[-1])

    cond8 = jnp.concatenate([c_ctx[None, :], c, jnp.zeros((8 - 1 - db, D_MODEL), F32)], axis=0)
    mods = _ada_mods(cond8, w_ada, b_ada)
    cos_p, sin_p = _rope_tables(dseq // GRID_W)
    past_kr = _pad_rope_cols(cache_krope)
    pw = _prepared_weights(w_in, rg_wa, rg_ba, rg_wx, rg_bx, w_qb, w_kvb)
    w1d, w3d, w2d = bf(w1_dense), bf(w3_dense), bf(w2_dense)
    w1e, w3e, w2e = bf(w1_exp), bf(w3_exp), bf(w2_exp)
    qg, kvg, cb = vec(q_norm_g), vec(kv_norm_g), vec(conv_b)
    l1g, l1b, l2g, l2b = vec(ln1_g), vec(ln1_b), vec(ln2_g), vec(ln2_b)
    br = vec(b_router)
    tiles = (n_ctx, dseq)

    x = jnp.concatenate([x_prompt.reshape(n_ctx, D_MODEL), x_sample.reshape(n_lat, D_MODEL)], axis=0)
    new_ckv, new_kr, new_h = [], [], []
    for l in range(DEPTH):
        xr, cq, ckv, kr, krr = _in_proj(l, x, mods, pw['w_x'], pw['w_s'], qg, kvg, *tiles)
        new_ckv.append(ckv[:n_ctx].reshape(nb, seq, KV_LORA))
        new_kr.append(kr[:n_ctx, ROPE_OFF:ROPE_OFF + QK_ROPE].reshape(nb, seq, QK_ROPE))

        rnn_args = (conv_w, cb, pw['w_gate'], pw['b_gate'], rg_lambda)
        hs_ctx, h_last = _rglru(l, xr, 0, nb, seq, *rnn_args, emit_last=True)
        (hs_lat,) = _rglru(l, xr, n_ctx, db, dseq, *rnn_args, h0=state_rnn)
        new_h.append(h_last)

        att_w = (pw['w_q'], pw['w_k'], pw['w_v'])
        at_ctx = _attention(l, cq, ckv, kr, 0, nb, seq, *att_w)
        at_lat = _attention(l, cq, ckv, kr, n_ctx, db, dseq, *att_w,
                            lat=(krr, cache_ckv, past_kr, cos_p, sin_p, pw['w_q_rot']))

        x = _mix(l, x, hs_ctx, hs_lat, at_ctx, at_lat, mods, pw['w_ygg'], w_rnn_out, w_attn_out, w_out,
                 l1g, l1b, *tiles)
        if l % 2 == 0:
            x = _ffn_dense(l, x, mods, w1d, w3d, w2d, l2g, l2b, *tiles)
        else:
            x = _ffn_moe(l, x, mods, w_router, br, w1e, w3e, w2e, l2g, l2b, *tiles)

    return (x[:n_ctx].reshape(nb, seq, D_MODEL), x[n_ctx:].reshape(db, dseq, D_MODEL),
            jnp.stack(new_ckv, axis=1), jnp.stack(new_kr, axis=1), jnp.stack(new_h, axis=1))
```

```python
import functools
import math

import jax
import jax.numpy as jnp
from jax import lax
from jax.experimental import pallas as pl
from jax.experimental.pallas import tpu as pltpu

F32 = jnp.float32
BF16 = jnp.bfloat16

D_MODEL = 1024
DEPTH = 4
GRID_W = 64
D_RNN = D_MODEL
RNN_BLOCKS = 16
RNN_BW = D_RNN // RNN_BLOCKS
CONV_W = 4
RG_C = 8.0
N_HEADS = 16
QK_NOPE = 64
QK_ROPE = 32
V_HEAD = 64
Q_LORA = 256
KV_LORA = 128
ROPE_AXIS = QK_ROPE // 2
ROPE_THETA = 10000.0
D_FF = 2816
N_EXPERTS = 8
TOP_K = 2
D_EXPERT = 1408
ALPHA = (2 * DEPTH) ** 0.25
LN_EPS = 1e-5
RMS_EPS = 1e-6

LANE = 128
SUBLANE = 8
HEAD_PAD = LANE
ROPE_OFF = QK_NOPE
RNN_GROUP = 256
VMEM_LIMIT = 56 * 1024 * 1024

_ROW_TILE = 512
_MOE_SLOT_TILE = 512
_ATT_Q_TILE = 512
_FF_CHUNK = 1408

_COL_Q = 2 * D_RNN
_COL_KR = _COL_Q + Q_LORA + KV_LORA
_COL_GATE = _COL_KR + QK_ROPE


def _dot(a, b):
    return jnp.dot(a, b, preferred_element_type=F32)


def _layer_norm(y, g, b):
    mu = jnp.mean(y, axis=-1, keepdims=True)
    d = y - mu
    var = jnp.mean(d * d, axis=-1, keepdims=True)
    return d * lax.rsqrt(var + LN_EPS) * g + b


def _rms_norm(y, g):
    return y * lax.rsqrt(jnp.mean(y * y, axis=-1, keepdims=True) + RMS_EPS) * g


def _params(*sem):
    return pltpu.CompilerParams(dimension_semantics=sem, vmem_limit_bytes=VMEM_LIMIT)


def _fixed_spec(shape, lead=(), tail=None):
    tail = (0,) * len(shape) if tail is None else tuple(tail)
    index = tuple(lead) + tail
    return pl.BlockSpec((None,) * len(lead) + tuple(shape), lambda *_: index,
                        pipeline_mode=pl.Buffered(1))


def _mod_spec(layer, tm, n_ctx_rows, rows_per_latent):
    n_ctx_tiles = n_ctx_rows // tm
    per = rows_per_latent // tm

    def index(i, *_):
        return (layer, jnp.where(i < n_ctx_tiles, 0, 1 + (i - n_ctx_tiles) // per), 0, 0)

    return pl.BlockSpec((None, None, 6, D_MODEL), index)


def _cast_once(pairs):
    @pl.when(pl.program_id(0) == 0)
    def _():
        for src, dst in pairs:
            dst[...] = src[...].astype(BF16)


def _mods_kernel(c_ref, w_ref, b_ref, o_ref):
    s = jax.nn.silu(c_ref[...]).astype(BF16)
    o_ref[...] = _dot(s, w_ref[...].astype(BF16)) + b_ref[...]


def _ada_mods(cond8, w_ada, b_ada):
    out = pl.pallas_call(
        _mods_kernel,
        grid=(DEPTH, 6),
        in_specs=[
            pl.BlockSpec((8, D_MODEL), lambda l, j: (0, 0)),
            pl.BlockSpec((None, D_MODEL, D_MODEL), lambda l, j: (l, 0, j)),
            pl.BlockSpec((None, 1, D_MODEL), lambda l, j: (l, 0, j)),
        ],
        out_specs=pl.BlockSpec((None, 8, D_MODEL), lambda l, j: (l, 0, j)),
        out_shape=jax.ShapeDtypeStruct((DEPTH, 8, 6 * D_MODEL), F32),
        compiler_params=_params("arbitrary", "arbitrary"),
        name="ada_mods",
    )(cond8, w_ada, b_ada.reshape(DEPTH, 1, 6 * D_MODEL))
    return out.reshape(DEPTH, 8, 6, D_MODEL)


_S_COLS = Q_LORA + KV_LORA + 2 * HEAD_PAD


def _inproj_kernel(x_ref, m_ref, wx_ref, ws_ref, qg_ref, kvg_ref,
                   xr_ref, cq_ref, ckv_ref, kr_ref, krr_ref):
    h = (x_ref[...] * (1.0 + m_ref[1:2, :]) + m_ref[0:1, :]).astype(BF16)
    xr_ref[...] = _dot(h, wx_ref[...])
    s = _dot(h, ws_ref[...])
    cq_ref[...] = _rms_norm(s[:, :Q_LORA], qg_ref[...]).astype(BF16)
    ckv_ref[...] = _rms_norm(s[:, Q_LORA:Q_LORA + KV_LORA], kvg_ref[...])
    kr_ref[...] = s[:, Q_LORA + KV_LORA:Q_LORA + KV_LORA + HEAD_PAD]
    krr_ref[...] = s[:, Q_LORA + KV_LORA + HEAD_PAD:]


def _in_proj(layer, x, mods, w_x, w_s, q_g, kv_g, n_ctx_rows, rows_per_latent):
    n_tok = x.shape[0]
    tm = _ROW_TILE
    row = lambda w: pl.BlockSpec((tm, w), lambda i: (i, 0))
    return pl.pallas_call(
        _inproj_kernel,
        grid=(n_tok // tm,),
        in_specs=[
            row(D_MODEL),
            _mod_spec(layer, tm, n_ctx_rows, rows_per_latent),
            _fixed_spec((D_MODEL, D_RNN), (layer,)),
            _fixed_spec((D_MODEL, _S_COLS), (layer,)),
            _fixed_spec((1, Q_LORA), (layer,)),
            _fixed_spec((1, KV_LORA), (layer,)),
        ],
        out_specs=[row(D_RNN), row(Q_LORA), row(KV_LORA), row(HEAD_PAD), row(HEAD_PAD)],
        out_shape=[
            jax.ShapeDtypeStruct((n_tok, D_RNN), F32),
            jax.ShapeDtypeStruct((n_tok, Q_LORA), BF16),
            jax.ShapeDtypeStruct((n_tok, KV_LORA), F32),
            jax.ShapeDtypeStruct((n_tok, HEAD_PAD), F32),
            jax.ShapeDtypeStruct((n_tok, HEAD_PAD), F32),
        ],
        compiler_params=_params("arbitrary"),
        name="in_proj",
    )(x, mods, w_x, w_s, q_g, kv_g)


def _segment_pitch(seg):
    assert seg % SUBLANE == 0
    return seg if (seg // SUBLANE) % 2 == 1 else seg + SUBLANE


def _rglru_kernel(*refs, seq, bt, has_h0, emit_last):
    it = iter(refs)
    x_ref, cw_ref, cb_ref, wg_ref, bg_ref, lam_ref = (next(it) for _ in range(6))
    h0_ref = next(it) if has_h0 else None
    hs_ref = next(it)
    hl_ref = next(it) if emit_last else None
    a_refs = (next(it), next(it))
    b_refs = (next(it), next(it))
    dg = RNN_GROUP
    nslab = dg // LANE
    seg = seq // SUBLANE
    pitch = _segment_pitch(seg)

    row = lax.broadcasted_iota(jnp.int32, (seq, dg), 0)
    neg_softplus = [jax.nn.softplus(-lam_ref[d:d + 1, :]) for d in range(2)]
    for b in range(bt):
        rows = slice(b * seq, (b + 1) * seq)
        x = x_ref[rows, :]
        xm1 = jnp.where(row >= 1, pltpu.roll(x, 1, 0), 0.0)
        xp1 = jnp.where(row < seq - 1, pltpu.roll(x, seq - 1, 0), 0.0)
        xp2 = jnp.where(row < seq - 2, pltpu.roll(x, seq - 2, 0), 0.0)
        xc = cb_ref[...] + xm1 * cw_ref[0:1, :]
        xc = xc + x * cw_ref[1:2, :]
        xc = xc + xp1 * cw_ref[2:3, :]
        xc = xc + xp2 * cw_ref[3:4, :]
        gates = _dot(xc.astype(BF16), wg_ref[...]) + bg_ref[...]
        for d in range(2):
            r = jax.nn.sigmoid(gates[:, (2 * d) * dg:(2 * d + 1) * dg])
            i = jax.nn.sigmoid(gates[:, (2 * d + 1) * dg:(2 * d + 2) * dg])
            log_a = -RG_C * r * neg_softplus[d]
            a = jnp.exp(log_a)
            bq = jnp.sqrt(jnp.tanh(-log_a) * (a * a + 1.0)) * (i * xc)
            for s in range(SUBLANE):
                for c in range(nslab):
                    src = (slice(s * seg, (s + 1) * seg), slice(c * LANE, (c + 1) * LANE))
                    dst = slice(s * pitch, s * pitch + seg)
                    a_refs[d][b * nslab + c, dst, :] = a[src]
                    b_refs[d][b * nslab + c, dst, :] = bq[src]

    nch = bt * nslab
    zeros = tuple(jnp.zeros((SUBLANE, LANE), F32) for _ in range(nch))
    ones = tuple(jnp.ones((SUBLANE, LANE), F32) for _ in range(nch))

    def step(j, carry):
        out = []
        for d, t in ((0, j), (1, seg - 1 - j)):
            h_prev, p_prev = carry[2 * d], carry[2 * d + 1]
            h_new, p_new = [], []
            idx = pl.ds(t, SUBLANE, stride=pitch)
            for ch in range(nch):
                a = a_refs[d].at[ch][idx, :]
                h = a * h_prev[ch] + b_refs[d].at[ch][idx, :]
                p = a * p_prev[ch]
                b_refs[d].at[ch][idx, :] = h
                a_refs[d].at[ch][idx, :] = p
                h_new.append(h)
                p_new.append(p)
            out += [tuple(h_new), tuple(p_new)]
        return tuple(out)

    h_f, p_f, h_b, p_b = lax.fori_loop(0, seg, step, (zeros, ones, zeros, ones), unroll=2)

    for b in range(bt):
        for c in range(nslab):
            ch = b * nslab + c
            lanes = slice(c * LANE, (c + 1) * LANE)
            row_of = lambda v, s: v[s:s + 1, :]
            if has_h0:
                enter_f, enter_b = h0_ref[b, 0:1, lanes], h0_ref[b, 1:2, lanes]
            else:
                enter_f = enter_b = jnp.zeros((1, LANE), F32)
            ent_f, ent_b = [None] * SUBLANE, [None] * SUBLANE
            for s in range(SUBLANE):
                ent_f[s] = enter_f
                enter_f = row_of(h_f[ch], s) + row_of(p_f[ch], s) * enter_f
            for s in reversed(range(SUBLANE)):
                ent_b[s] = enter_b
                enter_b = row_of(h_b[ch], s) + row_of(p_b[ch], s) * enter_b
            if emit_last:
                hl_ref[b, 0:1, lanes] = enter_f
                hl_ref[b, 1:2, lanes] = enter_b
            for s in range(SUBLANE):
                src = slice(s * pitch, s * pitch + seg)
                fwd = b_refs[0][ch, src, :] + a_refs[0][ch, src, :] * ent_f[s]
                bwd = b_refs[1][ch, src, :] + a_refs[1][ch, src, :] * ent_b[s]
                hs_ref[b * seq + s * seg:b * seq + (s + 1) * seg, lanes] = fwd + bwd


def _rglru(layer, xr, row_start, nseq, seq, conv_w, conv_b, w_gate, b_gate, lam, h0=None, emit_last=False):
    bt = 4
    dg = RNN_GROUP
    ng = D_RNN // dg
    rows = bt * seq
    assert nseq % bt == 0 and row_start % rows == 0
    off = row_start // rows
    in_specs = [
        pl.BlockSpec((rows, dg), lambda bi, g: (off + bi, g)),
        pl.BlockSpec((None, CONV_W, dg), lambda bi, g: (layer, 0, g)),
        pl.BlockSpec((None, 1, dg), lambda bi, g: (layer, 0, g)),
        pl.BlockSpec((None, None, dg, 4 * dg), lambda bi, g: (layer, g, 0, 0)),
        pl.BlockSpec((None, None, 1, 4 * dg), lambda bi, g: (layer, g, 0, 0)),
        pl.BlockSpec((None, 2, dg), lambda bi, g: (layer, 0, g)),
    ]
    args = [xr, conv_w, conv_b, w_gate, b_gate, lam]
    if h0 is not None:
        in_specs.append(pl.BlockSpec((bt, None, 2, dg), lambda bi, g: (bi, layer, 0, g)))
        args.append(h0)
    out_specs = [pl.BlockSpec((rows, dg), lambda bi, g: (bi, g))]
    out_shape = [jax.ShapeDtypeStruct((nseq * seq, D_RNN), F32)]
    if emit_last:
        out_specs.append(pl.BlockSpec((bt, 2, dg), lambda bi, g: (bi, 0, g)))
        out_shape.append(jax.ShapeDtypeStruct((nseq, 2, D_RNN), F32))
    return pl.pallas_call(
        functools.partial(_rglru_kernel, seq=seq, bt=bt, has_h0=h0 is not None, emit_last=emit_last),
        grid=(nseq // bt, ng),
        in_specs=in_specs,
        out_specs=out_specs,
        out_shape=out_shape,
        scratch_shapes=[pltpu.VMEM((bt * (dg // LANE), SUBLANE * _segment_pitch(seq // SUBLANE), LANE), F32)
                        for _ in range(4)],
        compiler_params=_params("arbitrary", "arbitrary"),
        name=f"rglru_{seq}",
    )(*args)


def _attn_kernel(*refs, past, tq, rope):
    it = iter(refs)
    cq_ref, ckv_ref, kr_ref = (next(it) for _ in range(3))
    if rope:
        krr_ref, pckv_ref, pkr_ref, cos_ref, sin_ref = (next(it) for _ in range(5))
    wq_ref = next(it)
    wqr_ref = next(it) if rope else None
    wk_ref, wv_ref = next(it), next(it)
    o_ref = next(it)
    k_scr, v_scr = next(it), next(it)
    qi = pl.program_id(1)
    scale = 1.0 / math.sqrt(QK_NOPE + QK_ROPE)

    def fill(rows, ckv, k_rope):
        c = ckv.astype(BF16)
        k = _dot(c, wk_ref[...]) + jnp.tile(k_rope, (1, N_HEADS))
        k_scr[rows, :] = k.astype(BF16)
        v_scr[rows, :] = _dot(c, wv_ref[...]).astype(BF16)

    @pl.when(qi == 0)
    def _():
        seq = ckv_ref.shape[0]
        if rope:
            fill(pl.ds(0, past), pckv_ref[...], pkr_ref[...])
            fill(pl.ds(past, seq), ckv_ref[...],
                 kr_ref[...] * cos_ref[...] + krr_ref[...] * sin_ref[...])
        else:
            fill(pl.ds(0, seq), ckv_ref[...], kr_ref[...])

    cq = cq_ref[...]
    qa = _dot(cq, wq_ref[...])
    if rope:
        qb = _dot(cq, wqr_ref[...])
        q0 = pl.multiple_of(qi * tq, tq)
        cos_t = cos_ref[pl.ds(q0, tq), :]
        sin_t = sin_ref[pl.ds(q0, tq), :]
    for j in range(N_HEADS // 2):
        pair = None
        for h in (2 * j, 2 * j + 1):
            cols = slice(h * HEAD_PAD, (h + 1) * HEAD_PAD)
            qh = qa[:, cols]
            if rope:
                qh = qh * cos_t + qb[:, cols] * sin_t
            s = lax.dot_general(qh.astype(BF16), k_scr[:, cols], (((1,), (1,)), ((), ())),
                                preferred_element_type=F32)
            e = jnp.exp((s - jnp.max(s, axis=-1, keepdims=True)) * scale)
            denom = jnp.sum(e, axis=-1, keepdims=True)
            o = _dot(e.astype(BF16), v_scr[:, cols]) * (1.0 / denom)
            pair = o if pair is None else pair + o
        o_ref[:, j * LANE:(j + 1) * LANE] = pair.astype(BF16)


def _attention(layer, cq, ckv, kr, row_start, nseq, seq, w_q, w_k, w_v, lat=None):
    rope = lat is not None
    tq = min(_ATT_Q_TILE, seq)
    assert seq % tq == 0 and row_start % seq == 0
    past = lat[1].shape[2] if rope else 0
    total = past + seq
    hp = N_HEADS * HEAD_PAD
    nq = seq // tq
    s_off, q_off = row_start // seq, row_start // tq
    full = lambda w: pl.BlockSpec((seq, w), lambda b, q: (s_off + b, 0))
    q_rows = lambda w: pl.BlockSpec((tq, w), lambda b, q: (q_off + b * nq + q, 0))
    in_specs = [q_rows(Q_LORA), full(KV_LORA), full(HEAD_PAD)]
    args = [cq, ckv, kr]
    if rope:
        krr, pckv, pkr, cos, sin, w_q_rot = lat
        in_specs += [full(HEAD_PAD),
                     pl.BlockSpec((None, None, past, KV_LORA), lambda b, q: (b, layer, 0, 0)),
                     pl.BlockSpec((None, None, past, HEAD_PAD), lambda b, q: (b, layer, 0, 0)),
                     _fixed_spec((seq, HEAD_PAD)), _fixed_spec((seq, HEAD_PAD))]
        args += [krr, pckv, pkr, cos, sin]
    in_specs.append(_fixed_spec((Q_LORA, hp), (layer,)))
    args.append(w_q)
    if rope:
        in_specs.append(_fixed_spec((Q_LORA, hp), (layer,)))
        args.append(w_q_rot)
    in_specs += [_fixed_spec((KV_LORA, hp), (layer,)), _fixed_spec((KV_LORA, hp), (layer,))]
    args += [w_k, w_v]
    return pl.pallas_call(
        functools.partial(_attn_kernel, past=past, tq=tq, rope=rope),
        grid=(nseq, nq),
        in_specs=in_specs,
        out_specs=pl.BlockSpec((tq, N_HEADS * V_HEAD), lambda b, q: (b * nq + q, 0)),
        out_shape=jax.ShapeDtypeStruct((nseq * seq, N_HEADS * V_HEAD), BF16),
        scratch_shapes=[pltpu.VMEM((total, hp), BF16), pltpu.VMEM((total, hp), BF16)],
        compiler_params=_params("arbitrary", "arbitrary"),
        name=f"attn_{seq}",
    )(*args)


def _mix_kernel(x_ref, hsc_ref, hsl_ref, atc_ref, atl_ref, m_ref, wygg_ref, wro_ref, wao_ref, wo_ref,
                g_ref, b_ref, o_ref, wro_bf, wao_bf, wo_bf, *, n_ctx_tiles):
    _cast_once([(wro_ref, wro_bf), (wao_ref, wao_bf), (wo_ref, wo_bf)])
    is_ctx = pl.program_id(0) < n_ctx_tiles
    hs = jnp.where(is_ctx, hsc_ref[...], hsl_ref[...])
    at = jnp.where(is_ctx, atc_ref[...], atl_ref[...])
    x = x_ref[...]
    h = (x * (1.0 + m_ref[1:2, :]) + m_ref[0:1, :]).astype(BF16)
    ygg = _dot(h, wygg_ref[...])
    u = (hs * jax.nn.gelu(ygg[:, :D_RNN])).astype(BF16)
    rnn_out = _dot(u, wro_bf[...])
    att_out = _dot(at, wao_bf[...])
    merged = (jax.nn.sigmoid(ygg[:, D_RNN:D_RNN + D_MODEL]) * rnn_out
              + jax.nn.sigmoid(ygg[:, D_RNN + D_MODEL:]) * att_out)
    m = _dot(merged.astype(BF16), wo_bf[...])
    o_ref[...] = _layer_norm(ALPHA * x + m_ref[2:3, :] * m, g_ref[...], b_ref[...])


def _mix(layer, x, hs_ctx, hs_lat, at_ctx, at_lat, mods, w_ygg, w_ro, w_ao, w_o, ln_g, ln_b, n_ctx_rows,
         rows_per_latent):
    n_tok = x.shape[0]
    tm = _ROW_TILE
    nct = n_ctx_rows // tm
    row = pl.BlockSpec((tm, D_MODEL), lambda i: (i, 0))
    ctx_row = pl.BlockSpec((tm, D_MODEL), lambda i: (jnp.minimum(i, nct - 1), 0))
    lat_row = pl.BlockSpec((tm, D_MODEL), lambda i: (jnp.maximum(i - nct, 0), 0))
    sq = (D_MODEL, D_MODEL)
    return pl.pallas_call(
        functools.partial(_mix_kernel, n_ctx_tiles=nct),
        grid=(n_tok // tm,),
        in_specs=[row, ctx_row, lat_row, ctx_row, lat_row,
                  _mod_spec(layer, tm, n_ctx_rows, rows_per_latent),
                  _fixed_spec((D_MODEL, D_RNN + 2 * D_MODEL), (layer,)),
                  _fixed_spec(sq, (layer,)), _fixed_spec(sq, (layer,)), _fixed_spec(sq, (layer,)),
                  _fixed_spec((1, D_MODEL), (layer,)), _fixed_spec((1, D_MODEL), (layer,))],
        out_specs=row,
        out_shape=jax.ShapeDtypeStruct((n_tok, D_MODEL), F32),
        scratch_shapes=[pltpu.VMEM(sq, BF16) for _ in range(3)],
        compiler_params=_params("arbitrary"),
        name="mix_out",
    )(x, hs_ctx, hs_lat, at_ctx, at_lat, mods, w_ygg, w_ro, w_ao, w_o, ln_g, ln_b)


def _ffn_kernel(x_ref, m_ref, w1_ref, w3_ref, w2_ref, g_ref, b_ref, o_ref):
    x = x_ref[...]
    h = (x * (1.0 + m_ref[4:5, :]) + m_ref[3:4, :]).astype(BF16)
    f = None
    for c in range(D_FF // _FF_CHUNK):
        cols = slice(c * _FF_CHUNK, (c + 1) * _FF_CHUNK)
        act = (jax.nn.silu(_dot(h, w1_ref[:, cols])) * _dot(h, w3_ref[:, cols])).astype(BF16)
        y = _dot(act, w2_ref[cols, :])
        f = y if f is None else f + y
    o_ref[...] = _layer_norm(ALPHA * x + m_ref[5:6, :] * f, g_ref[...], b_ref[...])


def _ffn_dense(layer, x, mods, w1, w3, w2, ln_g, ln_b, n_ctx_rows, rows_per_latent):
    n_tok = x.shape[0]
    tm = _ROW_TILE
    j = layer // 2
    row = pl.BlockSpec((tm, D_MODEL), lambda i: (i, 0))
    return pl.pallas_call(
        _ffn_kernel,
        grid=(n_tok // tm,),
        in_specs=[row, _mod_spec(layer, tm, n_ctx_rows, rows_per_latent),
                  _fixed_spec((D_MODEL, D_FF), (j,)), _fixed_spec((D_MODEL, D_FF), (j,)),
                  _fixed_spec((D_FF, D_MODEL), (j,)),
                  _fixed_spec((1, D_MODEL), (layer,)), _fixed_spec((1, D_MODEL), (layer,))],
        out_specs=row,
        out_shape=jax.ShapeDtypeStruct((n_tok, D_MODEL), F32),
        compiler_params=_params("arbitrary"),
        name="ffn_dense",
    )(x, mods, w1, w3, w2, ln_g, ln_b)


def _split_bf16(a):
    hi = a.astype(BF16)
    return hi, (a - hi.astype(F32)).astype(BF16)


def _route_kernel(x_ref, m_ref, wr_ref, br_ref, h_ref, ids_ref, gw_ref, cnt_ref, run_scr, tri_scr):
    tm = x_ref.shape[0]

    @pl.when(pl.program_id(0) == 0)
    def _():
        run_scr[...] = jnp.zeros_like(run_scr)
        r = lax.broadcasted_iota(jnp.int32, (tm, tm), 0)
        c = lax.broadcasted_iota(jnp.int32, (tm, tm), 1)
        tri_scr[...] = jnp.where(c < r, 1.0, 0.0).astype(BF16)

    h = x_ref[...] * (1.0 + m_ref[4:5, :]) + m_ref[3:4, :]
    h_ref[...] = h
    h_hi, h_lo = _split_bf16(h)

    w_hi, w_lo = _split_bf16(wr_ref[...])
    logits = _dot(h_hi, w_hi) + (_dot(h_hi, w_lo) + _dot(h_lo, w_hi)) + br_ref[...]
    lane = lax.broadcasted_iota(jnp.int32, logits.shape, 1)
    big = jnp.int32(N_EXPERTS)
    v1 = jnp.max(logits, axis=-1, keepdims=True)
    i1 = jnp.min(jnp.where(logits == v1, lane, big), axis=-1, keepdims=True)
    rest = jnp.where(lane == i1, -jnp.inf, logits)
    v2 = jnp.max(rest, axis=-1, keepdims=True)
    i2 = jnp.min(jnp.where(rest == v2, lane, big), axis=-1, keepdims=True)
    e2 = jnp.exp(v2 - v1)
    g1 = 1.0 / (1.0 + e2)
    g2 = e2 / (1.0 + e2)

    m1, m2 = lane == i1, lane == i2
    member = jnp.where(m1, 1.0, 0.0) + jnp.where(m2, 1.0, 0.0)
    before = _dot(tri_scr[...], member.astype(BF16)) + run_scr[...]
    r1 = jnp.sum(jnp.where(m1, before, 0.0), axis=-1, keepdims=True).astype(jnp.int32)
    r2 = jnp.sum(jnp.where(m2, before, 0.0), axis=-1, keepdims=True).astype(jnp.int32)
    run_scr[...] += jnp.sum(member, axis=0, keepdims=True)

    ids_ref[...] = jnp.where(lane == 0, i1, jnp.where(lane == 1, i2, jnp.where(lane == 2, r1, r2)))
    gw_ref[...] = jnp.where(lane == 0, g1, g2)
    cnt_ref[...] = run_scr[...].astype(jnp.int32)


def _moe_route(layer, x, mods, w_router, b_router, n_ctx_rows, rows_per_latent):
    n_tok = x.shape[0]
    tm = _ROW_TILE
    j = layer // 2
    row = lambda w: pl.BlockSpec((tm, w), lambda i: (i, 0))
    return pl.pallas_call(
        _route_kernel,
        grid=(n_tok // tm,),
        in_specs=[row(D_MODEL), _mod_spec(layer, tm, n_ctx_rows, rows_per_latent),
                  _fixed_spec((D_MODEL, N_EXPERTS), (j,)), _fixed_spec((1, N_EXPERTS), (j,))],
        out_specs=[row(D_MODEL), row(N_EXPERTS), row(N_EXPERTS),
                   pl.BlockSpec((1, N_EXPERTS), lambda i: (0, 0))],
        out_shape=[jax.ShapeDtypeStruct((n_tok, D_MODEL), F32),
                   jax.ShapeDtypeStruct((n_tok, N_EXPERTS), jnp.int32),
                   jax.ShapeDtypeStruct((n_tok, N_EXPERTS), F32),
                   jax.ShapeDtypeStruct((1, N_EXPERTS), jnp.int32)],
        scratch_shapes=[pltpu.VMEM((1, N_EXPERTS), F32), pltpu.VMEM((tm, tm), BF16)],
        compiler_params=_params("arbitrary"),
        name="moe_route",
    )(x, mods, w_router, b_router)


def _slot_tables(ids, gw, counts, n_tok):
    tm = _MOE_SLOT_TILE
    n_slots = TOP_K * n_tok + N_EXPERTS * tm
    cnt = counts[0]
    padded = ((cnt + tm - 1) // tm) * tm
    ends = jnp.cumsum(padded)
    starts = ends - padded
    dest = (jnp.take(starts, ids[:, :TOP_K]) + ids[:, TOP_K:2 * TOP_K]).reshape(-1)
    tok = jnp.repeat(jnp.arange(n_tok, dtype=jnp.int32), TOP_K)
    slot_tok = jnp.full((n_slots,), n_tok, jnp.int32).at[dest].set(tok)
    slot_gate = jnp.zeros((n_slots,), F32).at[dest].set(gw[:, :TOP_K].reshape(-1))
    tile_start = jnp.arange(n_slots // tm, dtype=jnp.int32) * tm
    tile_exp = jnp.sum((tile_start[:, None] >= ends[None, :]).astype(jnp.int32), axis=1)
    tile_exp = jnp.minimum(tile_exp, N_EXPERTS - 1)
    tile_on = (tile_start < ends[-1]).astype(jnp.int32)
    return tile_exp, tile_on, slot_tok, slot_gate.reshape(n_slots, 1)


def _moe_up_kernel(exp_ref, on_ref, tok_ref, h_ref, w1_ref, w3_ref, act_ref, xg_scr, *, n_tok):
    t = pl.program_id(0)
    tm = xg_scr.shape[0]

    @pl.when(on_ref[t] == 1)
    def _():
        def gather(r, carry):
            src = jnp.minimum(tok_ref[t * tm + r], n_tok - 1)
            xg_scr[pl.ds(r, 1), :] = h_ref[pl.ds(src, 1), :]
            return carry

        lax.fori_loop(0, tm, gather, 0, unroll=8)
        h = xg_scr[...].astype(BF16)
        act_ref[...] = (jax.nn.silu(_dot(h, w1_ref[...])) * _dot(h, w3_ref[...])).astype(BF16)

    @pl.when(on_ref[t] == 0)
    def _():
        act_ref[...] = jnp.zeros_like(act_ref)


def _moe_up(layer, tables, h, w1, w3):
    tile_exp, tile_on, slot_tok, _ = tables
    n_tok = h.shape[0]
    tm = _MOE_SLOT_TILE
    n_tiles = tile_exp.shape[0]
    j = layer // 2
    wspec = pl.BlockSpec((None, None, D_MODEL, D_EXPERT), lambda t, ex, on, tk: (j, ex[t], 0, 0),
                         pipeline_mode=pl.Buffered(1))
    return pl.pallas_call(
        functools.partial(_moe_up_kernel, n_tok=n_tok),
        grid_spec=pltpu.PrefetchScalarGridSpec(
            num_scalar_prefetch=3,
            grid=(n_tiles,),
            in_specs=[pl.BlockSpec((n_tok, D_MODEL), lambda t, ex, on, tk: (0, 0),
                                   pipeline_mode=pl.Buffered(1)),
                      wspec, wspec],
            out_specs=pl.BlockSpec((tm, D_EXPERT), lambda t, ex, on, tk: (t, 0)),
            scratch_shapes=[pltpu.VMEM((tm, D_MODEL), F32)],
        ),
        out_shape=jax.ShapeDtypeStruct((n_tiles * tm, D_EXPERT), BF16),
        compiler_params=_params("arbitrary"),
        name="moe_up",
    )(tile_exp, tile_on, slot_tok, h, w1, w3)


def _moe_down_kernel(exp_ref, on_ref, tok_ref, act_ref, gate_ref, w2_ref, x_ref, m_ref, g_ref, b_ref,
                     o_ref, acc_scr, y_scr, *, n_tiles, ln_rows):
    t = pl.program_id(0)
    tm = y_scr.shape[0]

    @pl.when(t == 0)
    def _():
        acc_scr[...] = jnp.zeros_like(acc_scr)

    @pl.when(jnp.logical_and(t < n_tiles, on_ref[jnp.minimum(t, n_tiles - 1)] == 1))
    def _():
        y_scr[...] = _dot(act_ref[...], w2_ref[...]) * gate_ref[...]

        def scatter(g, carry):
            base = pl.multiple_of(g * SUBLANE, SUBLANE)
            toks = [tok_ref[t * tm + base + u] for u in range(SUBLANE)]
            rows = [acc_scr[pl.ds(toks[u], 1), :] for u in range(SUBLANE)]
            ys = y_scr[pl.ds(base, SUBLANE), :]
            for u in range(SUBLANE):
                acc_scr[pl.ds(toks[u], 1), :] = rows[u] + ys[u:u + 1, :]
            return carry

        lax.fori_loop(0, tm // SUBLANE, scatter, 0)

    @pl.when(t >= n_tiles)
    def _():
        r0 = pl.multiple_of((t - n_tiles) * ln_rows, ln_rows)
        f = acc_scr[pl.ds(r0, ln_rows), :]
        o_ref[...] = _layer_norm(ALPHA * x_ref[...] + m_ref[5:6, :] * f, g_ref[...], b_ref[...])


def _moe_down(layer, tables, act, x, mods, w2, ln_g, ln_b, n_ctx_rows, rows_per_latent):
    tile_exp, tile_on, slot_tok, slot_gate = tables
    n_tok = x.shape[0]
    tm = _MOE_SLOT_TILE
    tl = _ROW_TILE
    n_tiles = tile_exp.shape[0]
    j = layer // 2
    slot = lambda t: jnp.minimum(t, n_tiles - 1)
    ln = lambda t: jnp.maximum(t - n_tiles, 0)
    n_ctx_tiles, per = n_ctx_rows // tl, rows_per_latent // tl

    def mod_index(t, *_):
        i = ln(t)
        return (layer, jnp.where(i < n_ctx_tiles, 0, 1 + (i - n_ctx_tiles) // per), 0, 0)

    fixed = lambda shape, lead: pl.BlockSpec((None,) + shape, lambda *_: (lead,) + (0,) * len(shape),
                                             pipeline_mode=pl.Buffered(1))
    return pl.pallas_call(
        functools.partial(_moe_down_kernel, n_tiles=n_tiles, ln_rows=tl),
        grid_spec=pltpu.PrefetchScalarGridSpec(
            num_scalar_prefetch=3,
            grid=(n_tiles + n_tok // tl,),
            in_specs=[pl.BlockSpec((tm, D_EXPERT), lambda t, ex, on, tk: (slot(t), 0)),
                      pl.BlockSpec((tm, 1), lambda t, ex, on, tk: (slot(t), 0)),
                      pl.BlockSpec((None, None, D_EXPERT, D_MODEL), lambda t, ex, on, tk: (j, ex[slot(t)], 0, 0)),
                      pl.BlockSpec((tl, D_MODEL), lambda t, ex, on, tk: (ln(t), 0)),
                      pl.BlockSpec((None, None, 6, D_MODEL), mod_index),
                      fixed((1, D_MODEL), layer), fixed((1, D_MODEL), layer)],
            out_specs=pl.BlockSpec((tl, D_MODEL), lambda t, ex, on, tk: (ln(t), 0)),
            scratch_shapes=[pltpu.VMEM((n_tok + SUBLANE, D_MODEL), F32), pltpu.VMEM((tm, D_MODEL), F32)],
        ),
        out_shape=jax.ShapeDtypeStruct((n_tok, D_MODEL), F32),
        compiler_params=_params("arbitrary"),
        name="moe_down",
    )(tile_exp, tile_on, slot_tok, act, slot_gate, w2, x, mods, ln_g, ln_b)


def _ffn_moe(layer, x, mods, w_router, b_router, w1, w3, w2, ln_g, ln_b, n_ctx_rows, rows_per_latent):
    h, ids, gw, counts = _moe_route(layer, x, mods, w_router, b_router, n_ctx_rows, rows_per_latent)
    tables = _slot_tables(ids, gw, counts, x.shape[0])
    act = _moe_up(layer, tables, h, w1, w3)
    return _moe_down(layer, tables, act, x, mods, w2, ln_g, ln_b, n_ctx_rows, rows_per_latent)


def _rot_cols(w):
    ws = w.reshape(w.shape[:-1] + (2, 2, ROPE_AXIS // 2))
    return jnp.stack([-ws[..., 1, :], ws[..., 0, :]], axis=-2).reshape(w.shape)


def _pad_rope_cols(w):
    return jnp.pad(w, ((0, 0),) * (w.ndim - 1) + ((ROPE_OFF, HEAD_PAD - ROPE_OFF - QK_ROPE),))


def _rope_tables(rows):
    r = jnp.repeat(jnp.arange(rows, dtype=F32), GRID_W)
    col = jnp.tile(jnp.arange(GRID_W, dtype=F32), rows)
    freqs = ROPE_THETA ** (-jnp.arange(0, ROPE_AXIS, 2, dtype=F32) / ROPE_AXIS)
    ar = r[:, None] * freqs
    ac = col[:, None] * freqs
    ang = jnp.concatenate([ar, ar, ac, ac], axis=-1)
    cos, sin = jnp.cos(ang), jnp.sin(ang)
    n = cos.shape[0]
    cos_p = jnp.concatenate([jnp.ones((n, ROPE_OFF), F32), cos,
                             jnp.zeros((n, HEAD_PAD - ROPE_OFF - QK_ROPE), F32)], axis=-1)
    return cos_p, _pad_rope_cols(sin)


def _prepared_weights(w_in, rg_wa, rg_ba, rg_wx, rg_bx, w_qb, w_kvb):
    bf = lambda a: a.astype(BF16)
    w_kr = w_in[:, :, _COL_KR:_COL_KR + QK_ROPE]
    w_s = jnp.concatenate([w_in[:, :, _COL_Q:_COL_KR], _pad_rope_cols(w_kr),
                           _pad_rope_cols(_rot_cols(w_kr))], axis=-1)
    w_x = w_in[:, :, :D_RNN]
    w_ygg = jnp.concatenate([w_in[:, :, D_RNN:2 * D_RNN], w_in[:, :, _COL_GATE:]], axis=-1)

    wq = w_qb.reshape(DEPTH, Q_LORA, N_HEADS, QK_NOPE + QK_ROPE)
    q_nope, q_rope = wq[..., :QK_NOPE], wq[..., QK_NOPE:]
    tail = jnp.zeros((DEPTH, Q_LORA, N_HEADS, HEAD_PAD - QK_NOPE - QK_ROPE), F32)
    hp = N_HEADS * HEAD_PAD
    w_q = jnp.concatenate([q_nope, q_rope, tail], axis=-1).reshape(DEPTH, Q_LORA, hp)
    w_q_rot = jnp.concatenate([jnp.zeros_like(q_nope), _rot_cols(q_rope), tail], axis=-1)
    w_q_rot = w_q_rot.reshape(DEPTH, Q_LORA, hp)
    wkv = w_kvb.reshape(DEPTH, KV_LORA, N_HEADS, QK_NOPE + V_HEAD)
    k_nope, v = wkv[..., :QK_NOPE], wkv[..., QK_NOPE:]
    w_k = jnp.concatenate([k_nope, jnp.zeros((DEPTH, KV_LORA, N_HEADS, HEAD_PAD - QK_NOPE), F32)], axis=-1)
    w_k = w_k.reshape(DEPTH, KV_LORA, hp)
    vz = jnp.zeros_like(v)
    odd = (jnp.arange(N_HEADS) % 2 == 1)[None, None, :, None]
    w_v = jnp.where(odd, jnp.concatenate([vz, v], axis=-1), jnp.concatenate([v, vz], axis=-1))
    w_v = w_v.reshape(DEPTH, KV_LORA, hp)

    nblk = RNN_GROUP // RNN_BW
    ng = D_RNN // RNN_GROUP
    eye = jnp.eye(nblk, dtype=F32)

    def block_diag(w):
        wg = w.reshape(DEPTH, ng, nblk, RNN_BW, RNN_BW)
        return jnp.einsum('lgnjk,nm->lgnjmk', wg, eye).reshape(DEPTH, ng, RNN_GROUP, RNN_GROUP)

    w_gate = jnp.concatenate([block_diag(rg_wa[:, 0]), block_diag(rg_wx[:, 0]),
                              block_diag(rg_wa[:, 1]), block_diag(rg_wx[:, 1])], axis=-1)
    grp = lambda b: b.reshape(DEPTH, ng, 1, RNN_GROUP)
    b_gate = jnp.concatenate([grp(rg_ba[:, 0]), grp(rg_bx[:, 0]), grp(rg_ba[:, 1]), grp(rg_bx[:, 1])],
                             axis=-1)
    return dict(w_x=bf(w_x), w_s=bf(w_s), w_ygg=bf(w_ygg), w_q=bf(w_q), w_q_rot=bf(w_q_rot), w_k=bf(w_k), w_v=bf(w_v),
                w_gate=bf(w_gate), b_gate=b_gate)


def kernel(x_prompt, x_sample, cache_ckv, cache_krope, state_rnn, c, c_ctx, w_ada, b_ada, w_in, conv_w, conv_b, rg_wa, rg_ba, rg_wx, rg_bx, rg_lambda, w_rnn_out, q_norm_g, w_qb, kv_norm_g, w_kvb, w_attn_out, w_out, ln1_g, ln1_b, ln2_g, ln2_b, w1_dense, w3_dense, w2_dense, w_router, b_router, w1_exp, w3_exp, w2_exp):
    nb, seq, _ = x_prompt.shape
    db, dseq, _ = x_sample.shape
    n_ctx = nb * seq
    n_lat = db * dseq
    assert n_ctx % _ROW_TILE == 0 and dseq % _ROW_TILE == 0 and db + 1 <= 8
    bf = lambda a: a.astype(BF16)
    vec = lambda a: a.reshape(a.shape[0], 1, a.shape[-1])

    cond8 = jnp.concatenate([c_ctx[None, :], c, jnp.zeros((8 - 1 - db, D_MODEL), F32)], axis=0)
    mods = _ada_mods(cond8, w_ada, b_ada)
    cos_p, sin_p = _rope_tables(dseq // GRID_W)
    past_kr = _pad_rope_cols(cache_krope)
    pw = _prepared_weights(w_in, rg_wa, rg_ba, rg_wx, rg_bx, w_qb, w_kvb)
    w1d, w3d, w2d = bf(w1_dense), bf(w3_dense), bf(w2_dense)
    w1e, w3e, w2e = bf(w1_exp), bf(w3_exp), bf(w2_exp)
    qg, kvg, cb = vec(q_norm_g), vec(kv_norm_g), vec(conv_b)
    l1g, l1b, l2g, l2b = vec(ln1_g), vec(ln1_b), vec(ln2_g), vec(ln2_b)
    br = vec(b_router)
    tiles = (n_ctx, dseq)

    x = jnp.concatenate([x_prompt.reshape(n_ctx, D_MODEL), x_sample.reshape(n_lat, D_MODEL)], axis=0)
    new_ckv, new_kr, new_h = [], [], []
    for l in range(DEPTH):
        xr, cq, ckv, kr, krr = _in_proj(l, x, mods, pw['w_x'], pw['w_s'], qg, kvg, *tiles)
        new_ckv.append(ckv[:n_ctx].reshape(nb, seq, KV_LORA))
        new_kr.append(kr[:n_ctx, ROPE_OFF:ROPE_OFF + QK_ROPE].reshape(nb, seq, QK_ROPE))

        rnn_args = (conv_w, cb, pw['w_gate'], pw['b_gate'], rg_lambda)
        hs_ctx, h_last = _rglru(l, xr, 0, nb, seq, *rnn_args, emit_last=True)
        (hs_lat,) = _rglru(l, xr, n_ctx, db, dseq, *rnn_args, h0=state_rnn)
        new_h.append(h_last)

        att_w = (pw['w_q'], pw['w_k'], pw['w_v'])
        at_ctx = _attention(l, cq, ckv, kr, 0, nb, seq, *att_w)
        at_lat = _attention(l, cq, ckv, kr, n_ctx, db, dseq, *att_w,
                            lat=(krr, cache_ckv, past_kr, cos_p, sin_p, pw['w_q_rot']))

        x = _mix(l, x, hs_ctx, hs_lat, at_ctx, at_lat, mods, pw['w_ygg'], w_rnn_out, w_attn_out, w_out,
                 l1g, l1b, *tiles)
        if l % 2 == 0:
            x = _ffn_dense(l, x, mods, w1d, w3d, w2d, l2g, l2b, *tiles)
        else:
            x = _ffn_moe(l, x, mods, w_router, br, w1e, w3e, w2e, l2g, l2b, *tiles)

    return (x[:n_ctx].reshape(nb, seq, D_MODEL), x[n_ctx:].reshape(db, dseq, D_MODEL),
            jnp.stack(new_ckv, axis=1), jnp.stack(new_kr, axis=1), jnp.stack(new_h, axis=1))
```

```python
import functools
import math

import jax
import jax.numpy as jnp
from jax import lax
from jax.experimental import pallas as pl
from jax.experimental.pallas import tpu as pltpu

F32 = jnp.float32
BF16 = jnp.bfloat16

D_MODEL = 1024
DEPTH = 4
GRID_W = 64
D_RNN = D_MODEL
RNN_BLOCKS = 16
RNN_BW = D_RNN // RNN_BLOCKS
CONV_W = 4
RG_C = 8.0
N_HEADS = 16
QK_NOPE = 64
QK_ROPE = 32
V_HEAD = 64
Q_LORA = 256
KV_LORA = 128
ROPE_AXIS = QK_ROPE // 2
ROPE_THETA = 10000.0
D_FF = 2816
N_EXPERTS = 8
TOP_K = 2
D_EXPERT = 1408
ALPHA = (2 * DEPTH) ** 0.25
LN_EPS = 1e-5
RMS_EPS = 1e-6

LANE = 128
SUBLANE = 8
HEAD_PAD = LANE
ROPE_OFF = QK_NOPE
RNN_GROUP = 256
VMEM_LIMIT = 56 * 1024 * 1024

_ROW_TILE = 512
_MOE_SLOT_TILE = 512
_MOE_LN_TILE = 256
_ATT_Q_TILE = 512
_FF_CHUNK = 1408

_COL_Q = 2 * D_RNN
_COL_KR = _COL_Q + Q_LORA + KV_LORA
_COL_GATE = _COL_KR + QK_ROPE


def _dot(a, b):
    return jnp.dot(a, b, preferred_element_type=F32)


def _layer_norm(y, g, b):
    mu = jnp.mean(y, axis=-1, keepdims=True)
    d = y - mu
    var = jnp.mean(d * d, axis=-1, keepdims=True)
    return d * lax.rsqrt(var + LN_EPS) * g + b


def _rms_norm(y, g):
    return y * lax.rsqrt(jnp.mean(y * y, axis=-1, keepdims=True) + RMS_EPS) * g


def _params(*sem):
    return pltpu.CompilerParams(dimension_semantics=sem, vmem_limit_bytes=VMEM_LIMIT)


def _fixed_spec(shape, lead=(), tail=None):
    tail = (0,) * len(shape) if tail is None else tuple(tail)
    index = tuple(lead) + tail
    return pl.BlockSpec((None,) * len(lead) + tuple(shape), lambda *_: index,
                        pipeline_mode=pl.Buffered(1))


def _mod_spec(layer, tm, n_ctx_rows, rows_per_latent):
    n_ctx_tiles = n_ctx_rows // tm
    per = rows_per_latent // tm

    def index(i, *_):
        return (layer, jnp.where(i < n_ctx_tiles, 0, 1 + (i - n_ctx_tiles) // per), 0, 0)

    return pl.BlockSpec((None, None, 6, D_MODEL), index)


def _cast_once(pairs):
    @pl.when(pl.program_id(0) == 0)
    def _():
        for src, dst in pairs:
            dst[...] = src[...].astype(BF16)


def _mods_kernel(c_ref, w_ref, b_ref, o_ref):
    s = jax.nn.silu(c_ref[...]).astype(BF16)
    o_ref[...] = _dot(s, w_ref[...].astype(BF16)) + b_ref[...]


def _ada_mods(cond8, w_ada, b_ada):
    out = pl.pallas_call(
        _mods_kernel,
        grid=(DEPTH, 6),
        in_specs=[
            pl.BlockSpec((8, D_MODEL), lambda l, j: (0, 0)),
            pl.BlockSpec((None, D_MODEL, D_MODEL), lambda l, j: (l, 0, j)),
            pl.BlockSpec((None, 1, D_MODEL), lambda l, j: (l, 0, j)),
        ],
        out_specs=pl.BlockSpec((None, 8, D_MODEL), lambda l, j: (l, 0, j)),
        out_shape=jax.ShapeDtypeStruct((DEPTH, 8, 6 * D_MODEL), F32),
        compiler_params=_params("arbitrary", "arbitrary"),
        name="ada_mods",
    )(cond8, w_ada, b_ada.reshape(DEPTH, 1, 6 * D_MODEL))
    return out.reshape(DEPTH, 8, 6, D_MODEL)


_S_COLS = Q_LORA + KV_LORA + 2 * HEAD_PAD


def _inproj_kernel(x_ref, m_ref, wx_ref, ws_ref, qg_ref, kvg_ref,
                   xr_ref, cq_ref, ckv_ref, kr_ref, krr_ref):
    h = (x_ref[...] * (1.0 + m_ref[1:2, :]) + m_ref[0:1, :]).astype(BF16)
    xr_ref[...] = _dot(h, wx_ref[...])
    s = _dot(h, ws_ref[...])
    cq_ref[...] = _rms_norm(s[:, :Q_LORA], qg_ref[...]).astype(BF16)
    ckv_ref[...] = _rms_norm(s[:, Q_LORA:Q_LORA + KV_LORA], kvg_ref[...])
    kr_ref[...] = s[:, Q_LORA + KV_LORA:Q_LORA + KV_LORA + HEAD_PAD]
    krr_ref[...] = s[:, Q_LORA + KV_LORA + HEAD_PAD:]


def _in_proj(layer, x, mods, w_x, w_s, q_g, kv_g, n_ctx_rows, rows_per_latent):
    n_tok = x.shape[0]
    tm = _ROW_TILE
    row = lambda w: pl.BlockSpec((tm, w), lambda i: (i, 0))
    return pl.pallas_call(
        _inproj_kernel,
        grid=(n_tok // tm,),
        in_specs=[
            row(D_MODEL),
            _mod_spec(layer, tm, n_ctx_rows, rows_per_latent),
            _fixed_spec((D_MODEL, D_RNN), (layer,)),
            _fixed_spec((D_MODEL, _S_COLS), (layer,)),
            _fixed_spec((1, Q_LORA), (layer,)),
            _fixed_spec((1, KV_LORA), (layer,)),
        ],
        out_specs=[row(D_RNN), row(Q_LORA), row(KV_LORA), row(HEAD_PAD), row(HEAD_PAD)],
        out_shape=[
            jax.ShapeDtypeStruct((n_tok, D_RNN), F32),
            jax.ShapeDtypeStruct((n_tok, Q_LORA), BF16),
            jax.ShapeDtypeStruct((n_tok, KV_LORA), F32),
            jax.ShapeDtypeStruct((n_tok, HEAD_PAD), F32),
            jax.ShapeDtypeStruct((n_tok, HEAD_PAD), F32),
        ],
        compiler_params=_params("arbitrary"),
        name="in_proj",
    )(x, mods, w_x, w_s, q_g, kv_g)


def _segment_pitch(seg):
    assert seg % SUBLANE == 0
    return seg if (seg // SUBLANE) % 2 == 1 else seg + SUBLANE


def _rglru_kernel(*refs, seq, bt, has_h0, emit_last):
    it = iter(refs)
    x_ref, cw_ref, cb_ref, wg_ref, bg_ref, lam_ref = (next(it) for _ in range(6))
    h0_ref = next(it) if has_h0 else None
    hs_ref = next(it)
    hl_ref = next(it) if emit_last else None
    a_refs = (next(it), next(it))
    b_refs = (next(it), next(it))
    dg = RNN_GROUP
    nslab = dg // LANE
    seg = seq // SUBLANE
    pitch = _segment_pitch(seg)

    row = lax.broadcasted_iota(jnp.int32, (seq, dg), 0)
    neg_softplus = [jax.nn.softplus(-lam_ref[d:d + 1, :]) for d in range(2)]
    for b in range(bt):
        rows = slice(b * seq, (b + 1) * seq)
        x = x_ref[rows, :]
        xm1 = jnp.where(row >= 1, pltpu.roll(x, 1, 0), 0.0)
        xp1 = jnp.where(row < seq - 1, pltpu.roll(x, seq - 1, 0), 0.0)
        xp2 = jnp.where(row < seq - 2, pltpu.roll(x, seq - 2, 0), 0.0)
        xc = cb_ref[...] + xm1 * cw_ref[0:1, :]
        xc = xc + x * cw_ref[1:2, :]
        xc = xc + xp1 * cw_ref[2:3, :]
        xc = xc + xp2 * cw_ref[3:4, :]
        gates = _dot(xc.astype(BF16), wg_ref[...]) + bg_ref[...]
        for d in range(2):
            r = jax.nn.sigmoid(gates[:, (2 * d) * dg:(2 * d + 1) * dg])
            i = jax.nn.sigmoid(gates[:, (2 * d + 1) * dg:(2 * d + 2) * dg])
            log_a = -RG_C * r * neg_softplus[d]
            a = jnp.exp(log_a)
            bq = jnp.sqrt(jnp.tanh(-log_a) * (a * a + 1.0)) * (i * xc)
            for s in range(SUBLANE):
                for c in range(nslab):
                    src = (slice(s * seg, (s + 1) * seg), slice(c * LANE, (c + 1) * LANE))
                    dst = slice(s * pitch, s * pitch + seg)
                    a_refs[d][b * nslab + c, dst, :] = a[src]
                    b_refs[d][b * nslab + c, dst, :] = bq[src]

    nch = bt * nslab
    zeros = tuple(jnp.zeros((SUBLANE, LANE), F32) for _ in range(nch))
    ones = tuple(jnp.ones((SUBLANE, LANE), F32) for _ in range(nch))

    def step(j, carry):
        out = []
        for d, t in ((0, j), (1, seg - 1 - j)):
            h_prev, p_prev = carry[2 * d], carry[2 * d + 1]
            h_new, p_new = [], []
            idx = pl.ds(t, SUBLANE, stride=pitch)
            for ch in range(nch):
                a = a_refs[d].at[ch][idx, :]
                h = a * h_prev[ch] + b_refs[d].at[ch][idx, :]
                p = a * p_prev[ch]
                b_refs[d].at[ch][idx, :] = h
                a_refs[d].at[ch][idx, :] = p
                h_new.append(h)
                p_new.append(p)
            out += [tuple(h_new), tuple(p_new)]
        return tuple(out)

    h_f, p_f, h_b, p_b = lax.fori_loop(0, seg, step, (zeros, ones, zeros, ones), unroll=2)

    for b in range(bt):
        for c in range(nslab):
            ch = b * nslab + c
            lanes = slice(c * LANE, (c + 1) * LANE)
            row_of = lambda v, s: v[s:s + 1, :]
            if has_h0:
                enter_f, enter_b = h0_ref[b, 0:1, lanes], h0_ref[b, 1:2, lanes]
            else:
                enter_f = enter_b = jnp.zeros((1, LANE), F32)
            ent_f, ent_b = [None] * SUBLANE, [None] * SUBLANE
            for s in range(SUBLANE):
                ent_f[s] = enter_f
                enter_f = row_of(h_f[ch], s) + row_of(p_f[ch], s) * enter_f
            for s in reversed(range(SUBLANE)):
                ent_b[s] = enter_b
                enter_b = row_of(h_b[ch], s) + row_of(p_b[ch], s) * enter_b
            if emit_last:
                hl_ref[b, 0:1, lanes] = enter_f
                hl_ref[b, 1:2, lanes] = enter_b
            for s in range(SUBLANE):
                src = slice(s * pitch, s * pitch + seg)
                fwd = b_refs[0][ch, src, :] + a_refs[0][ch, src, :] * ent_f[s]
                bwd = b_refs[1][ch, src, :] + a_refs[1][ch, src, :] * ent_b[s]
                hs_ref[b * seq + s * seg:b * seq + (s + 1) * seg, lanes] = fwd + bwd


def _rglru(layer, xr, row_start, nseq, seq, conv_w, conv_b, w_gate, b_gate, lam, h0=None, emit_last=False):
    bt = 4
    dg = RNN_GROUP
    ng = D_RNN // dg
    rows = bt * seq
    assert nseq % bt == 0 and row_start % rows == 0
    off = row_start // rows
    in_specs = [
        pl.BlockSpec((rows, dg), lambda bi, g: (off + bi, g)),
        pl.BlockSpec((None, CONV_W, dg), lambda bi, g: (layer, 0, g)),
        pl.BlockSpec((None, 1, dg), lambda bi, g: (layer, 0, g)),
        pl.BlockSpec((None, None, dg, 4 * dg), lambda bi, g: (layer, g, 0, 0)),
        pl.BlockSpec((None, None, 1, 4 * dg), lambda bi, g: (layer, g, 0, 0)),
        pl.BlockSpec((None, 2, dg), lambda bi, g: (layer, 0, g)),
    ]
    args = [xr, conv_w, conv_b, w_gate, b_gate, lam]
    if h0 is not None:
        in_specs.append(pl.BlockSpec((bt, None, 2, dg), lambda bi, g: (bi, layer, 0, g)))
        args.append(h0)
    out_specs = [pl.BlockSpec((rows, dg), lambda bi, g: (bi, g))]
    out_shape = [jax.ShapeDtypeStruct((nseq * seq, D_RNN), F32)]
    if emit_last:
        out_specs.append(pl.BlockSpec((bt, 2, dg), lambda bi, g: (bi, 0, g)))
        out_shape.append(jax.ShapeDtypeStruct((nseq, 2, D_RNN), F32))
    return pl.pallas_call(
        functools.partial(_rglru_kernel, seq=seq, bt=bt, has_h0=h0 is not None, emit_last=emit_last),
        grid=(nseq // bt, ng),
        in_specs=in_specs,
        out_specs=out_specs,
        out_shape=out_shape,
        scratch_shapes=[pltpu.VMEM((bt * (dg // LANE), SUBLANE * _segment_pitch(seq // SUBLANE), LANE), F32)
                        for _ in range(4)],
        compiler_params=_params("arbitrary", "arbitrary"),
        name=f"rglru_{seq}",
    )(*args)


def _attn_kernel(*refs, past, tq, rope):
    it = iter(refs)
    cq_ref, ckv_ref, kr_ref = (next(it) for _ in range(3))
    if rope:
        krr_ref, pckv_ref, pkr_ref, cos_ref, sin_ref = (next(it) for _ in range(5))
    wq_ref = next(it)
    wqr_ref = next(it) if rope else None
    wk_ref, wv_ref = next(it), next(it)
    o_ref = next(it)
    k_scr, v_scr = next(it), next(it)
    qi = pl.program_id(1)
    scale = 1.0 / math.sqrt(QK_NOPE + QK_ROPE)

    def fill(rows, ckv, k_rope):
        c = ckv.astype(BF16)
        k = _dot(c, wk_ref[...]) + jnp.tile(k_rope, (1, N_HEADS))
        k_scr[rows, :] = k.astype(BF16)
        v_scr[rows, :] = _dot(c, wv_ref[...]).astype(BF16)

    @pl.when(qi == 0)
    def _():
        seq = ckv_ref.shape[0]
        if rope:
            fill(pl.ds(0, past), pckv_ref[...], pkr_ref[...])
            fill(pl.ds(past, seq), ckv_ref[...],
                 kr_ref[...] * cos_ref[...] + krr_ref[...] * sin_ref[...])
        else:
            fill(pl.ds(0, seq), ckv_ref[...], kr_ref[...])

    cq = cq_ref[...]
    qa = _dot(cq, wq_ref[...])
    if rope:
        qb = _dot(cq, wqr_ref[...])
        q0 = pl.multiple_of(qi * tq, tq)
        cos_t = cos_ref[pl.ds(q0, tq), :]
        sin_t = sin_ref[pl.ds(q0, tq), :]
    for j in range(N_HEADS // 2):
        pair = None
        for h in (2 * j, 2 * j + 1):
            cols = slice(h * HEAD_PAD, (h + 1) * HEAD_PAD)
            qh = qa[:, cols]
            if rope:
                qh = qh * cos_t + qb[:, cols] * sin_t
            s = lax.dot_general(qh.astype(BF16), k_scr[:, cols], (((1,), (1,)), ((), ())),
                                preferred_element_type=F32)
            e = jnp.exp((s - jnp.max(s, axis=-1, keepdims=True)) * scale)
            denom = jnp.sum(e, axis=-1, keepdims=True)
            o = _dot(e.astype(BF16), v_scr[:, cols]) * (1.0 / denom)
            pair = o if pair is None else pair + o
        o_ref[:, j * LANE:(j + 1) * LANE] = pair.astype(BF16)


def _attention(layer, cq, ckv, kr, row_start, nseq, seq, w_q, w_k, w_v, lat=None):
    rope = lat is not None
    tq = min(_ATT_Q_TILE, seq)
    assert seq % tq == 0 and row_start % seq == 0
    past = lat[1].shape[2] if rope else 0
    total = past + seq
    hp = N_HEADS * HEAD_PAD
    nq = seq // tq
    s_off, q_off = row_start // seq, row_start // tq
    full = lambda w: pl.BlockSpec((seq, w), lambda b, q: (s_off + b, 0))
    q_rows = lambda w: pl.BlockSpec((tq, w), lambda b, q: (q_off + b * nq + q, 0))
    in_specs = [q_rows(Q_LORA), full(KV_LORA), full(HEAD_PAD)]
    args = [cq, ckv, kr]
    if rope:
        krr, pckv, pkr, cos, sin, w_q_rot = lat
        in_specs += [full(HEAD_PAD),
                     pl.BlockSpec((None, None, past, KV_LORA), lambda b, q: (b, layer, 0, 0)),
                     pl.BlockSpec((None, None, past, HEAD_PAD), lambda b, q: (b, layer, 0, 0)),
                     _fixed_spec((seq, HEAD_PAD)), _fixed_spec((seq, HEAD_PAD))]
        args += [krr, pckv, pkr, cos, sin]
    in_specs.append(_fixed_spec((Q_LORA, hp), (layer,)))
    args.append(w_q)
    if rope:
        in_specs.append(_fixed_spec((Q_LORA, hp), (layer,)))
        args.append(w_q_rot)
    in_specs += [_fixed_spec((KV_LORA, hp), (layer,)), _fixed_spec((KV_LORA, hp), (layer,))]
    args += [w_k, w_v]
    return pl.pallas_call(
        functools.partial(_attn_kernel, past=past, tq=tq, rope=rope),
        grid=(nseq, nq),
        in_specs=in_specs,
        out_specs=pl.BlockSpec((tq, N_HEADS * V_HEAD), lambda b, q: (b * nq + q, 0)),
        out_shape=jax.ShapeDtypeStruct((nseq * seq, N_HEADS * V_HEAD), BF16),
        scratch_shapes=[pltpu.VMEM((total, hp), BF16), pltpu.VMEM((total, hp), BF16)],
        compiler_params=_params("arbitrary", "arbitrary"),
        name=f"attn_{seq}",
    )(*args)


def _mix_kernel(x_ref, hsc_ref, hsl_ref, atc_ref, atl_ref, m_ref, wygg_ref, wro_ref, wao_ref, wo_ref,
                g_ref, b_ref, o_ref, wro_bf, wao_bf, wo_bf, *, n_ctx_tiles):
    _cast_once([(wro_ref, wro_bf), (wao_ref, wao_bf), (wo_ref, wo_bf)])
    is_ctx = pl.program_id(0) < n_ctx_tiles
    hs = jnp.where(is_ctx, hsc_ref[...], hsl_ref[...])
    at = jnp.where(is_ctx, atc_ref[...], atl_ref[...])
    x = x_ref[...]
    h = (x * (1.0 + m_ref[1:2, :]) + m_ref[0:1, :]).astype(BF16)
    ygg = _dot(h, wygg_ref[...])
    u = (hs * jax.nn.gelu(ygg[:, :D_RNN])).astype(BF16)
    rnn_out = _dot(u, wro_bf[...])
    att_out = _dot(at, wao_bf[...])
    merged = (jax.nn.sigmoid(ygg[:, D_RNN:D_RNN + D_MODEL]) * rnn_out
              + jax.nn.sigmoid(ygg[:, D_RNN + D_MODEL:]) * att_out)
    m = _dot(merged.astype(BF16), wo_bf[...])
    o_ref[...] = _layer_norm(ALPHA * x + m_ref[2:3, :] * m, g_ref[...], b_ref[...])


def _mix(layer, x, hs_ctx, hs_lat, at_ctx, at_lat, mods, w_ygg, w_ro, w_ao, w_o, ln_g, ln_b, n_ctx_rows,
         rows_per_latent):
    n_tok = x.shape[0]
    tm = _ROW_TILE
    nct = n_ctx_rows // tm
    row = pl.BlockSpec((tm, D_MODEL), lambda i: (i, 0))
    ctx_row = pl.BlockSpec((tm, D_MODEL), lambda i: (jnp.minimum(i, nct - 1), 0))
    lat_row = pl.BlockSpec((tm, D_MODEL), lambda i: (jnp.maximum(i - nct, 0), 0))
    sq = (D_MODEL, D_MODEL)
    return pl.pallas_call(
        functools.partial(_mix_kernel, n_ctx_tiles=nct),
        grid=(n_tok // tm,),
        in_specs=[row, ctx_row, lat_row, ctx_row, lat_row,
                  _mod_spec(layer, tm, n_ctx_rows, rows_per_latent),
                  _fixed_spec((D_MODEL, D_RNN + 2 * D_MODEL), (layer,)),
                  _fixed_spec(sq, (layer,)), _fixed_spec(sq, (layer,)), _fixed_spec(sq, (layer,)),
                  _fixed_spec((1, D_MODEL), (layer,)), _fixed_spec((1, D_MODEL), (layer,))],
        out_specs=row,
        out_shape=jax.ShapeDtypeStruct((n_tok, D_MODEL), F32),
        scratch_shapes=[pltpu.VMEM(sq, BF16) for _ in range(3)],
        compiler_params=_params("arbitrary"),
        name="mix_out",
    )(x, hs_ctx, hs_lat, at_ctx, at_lat, mods, w_ygg, w_ro, w_ao, w_o, ln_g, ln_b)


def _ffn_kernel(x_ref, m_ref, w1_ref, w3_ref, w2_ref, g_ref, b_ref, o_ref):
    x = x_ref[...]
    h = (x * (1.0 + m_ref[4:5, :]) + m_ref[3:4, :]).astype(BF16)
    f = None
    for c in range(D_FF // _FF_CHUNK):
        cols = slice(c * _FF_CHUNK, (c + 1) * _FF_CHUNK)
        act = (jax.nn.silu(_dot(h, w1_ref[:, cols])) * _dot(h, w3_ref[:, cols])).astype(BF16)
        y = _dot(act, w2_ref[cols, :])
        f = y if f is None else f + y
    o_ref[...] = _layer_norm(ALPHA * x + m_ref[5:6, :] * f, g_ref[...], b_ref[...])


def _ffn_dense(layer, x, mods, w1, w3, w2, ln_g, ln_b, n_ctx_rows, rows_per_latent):
    n_tok = x.shape[0]
    tm = _ROW_TILE
    j = layer // 2
    row = pl.BlockSpec((tm, D_MODEL), lambda i: (i, 0))
    return pl.pallas_call(
        _ffn_kernel,
        grid=(n_tok // tm,),
        in_specs=[row, _mod_spec(layer, tm, n_ctx_rows, rows_per_latent),
                  _fixed_spec((D_MODEL, D_FF), (j,)), _fixed_spec((D_MODEL, D_FF), (j,)),
                  _fixed_spec((D_FF, D_MODEL), (j,)),
                  _fixed_spec((1, D_MODEL), (layer,)), _fixed_spec((1, D_MODEL), (layer,))],
        out_specs=row,
        out_shape=jax.ShapeDtypeStruct((n_tok, D_MODEL), F32),
        compiler_params=_params("arbitrary"),
        name="ffn_dense",
    )(x, mods, w1, w3, w2, ln_g, ln_b)


def _split_bf16(a):
    hi = a.astype(BF16)
    return hi, (a - hi.astype(F32)).astype(BF16)


def _route_kernel(x_ref, m_ref, wr_ref, br_ref, h_ref, ids_ref, gw_ref, cnt_ref, run_scr, tri_scr):
    tm = x_ref.shape[0]

    @pl.when(pl.program_id(0) == 0)
    def _():
        run_scr[...] = jnp.zeros_like(run_scr)
        r = lax.broadcasted_iota(jnp.int32, (tm, tm), 0)
        c = lax.broadcasted_iota(jnp.int32, (tm, tm), 1)
        tri_scr[...] = jnp.where(c < r, 1.0, 0.0).astype(BF16)

    h = x_ref[...] * (1.0 + m_ref[4:5, :]) + m_ref[3:4, :]
    h_ref[...] = h
    h_hi, h_lo = _split_bf16(h)

    w_hi, w_lo = _split_bf16(wr_ref[...])
    logits = _dot(h_hi, w_hi) + (_dot(h_hi, w_lo) + _dot(h_lo, w_hi)) + br_ref[...]
    lane = lax.broadcasted_iota(jnp.int32, logits.shape, 1)
    big = jnp.int32(N_EXPERTS)
    v1 = jnp.max(logits, axis=-1, keepdims=True)
    i1 = jnp.min(jnp.where(logits == v1, lane, big), axis=-1, keepdims=True)
    rest = jnp.where(lane == i1, -jnp.inf, logits)
    v2 = jnp.max(rest, axis=-1, keepdims=True)
    i2 = jnp.min(jnp.where(rest == v2, lane, big), axis=-1, keepdims=True)
    e2 = jnp.exp(v2 - v1)
    g1 = 1.0 / (1.0 + e2)
    g2 = e2 / (1.0 + e2)

    m1, m2 = lane == i1, lane == i2
    member = jnp.where(m1, 1.0, 0.0) + jnp.where(m2, 1.0, 0.0)
    before = _dot(tri_scr[...], member.astype(BF16)) + run_scr[...]
    r1 = jnp.sum(jnp.where(m1, before, 0.0), axis=-1, keepdims=True).astype(jnp.int32)
    r2 = jnp.sum(jnp.where(m2, before, 0.0), axis=-1, keepdims=True).astype(jnp.int32)
    run_scr[...] += jnp.sum(member, axis=0, keepdims=True)

    ids_ref[...] = jnp.where(lane == 0, i1, jnp.where(lane == 1, i2, jnp.where(lane == 2, r1, r2)))
    gw_ref[...] = jnp.where(lane == 0, g1, g2)
    cnt_ref[...] = run_scr[...].astype(jnp.int32)


def _moe_route(layer, x, mods, w_router, b_router, n_ctx_rows, rows_per_latent):
    n_tok = x.shape[0]
    tm = _ROW_TILE
    j = layer // 2
    row = lambda w: pl.BlockSpec((tm, w), lambda i: (i, 0))
    return pl.pallas_call(
        _route_kernel,
        grid=(n_tok // tm,),
        in_specs=[row(D_MODEL), _mod_spec(layer, tm, n_ctx_rows, rows_per_latent),
                  _fixed_spec((D_MODEL, N_EXPERTS), (j,)), _fixed_spec((1, N_EXPERTS), (j,))],
        out_specs=[row(D_MODEL), row(N_EXPERTS), row(N_EXPERTS),
                   pl.BlockSpec((1, N_EXPERTS), lambda i: (0, 0))],
        out_shape=[jax.ShapeDtypeStruct((n_tok, D_MODEL), F32),
                   jax.ShapeDtypeStruct((n_tok, N_EXPERTS), jnp.int32),
                   jax.ShapeDtypeStruct((n_tok, N_EXPERTS), F32),
                   jax.ShapeDtypeStruct((1, N_EXPERTS), jnp.int32)],
        scratch_shapes=[pltpu.VMEM((1, N_EXPERTS), F32), pltpu.VMEM((tm, tm), BF16)],
        compiler_params=_params("arbitrary"),
        name="moe_route",
    )(x, mods, w_router, b_router)


def _slot_tables(ids, gw, counts, n_tok):
    tm = _MOE_SLOT_TILE
    n_slots = TOP_K * n_tok + N_EXPERTS * tm
    cnt = counts[0]
    padded = ((cnt + tm - 1) // tm) * tm
    ends = jnp.cumsum(padded)
    starts = ends - padded
    dest = (jnp.take(starts, ids[:, :TOP_K]) + ids[:, TOP_K:2 * TOP_K]).reshape(-1)
    spare = TOP_K * n_tok
    slot_src = jnp.full((n_slots,), spare, jnp.int32).at[dest].set(jnp.arange(spare, dtype=jnp.int32))
    slot_tok = slot_src // TOP_K
    gates = jnp.concatenate([gw[:, :TOP_K].reshape(-1), jnp.zeros((1,), F32)])
    slot_gate = jnp.take(gates, slot_src)
    tile_start = jnp.arange(n_slots // tm, dtype=jnp.int32) * tm
    tile_exp = jnp.sum((tile_start[:, None] >= ends[None, :]).astype(jnp.int32), axis=1)
    tile_exp = jnp.minimum(tile_exp, N_EXPERTS - 1)
    tile_on = (tile_start < ends[-1]).astype(jnp.int32)
    return tile_exp, tile_on, slot_tok, slot_gate.reshape(n_slots, 1)


def _moe_up_kernel(exp_ref, on_ref, tok_ref, h_ref, w1_ref, w3_ref, act_ref, xg0_scr, xg1_scr, *,
                   n_tok, n_tiles):
    t = pl.program_id(0)
    tm = xg0_scr.shape[0]

    def gather_row(tile, r, dst):
        src = jnp.minimum(tok_ref[tile * tm + r], n_tok - 1)
        dst[pl.ds(r, 1), :] = h_ref[pl.ds(src, 1), :]

    @pl.when(t == 0)
    def _():
        def body(r, carry):
            gather_row(0, r, xg0_scr)
            return carry

        lax.fori_loop(0, tm, body, 0, unroll=8)

    for parity, (cur, nxt) in enumerate(((xg0_scr, xg1_scr), (xg1_scr, xg0_scr))):
        @pl.when(jnp.logical_and(on_ref[t] == 1, t % 2 == parity))
        def _():
            h = cur[...].astype(BF16)
            nxt_tile = jnp.minimum(t + 1, n_tiles - 1)
            for r in range(tm):
                gather_row(nxt_tile, r, nxt)
            act_ref[...] = (jax.nn.silu(_dot(h, w1_ref[...])) * _dot(h, w3_ref[...])).astype(BF16)

    @pl.when(on_ref[t] == 0)
    def _():
        act_ref[...] = jnp.zeros_like(act_ref)


def _moe_up(layer, tables, h, w1, w3):
    tile_exp, tile_on, slot_tok, _ = tables
    n_tok = h.shape[0]
    tm = _MOE_SLOT_TILE
    n_tiles = tile_exp.shape[0]
    j = layer // 2
    wspec = pl.BlockSpec((None, None, D_MODEL, D_EXPERT), lambda t, ex, on, tk: (j, ex[t], 0, 0),
                         pipeline_mode=pl.Buffered(1))
    return pl.pallas_call(
        functools.partial(_moe_up_kernel, n_tok=n_tok, n_tiles=n_tiles),
        grid_spec=pltpu.PrefetchScalarGridSpec(
            num_scalar_prefetch=3,
            grid=(n_tiles,),
            in_specs=[pl.BlockSpec((n_tok, D_MODEL), lambda t, ex, on, tk: (0, 0),
                                   pipeline_mode=pl.Buffered(1)),
                      wspec, wspec],
            out_specs=pl.BlockSpec((tm, D_EXPERT), lambda t, ex, on, tk: (t, 0)),
            scratch_shapes=[pltpu.VMEM((tm, D_MODEL), F32), pltpu.VMEM((tm, D_MODEL), F32)],
        ),
        out_shape=jax.ShapeDtypeStruct((n_tiles * tm, D_EXPERT), BF16),
        compiler_params=_params("arbitrary"),
        name="moe_up",
    )(tile_exp, tile_on, slot_tok, h, w1, w3)


def _moe_down_kernel(exp_ref, on_ref, tok_ref, act_ref, gate_ref, w2_ref, x_ref, m_ref, g_ref, b_ref,
                     o_ref, acc_scr, y0_scr, y1_scr, *, n_tiles, ln_rows):
    t = pl.program_id(0)
    tm = y0_scr.shape[0]

    @pl.when(t == 0)
    def _():
        acc_scr[...] = jnp.zeros_like(acc_scr)

    def project(dst):
        dst[...] = _dot(act_ref[...], w2_ref[...]) * gate_ref[...]

    def scatter_group(tile, base, ys):
        toks = [tok_ref[tile * tm + base + u] for u in range(SUBLANE)]
        rows = [acc_scr[pl.ds(toks[u], 1), :] for u in range(SUBLANE)]
        for u in range(SUBLANE):
            acc_scr[pl.ds(toks[u], 1), :] = rows[u] + ys[u:u + 1, :]

    last = n_tiles - 1
    project_now = jnp.logical_and(t < n_tiles, on_ref[jnp.minimum(t, last)] == 1)
    add_prev = jnp.logical_and(jnp.logical_and(t >= 1, t <= n_tiles), on_ref[jnp.clip(t - 1, 0, last)] == 1)
    for parity, (cur, prv) in enumerate(((y0_scr, y1_scr), (y1_scr, y0_scr))):
        mine = t % 2 == parity

        @pl.when(jnp.logical_and(mine, jnp.logical_and(project_now, add_prev)))
        def _():
            project(cur)
            for g in range(tm // SUBLANE):
                scatter_group(t - 1, g * SUBLANE, prv[g * SUBLANE:(g + 1) * SUBLANE, :])

        @pl.when(jnp.logical_and(mine, jnp.logical_and(project_now, jnp.logical_not(add_prev))))
        def _():
            project(cur)

        @pl.when(jnp.logical_and(mine, jnp.logical_and(add_prev, jnp.logical_not(project_now))))
        def _():
            def body(g, carry):
                base = pl.multiple_of(g * SUBLANE, SUBLANE)
                scatter_group(t - 1, base, prv[pl.ds(base, SUBLANE), :])
                return carry

            lax.fori_loop(0, tm // SUBLANE, body, 0)

    @pl.when(t > n_tiles)
    def _():
        r0 = pl.multiple_of((t - n_tiles - 1) * ln_rows, ln_rows)
        f = acc_scr[pl.ds(r0, ln_rows), :]
        o_ref[...] = _layer_norm(ALPHA * x_ref[...] + m_ref[5:6, :] * f, g_ref[...], b_ref[...])


def _moe_down(layer, tables, act, x, mods, w2, ln_g, ln_b, n_ctx_rows, rows_per_latent):
    tile_exp, tile_on, slot_tok, slot_gate = tables
    n_tok = x.shape[0]
    tm = _MOE_SLOT_TILE
    tl = _MOE_LN_TILE
    n_tiles = tile_exp.shape[0]
    j = layer // 2
    slot = lambda t: jnp.minimum(t, n_tiles - 1)
    ln = lambda t: jnp.maximum(t - n_tiles - 1, 0)
    n_ctx_tiles, per = n_ctx_rows // tl, rows_per_latent // tl

    def mod_index(t, *_):
        i = ln(t)
        return (layer, jnp.where(i < n_ctx_tiles, 0, 1 + (i - n_ctx_tiles) // per), 0, 0)

    fixed = lambda shape, lead: pl.BlockSpec((None,) + shape, lambda *_: (lead,) + (0,) * len(shape),
                                             pipeline_mode=pl.Buffered(1))
    return pl.pallas_call(
        functools.partial(_moe_down_kernel, n_tiles=n_tiles, ln_rows=tl),
        grid_spec=pltpu.PrefetchScalarGridSpec(
            num_scalar_prefetch=3,
            grid=(n_tiles + 1 + n_tok // tl,),
            in_specs=[pl.BlockSpec((tm, D_EXPERT), lambda t, ex, on, tk: (slot(t), 0)),
                      pl.BlockSpec((tm, 1), lambda t, ex, on, tk: (slot(t), 0)),
                      pl.BlockSpec((None, None, D_EXPERT, D_MODEL), lambda t, ex, on, tk: (j, ex[slot(t)], 0, 0)),
                      pl.BlockSpec((tl, D_MODEL), lambda t, ex, on, tk: (ln(t), 0)),
                      pl.BlockSpec((None, None, 6, D_MODEL), mod_index),
                      fixed((1, D_MODEL), layer), fixed((1, D_MODEL), layer)],
            out_specs=pl.BlockSpec((tl, D_MODEL), lambda t, ex, on, tk: (ln(t), 0)),
            scratch_shapes=[pltpu.VMEM((n_tok + SUBLANE, D_MODEL), F32), pltpu.VMEM((tm, D_MODEL), F32),
                            pltpu.VMEM((tm, D_MODEL), F32)],
        ),
        out_shape=jax.ShapeDtypeStruct((n_tok, D_MODEL), F32),
        compiler_params=_params("arbitrary"),
        name="moe_down",
    )(tile_exp, tile_on, slot_tok, act, slot_gate, w2, x, mods, ln_g, ln_b)


def _ffn_moe(layer, x, mods, w_router, b_router, w1, w3, w2, ln_g, ln_b, n_ctx_rows, rows_per_latent):
    h, ids, gw, counts = _moe_route(layer, x, mods, w_router, b_router, n_ctx_rows, rows_per_latent)
    tables = _slot_tables(ids, gw, counts, x.shape[0])
    act = _moe_up(layer, tables, h, w1, w3)
    return _moe_down(layer, tables, act, x, mods, w2, ln_g, ln_b, n_ctx_rows, rows_per_latent)


def _rot_cols(w):
    ws = w.reshape(w.shape[:-1] + (2, 2, ROPE_AXIS // 2))
    return jnp.stack([-ws[..., 1, :], ws[..., 0, :]], axis=-2).reshape(w.shape)


def _pad_rope_cols(w):
    return jnp.pad(w, ((0, 0),) * (w.ndim - 1) + ((ROPE_OFF, HEAD_PAD - ROPE_OFF - QK_ROPE),))


def _rope_tables(rows):
    r = jnp.repeat(jnp.arange(rows, dtype=F32), GRID_W)
    col = jnp.tile(jnp.arange(GRID_W, dtype=F32), rows)
    freqs = ROPE_THETA ** (-jnp.arange(0, ROPE_AXIS, 2, dtype=F32) / ROPE_AXIS)
    ar = r[:, None] * freqs
    ac = col[:, None] * freqs
    ang = jnp.concatenate([ar, ar, ac, ac], axis=-1)
    cos, sin = jnp.cos(ang), jnp.sin(ang)
    n = cos.shape[0]
    cos_p = jnp.concatenate([jnp.ones((n, ROPE_OFF), F32), cos,
                             jnp.zeros((n, HEAD_PAD - ROPE_OFF - QK_ROPE), F32)], axis=-1)
    return cos_p, _pad_rope_cols(sin)


def _prepared_weights(w_in, rg_wa, rg_ba, rg_wx, rg_bx, w_qb, w_kvb):
    bf = lambda a: a.astype(BF16)
    w_kr = w_in[:, :, _COL_KR:_COL_KR + QK_ROPE]
    w_s = jnp.concatenate([w_in[:, :, _COL_Q:_COL_KR], _pad_rope_cols(w_kr),
                           _pad_rope_cols(_rot_cols(w_kr))], axis=-1)
    w_x = w_in[:, :, :D_RNN]
    w_ygg = jnp.concatenate([w_in[:, :, D_RNN:2 * D_RNN], w_in[:, :, _COL_GATE:]], axis=-1)

    wq = w_qb.reshape(DEPTH, Q_LORA, N_HEADS, QK_NOPE + QK_ROPE)
    q_nope, q_rope = wq[..., :QK_NOPE], wq[..., QK_NOPE:]
    tail = jnp.zeros((DEPTH, Q_LORA, N_HEADS, HEAD_PAD - QK_NOPE - QK_ROPE), F32)
    hp = N_HEADS * HEAD_PAD
    w_q = jnp.concatenate([q_nope, q_rope, tail], axis=-1).reshape(DEPTH, Q_LORA, hp)
    w_q_rot = jnp.concatenate([jnp.zeros_like(q_nope), _rot_cols(q_rope), tail], axis=-1)
    w_q_rot = w_q_rot.reshape(DEPTH, Q_LORA, hp)
    wkv = w_kvb.reshape(DEPTH, KV_LORA, N_HEADS, QK_NOPE + V_HEAD)
    k_nope, v = wkv[..., :QK_NOPE], wkv[..., QK_NOPE:]
    w_k = jnp.concatenate([k_nope, jnp.zeros((DEPTH, KV_LORA, N_HEADS, HEAD_PAD - QK_NOPE), F32)], axis=-1)
    w_k = w_k.reshape(DEPTH, KV_LORA, hp)
    vz = jnp.zeros_like(v)
    odd = (jnp.arange(N_HEADS) % 2 == 1)[None, None, :, None]
    w_v = jnp.where(odd, jnp.concatenate([vz, v], axis=-1), jnp.concatenate([v, vz], axis=-1))
    w_v = w_v.reshape(DEPTH, KV_LORA, hp)

    nblk = RNN_GROUP // RNN_BW
    ng = D_RNN // RNN_GROUP
    eye = jnp.eye(nblk, dtype=F32)

    def block_diag(w):
        wg = w.reshape(DEPTH, ng, nblk, RNN_BW, RNN_BW)
        return jnp.einsum('lgnjk,nm->lgnjmk', wg, eye).reshape(DEPTH, ng, RNN_GROUP, RNN_GROUP)

    w_gate = jnp.concatenate([block_diag(rg_wa[:, 0]), block_diag(rg_wx[:, 0]),
                              block_diag(rg_wa[:, 1]), block_diag(rg_wx[:, 1])], axis=-1)
    grp = lambda b: b.reshape(DEPTH, ng, 1, RNN_GROUP)
    b_gate = jnp.concatenate([grp(rg_ba[:, 0]), grp(rg_bx[:, 0]), grp(rg_ba[:, 1]), grp(rg_bx[:, 1])],
                             axis=-1)
    return dict(w_x=bf(w_x), w_s=bf(w_s), w_ygg=bf(w_ygg), w_q=bf(w_q), w_q_rot=bf(w_q_rot), w_k=bf(w_k), w_v=bf(w_v),
                w_gate=bf(w_gate), b_gate=b_gate)


def kernel(x_prompt, x_sample, cache_ckv, cache_krope, state_rnn, c, c_ctx, w_ada, b_ada, w_in, conv_w, conv_b, rg_wa, rg_ba, rg_wx, rg_bx, rg_lambda, w_rnn_out, q_norm_g, w_qb, kv_norm_g, w_kvb, w_attn_out, w_out, ln1_g, ln1_b, ln2_g, ln2_b, w1_dense, w3_dense, w2_dense, w_router, b_router, w1_exp, w3_exp, w2_exp):
    nb, seq, _ = x_prompt.shape
    db, dseq, _ = x_sample.shape
    n_ctx = nb * seq
    n_lat = db * dseq
    assert n_ctx % _ROW_TILE == 0 and dseq % _ROW_TILE == 0 and db + 1 <= 8
    bf = lambda a: a.astype(BF16)
    vec = lambda a: a.reshape(a.shape[0], 1, a.shape[-1])

    cond8 = jnp.concatenate([c_ctx[None, :], c, jnp.zeros((8 - 1 - db, D_MODEL), F32)], axis=0)
    mods = _ada_mods(cond8, w_ada, b_ada)
    cos_p, sin_p = _rope_tables(dseq // GRID_W)
    past_kr = _pad_rope_cols(cache_krope)
    pw = _prepared_weights(w_in, rg_wa, rg_ba, rg_wx, rg_bx, w_qb, w_kvb)
    w1d, w3d, w2d = bf(w1_dense), bf(w3_dense), bf(w2_dense)
    w1e, w3e, w2e = bf(w1_exp), bf(w3_exp), bf(w2_exp)
    qg, kvg, cb = vec(q_norm_g), vec(kv_norm_g), vec(conv_b)
    l1g, l1b, l2g, l2b = vec(ln1_g), vec(ln1_b), vec(ln2_g), vec(ln2_b)
    br = vec(b_router)
    tiles = (n_ctx, dseq)

    x = jnp.concatenate([x_prompt.reshape(n_ctx, D_MODEL), x_sample.reshape(n_lat, D_MODEL)], axis=0)
    new_ckv, new_kr, new_h = [], [], []
    for l in range(DEPTH):
        xr, cq, ckv, kr, krr = _in_proj(l, x, mods, pw['w_x'], pw['w_s'], qg, kvg, *tiles)
        new_ckv.append(ckv[:n_ctx].reshape(nb, seq, KV_LORA))
        new_kr.append(kr[:n_ctx, ROPE_OFF:ROPE_OFF + QK_ROPE].reshape(nb, seq, QK_ROPE))

        rnn_args = (conv_w, cb, pw['w_gate'], pw['b_gate'], rg_lambda)
        hs_ctx, h_last = _rglru(l, xr, 0, nb, seq, *rnn_args, emit_last=True)
        (hs_lat,) = _rglru(l, xr, n_ctx, db, dseq, *rnn_args, h0=state_rnn)
        new_h.append(h_last)

        att_w = (pw['w_q'], pw['w_k'], pw['w_v'])
        at_ctx = _attention(l, cq, ckv, kr, 0, nb, seq, *att_w)
        at_lat = _attention(l, cq, ckv, kr, n_ctx, db, dseq, *att_w,
                            lat=(krr, cache_ckv, past_kr, cos_p, sin_p, pw['w_q_rot']))

        x = _mix(l, x, hs_ctx, hs_lat, at_ctx, at_lat, mods, pw['w_ygg'], w_rnn_out, w_attn_out, w_out,
                 l1g, l1b, *tiles)
        if l % 2 == 0:
            x = _ffn_dense(l, x, mods, w1d, w3d, w2d, l2g, l2b, *tiles)
        else:
            x = _ffn_moe(l, x, mods, w_router, br, w1e, w3e, w2e, l2g, l2b, *tiles)

    return (x[:n_ctx].reshape(nb, seq, D_MODEL), x[n_ctx:].reshape(db, dseq, D_MODEL),
            jnp.stack(new_ckv, axis=1), jnp.stack(new_kr, axis=1), jnp.stack(new_h, axis=1))
```

```python
import functools
import math

import jax
import jax.numpy as jnp
from jax import lax
from jax.experimental import pallas as pl
from jax.experimental.pallas import tpu as pltpu

F32 = jnp.float32
BF16 = jnp.bfloat16

D_MODEL = 1024
DEPTH = 4
GRID_W = 64
D_RNN = D_MODEL
RNN_BLOCKS = 16
RNN_BW = D_RNN // RNN_BLOCKS
CONV_W = 4
RG_C = 8.0
N_HEADS = 16
QK_NOPE = 64
QK_ROPE = 32
V_HEAD = 64
Q_LORA = 256
KV_LORA = 128
ROPE_AXIS = QK_ROPE // 2
ROPE_THETA = 10000.0
D_FF = 2816
N_EXPERTS = 8
TOP_K = 2
D_EXPERT = 1408
ALPHA = (2 * DEPTH) ** 0.25
LN_EPS = 1e-5
RMS_EPS = 1e-6

LANE = 128
SUBLANE = 8
HEAD_PAD = LANE
ROPE_OFF = QK_NOPE
RNN_GROUP = 256
VMEM_LIMIT = 56 * 1024 * 1024

_ROW_TILE = 512
_MOE_SLOT_TILE = 512
_MOE_LN_TILE = 256
_ATT_Q_TILE = 512
_FF_CHUNK = 1408

_COL_Q = 2 * D_RNN
_COL_KR = _COL_Q + Q_LORA + KV_LORA
_COL_GATE = _COL_KR + QK_ROPE


def _dot(a, b):
    return jnp.dot(a, b, preferred_element_type=F32)


def _layer_norm(y, g, b):
    mu = jnp.mean(y, axis=-1, keepdims=True)
    d = y - mu
    var = jnp.mean(d * d, axis=-1, keepdims=True)
    return d * lax.rsqrt(var + LN_EPS) * g + b


def _rms_norm(y, g):
    return y * lax.rsqrt(jnp.mean(y * y, axis=-1, keepdims=True) + RMS_EPS) * g


def _params(*sem):
    return pltpu.CompilerParams(dimension_semantics=sem, vmem_limit_bytes=VMEM_LIMIT)


def _fixed_spec(shape, lead=(), tail=None):
    tail = (0,) * len(shape) if tail is None else tuple(tail)
    index = tuple(lead) + tail
    return pl.BlockSpec((None,) * len(lead) + tuple(shape), lambda *_: index,
                        pipeline_mode=pl.Buffered(1))


def _mod_spec(layer, tm, n_ctx_rows, rows_per_latent):
    n_ctx_tiles = n_ctx_rows // tm
    per = rows_per_latent // tm

    def index(i, *_):
        return (layer, jnp.where(i < n_ctx_tiles, 0, 1 + (i - n_ctx_tiles) // per), 0, 0)

    return pl.BlockSpec((None, None, 6, D_MODEL), index)


def _cast_once(pairs):
    @pl.when(pl.program_id(0) == 0)
    def _():
        for src, dst in pairs:
            dst[...] = src[...].astype(BF16)


def _mods_kernel(c_ref, w_ref, b_ref, o_ref):
    s = jax.nn.silu(c_ref[...]).astype(BF16)
    o_ref[...] = _dot(s, w_ref[...].astype(BF16)) + b_ref[...]


def _ada_mods(cond8, w_ada, b_ada):
    out = pl.pallas_call(
        _mods_kernel,
        grid=(DEPTH, 6),
        in_specs=[
            pl.BlockSpec((8, D_MODEL), lambda l, j: (0, 0)),
            pl.BlockSpec((None, D_MODEL, D_MODEL), lambda l, j: (l, 0, j)),
            pl.BlockSpec((None, 1, D_MODEL), lambda l, j: (l, 0, j)),
        ],
        out_specs=pl.BlockSpec((None, 8, D_MODEL), lambda l, j: (l, 0, j)),
        out_shape=jax.ShapeDtypeStruct((DEPTH, 8, 6 * D_MODEL), F32),
        compiler_params=_params("arbitrary", "arbitrary"),
        name="ada_mods",
    )(cond8, w_ada, b_ada.reshape(DEPTH, 1, 6 * D_MODEL))
    return out.reshape(DEPTH, 8, 6, D_MODEL)


_S_COLS = Q_LORA + KV_LORA + 2 * HEAD_PAD


def _inproj_kernel(x_ref, m_ref, wx_ref, ws_ref, qg_ref, kvg_ref,
                   xr_ref, cq_ref, ckv_ref, kr_ref, krr_ref):
    h = (x_ref[...] * (1.0 + m_ref[1:2, :]) + m_ref[0:1, :]).astype(BF16)
    xr_ref[...] = _dot(h, wx_ref[...])
    s = _dot(h, ws_ref[...])
    cq_ref[...] = _rms_norm(s[:, :Q_LORA], qg_ref[...]).astype(BF16)
    ckv_ref[...] = _rms_norm(s[:, Q_LORA:Q_LORA + KV_LORA], kvg_ref[...])
    kr_ref[...] = s[:, Q_LORA + KV_LORA:Q_LORA + KV_LORA + HEAD_PAD]
    krr_ref[...] = s[:, Q_LORA + KV_LORA + HEAD_PAD:]


def _in_proj(layer, x, mods, w_x, w_s, q_g, kv_g, n_ctx_rows, rows_per_latent):
    n_tok = x.shape[0]
    tm = _ROW_TILE
    row = lambda w: pl.BlockSpec((tm, w), lambda i: (i, 0))
    return pl.pallas_call(
        _inproj_kernel,
        grid=(n_tok // tm,),
        in_specs=[
            row(D_MODEL),
            _mod_spec(layer, tm, n_ctx_rows, rows_per_latent),
            _fixed_spec((D_MODEL, D_RNN), (layer,)),
            _fixed_spec((D_MODEL, _S_COLS), (layer,)),
            _fixed_spec((1, Q_LORA), (layer,)),
            _fixed_spec((1, KV_LORA), (layer,)),
        ],
        out_specs=[row(D_RNN), row(Q_LORA), row(KV_LORA), row(HEAD_PAD), row(HEAD_PAD)],
        out_shape=[
            jax.ShapeDtypeStruct((n_tok, D_RNN), F32),
            jax.ShapeDtypeStruct((n_tok, Q_LORA), BF16),
            jax.ShapeDtypeStruct((n_tok, KV_LORA), F32),
            jax.ShapeDtypeStruct((n_tok, HEAD_PAD), F32),
            jax.ShapeDtypeStruct((n_tok, HEAD_PAD), F32),
        ],
        compiler_params=_params("arbitrary"),
        name="in_proj",
    )(x, mods, w_x, w_s, q_g, kv_g)


def _segment_pitch(seg):
    assert seg % SUBLANE == 0
    return seg if (seg // SUBLANE) % 2 == 1 else seg + SUBLANE


def _rglru_kernel(*refs, seq, bt, has_h0, emit_last):
    it = iter(refs)
    x_ref, cw_ref, cb_ref, wg_ref, bg_ref, lam_ref = (next(it) for _ in range(6))
    h0_ref = next(it) if has_h0 else None
    hs_ref = next(it)
    hl_ref = next(it) if emit_last else None
    a_refs = (next(it), next(it))
    b_refs = (next(it), next(it))
    dg = RNN_GROUP
    nslab = dg // LANE
    seg = seq // SUBLANE
    pitch = _segment_pitch(seg)

    row = lax.broadcasted_iota(jnp.int32, (seq, dg), 0)
    neg_softplus = [jax.nn.softplus(-lam_ref[d:d + 1, :]) for d in range(2)]
    for b in range(bt):
        rows = slice(b * seq, (b + 1) * seq)
        x = x_ref[rows, :]
        xm1 = jnp.where(row >= 1, pltpu.roll(x, 1, 0), 0.0)
        xp1 = jnp.where(row < seq - 1, pltpu.roll(x, seq - 1, 0), 0.0)
        xp2 = jnp.where(row < seq - 2, pltpu.roll(x, seq - 2, 0), 0.0)
        xc = cb_ref[...] + xm1 * cw_ref[0:1, :]
        xc = xc + x * cw_ref[1:2, :]
        xc = xc + xp1 * cw_ref[2:3, :]
        xc = xc + xp2 * cw_ref[3:4, :]
        gates = _dot(xc.astype(BF16), wg_ref[...]) + bg_ref[...]
        for d in range(2):
            r = jax.nn.sigmoid(gates[:, (2 * d) * dg:(2 * d + 1) * dg])
            i = jax.nn.sigmoid(gates[:, (2 * d + 1) * dg:(2 * d + 2) * dg])
            log_a = -RG_C * r * neg_softplus[d]
            a = jnp.exp(log_a)
            bq = jnp.sqrt(jnp.tanh(-log_a) * (a * a + 1.0)) * (i * xc)
            for s in range(SUBLANE):
                for c in range(nslab):
                    src = (slice(s * seg, (s + 1) * seg), slice(c * LANE, (c + 1) * LANE))
                    dst = slice(s * pitch, s * pitch + seg)
                    a_refs[d][b * nslab + c, dst, :] = a[src]
                    b_refs[d][b * nslab + c, dst, :] = bq[src]

    nch = bt * nslab
    zeros = tuple(jnp.zeros((SUBLANE, LANE), F32) for _ in range(nch))
    ones = tuple(jnp.ones((SUBLANE, LANE), F32) for _ in range(nch))

    def step(j, carry):
        out = []
        for d, t in ((0, j), (1, seg - 1 - j)):
            h_prev, p_prev = carry[2 * d], carry[2 * d + 1]
            h_new, p_new = [], []
            idx = pl.ds(t, SUBLANE, stride=pitch)
            for ch in range(nch):
                a = a_refs[d].at[ch][idx, :]
                h = a * h_prev[ch] + b_refs[d].at[ch][idx, :]
                p = a * p_prev[ch]
                b_refs[d].at[ch][idx, :] = h
                a_refs[d].at[ch][idx, :] = p
                h_new.append(h)
                p_new.append(p)
            out += [tuple(h_new), tuple(p_new)]
        return tuple(out)

    h_f, p_f, h_b, p_b = lax.fori_loop(0, seg, step, (zeros, ones, zeros, ones), unroll=2)

    for b in range(bt):
        for c in range(nslab):
            ch = b * nslab + c
            lanes = slice(c * LANE, (c + 1) * LANE)
            row_of = lambda v, s: v[s:s + 1, :]
            if has_h0:
                enter_f, enter_b = h0_ref[b, 0:1, lanes], h0_ref[b, 1:2, lanes]
            else:
                enter_f = enter_b = jnp.zeros((1, LANE), F32)
            ent_f, ent_b = [None] * SUBLANE, [None] * SUBLANE
            for s in range(SUBLANE):
                ent_f[s] = enter_f
                enter_f = row_of(h_f[ch], s) + row_of(p_f[ch], s) * enter_f
            for s in reversed(range(SUBLANE)):
                ent_b[s] = enter_b
                enter_b = row_of(h_b[ch], s) + row_of(p_b[ch], s) * enter_b
            if emit_last:
                hl_ref[b, 0:1, lanes] = enter_f
                hl_ref[b, 1:2, lanes] = enter_b
            for s in range(SUBLANE):
                src = slice(s * pitch, s * pitch + seg)
                fwd = b_refs[0][ch, src, :] + a_refs[0][ch, src, :] * ent_f[s]
                bwd = b_refs[1][ch, src, :] + a_refs[1][ch, src, :] * ent_b[s]
                hs_ref[b * seq + s * seg:b * seq + (s + 1) * seg, lanes] = fwd + bwd


def _rglru(layer, xr, row_start, nseq, seq, conv_w, conv_b, w_gate, b_gate, lam, h0=None, emit_last=False):
    bt = 4
    dg = RNN_GROUP
    ng = D_RNN // dg
    rows = bt * seq
    assert nseq % bt == 0 and row_start % rows == 0
    off = row_start // rows
    in_specs = [
        pl.BlockSpec((rows, dg), lambda bi, g: (off + bi, g)),
        pl.BlockSpec((None, CONV_W, dg), lambda bi, g: (layer, 0, g)),
        pl.BlockSpec((None, 1, dg), lambda bi, g: (layer, 0, g)),
        pl.BlockSpec((None, None, dg, 4 * dg), lambda bi, g: (layer, g, 0, 0)),
        pl.BlockSpec((None, None, 1, 4 * dg), lambda bi, g: (layer, g, 0, 0)),
        pl.BlockSpec((None, 2, dg), lambda bi, g: (layer, 0, g)),
    ]
    args = [xr, conv_w, conv_b, w_gate, b_gate, lam]
    if h0 is not None:
        in_specs.append(pl.BlockSpec((bt, None, 2, dg), lambda bi, g: (bi, layer, 0, g)))
        args.append(h0)
    out_specs = [pl.BlockSpec((rows, dg), lambda bi, g: (bi, g))]
    out_shape = [jax.ShapeDtypeStruct((nseq * seq, D_RNN), F32)]
    if emit_last:
        out_specs.append(pl.BlockSpec((bt, 2, dg), lambda bi, g: (bi, 0, g)))
        out_shape.append(jax.ShapeDtypeStruct((nseq, 2, D_RNN), F32))
    return pl.pallas_call(
        functools.partial(_rglru_kernel, seq=seq, bt=bt, has_h0=h0 is not None, emit_last=emit_last),
        grid=(nseq // bt, ng),
        in_specs=in_specs,
        out_specs=out_specs,
        out_shape=out_shape,
        scratch_shapes=[pltpu.VMEM((bt * (dg // LANE), SUBLANE * _segment_pitch(seq // SUBLANE), LANE), F32)
                        for _ in range(4)],
        compiler_params=_params("arbitrary", "arbitrary"),
        name=f"rglru_{seq}",
    )(*args)


def _attn_kernel(*refs, past, tq, rope):
    it = iter(refs)
    cq_ref, ckv_ref, kr_ref = (next(it) for _ in range(3))
    if rope:
        krr_ref, pckv_ref, pkr_ref, cos_ref, sin_ref = (next(it) for _ in range(5))
    wq_ref = next(it)
    wqr_ref = next(it) if rope else None
    wk_ref, wv_ref = next(it), next(it)
    o_ref = next(it)
    k_scr, v_scr = next(it), next(it)
    qi = pl.program_id(1)
    scale = 1.0 / math.sqrt(QK_NOPE + QK_ROPE)

    def fill(rows, ckv, k_rope):
        c = ckv.astype(BF16)
        k = _dot(c, wk_ref[...]) + jnp.tile(k_rope, (1, N_HEADS))
        k_scr[rows, :] = k.astype(BF16)
        v_scr[rows, :] = _dot(c, wv_ref[...]).astype(BF16)

    @pl.when(qi == 0)
    def _():
        seq = ckv_ref.shape[0]
        if rope:
            fill(pl.ds(0, past), pckv_ref[...], pkr_ref[...])
            fill(pl.ds(past, seq), ckv_ref[...],
                 kr_ref[...] * cos_ref[...] + krr_ref[...] * sin_ref[...])
        else:
            fill(pl.ds(0, seq), ckv_ref[...], kr_ref[...])

    cq = cq_ref[...]
    qa = _dot(cq, wq_ref[...])
    if rope:
        qb = _dot(cq, wqr_ref[...])
        q0 = pl.multiple_of(qi * tq, tq)
        cos_t = cos_ref[pl.ds(q0, tq), :]
        sin_t = sin_ref[pl.ds(q0, tq), :]
    for j in range(N_HEADS // 2):
        pair = None
        for h in (2 * j, 2 * j + 1):
            cols = slice(h * HEAD_PAD, (h + 1) * HEAD_PAD)
            qh = qa[:, cols]
            if rope:
                qh = qh * cos_t + qb[:, cols] * sin_t
            s = lax.dot_general(qh.astype(BF16), k_scr[:, cols], (((1,), (1,)), ((), ())),
                                preferred_element_type=F32)
            e = jnp.exp((s - jnp.max(s, axis=-1, keepdims=True)) * scale)
            denom = jnp.sum(e, axis=-1, keepdims=True)
            o = _dot(e.astype(BF16), v_scr[:, cols]) * (1.0 / denom)
            pair = o if pair is None else pair + o
        o_ref[:, j * LANE:(j + 1) * LANE] = pair.astype(BF16)


def _attention(layer, cq, ckv, kr, row_start, nseq, seq, w_q, w_k, w_v, lat=None):
    rope = lat is not None
    tq = min(_ATT_Q_TILE, seq)
    assert seq % tq == 0 and row_start % seq == 0
    past = lat[1].shape[2] if rope else 0
    total = past + seq
    hp = N_HEADS * HEAD_PAD
    nq = seq // tq
    s_off, q_off = row_start // seq, row_start // tq
    full = lambda w: pl.BlockSpec((seq, w), lambda b, q: (s_off + b, 0))
    q_rows = lambda w: pl.BlockSpec((tq, w), lambda b, q: (q_off + b * nq + q, 0))
    in_specs = [q_rows(Q_LORA), full(KV_LORA), full(HEAD_PAD)]
    args = [cq, ckv, kr]
    if rope:
        krr, pckv, pkr, cos, sin, w_q_rot = lat
        in_specs += [full(HEAD_PAD),
                     pl.BlockSpec((None, None, past, KV_LORA), lambda b, q: (b, layer, 0, 0)),
                     pl.BlockSpec((None, None, past, HEAD_PAD), lambda b, q: (b, layer, 0, 0)),
                     _fixed_spec((seq, HEAD_PAD)), _fixed_spec((seq, HEAD_PAD))]
        args += [krr, pckv, pkr, cos, sin]
    in_specs.append(_fixed_spec((Q_LORA, hp), (layer,)))
    args.append(w_q)
    if rope:
        in_specs.append(_fixed_spec((Q_LORA, hp), (layer,)))
        args.append(w_q_rot)
    in_specs += [_fixed_spec((KV_LORA, hp), (layer,)), _fixed_spec((KV_LORA, hp), (layer,))]
    args += [w_k, w_v]
    return pl.pallas_call(
        functools.partial(_attn_kernel, past=past, tq=tq, rope=rope),
        grid=(nseq, nq),
        in_specs=in_specs,
        out_specs=pl.BlockSpec((tq, N_HEADS * V_HEAD), lambda b, q: (b * nq + q, 0)),
        out_shape=jax.ShapeDtypeStruct((nseq * seq, N_HEADS * V_HEAD), BF16),
        scratch_shapes=[pltpu.VMEM((total, hp), BF16), pltpu.VMEM((total, hp), BF16)],
        compiler_params=_params("arbitrary", "arbitrary"),
        name=f"attn_{seq}",
    )(*args)


def _mix_kernel(x_ref, hsc_ref, hsl_ref, atc_ref, atl_ref, m_ref, wygg_ref, wro_ref, wao_ref, wo_ref,
                g_ref, b_ref, o_ref, wro_bf, wao_bf, wo_bf, *, n_ctx_tiles):
    _cast_once([(wro_ref, wro_bf), (wao_ref, wao_bf), (wo_ref, wo_bf)])
    is_ctx = pl.program_id(0) < n_ctx_tiles
    hs = jnp.where(is_ctx, hsc_ref[...], hsl_ref[...])
    at = jnp.where(is_ctx, atc_ref[...], atl_ref[...])
    x = x_ref[...]
    h = (x * (1.0 + m_ref[1:2, :]) + m_ref[0:1, :]).astype(BF16)
    ygg = _dot(h, wygg_ref[...])
    u = (hs * jax.nn.gelu(ygg[:, :D_RNN])).astype(BF16)
    rnn_out = _dot(u, wro_bf[...])
    att_out = _dot(at, wao_bf[...])
    merged = (jax.nn.sigmoid(ygg[:, D_RNN:D_RNN + D_MODEL]) * rnn_out
              + jax.nn.sigmoid(ygg[:, D_RNN + D_MODEL:]) * att_out)
    m = _dot(merged.astype(BF16), wo_bf[...])
    o_ref[...] = _layer_norm(ALPHA * x + m_ref[2:3, :] * m, g_ref[...], b_ref[...])


def _mix(layer, x, hs_ctx, hs_lat, at_ctx, at_lat, mods, w_ygg, w_ro, w_ao, w_o, ln_g, ln_b, n_ctx_rows,
         rows_per_latent):
    n_tok = x.shape[0]
    tm = _ROW_TILE
    nct = n_ctx_rows // tm
    row = pl.BlockSpec((tm, D_MODEL), lambda i: (i, 0))
    ctx_row = pl.BlockSpec((tm, D_MODEL), lambda i: (jnp.minimum(i, nct - 1), 0))
    lat_row = pl.BlockSpec((tm, D_MODEL), lambda i: (jnp.maximum(i - nct, 0), 0))
    sq = (D_MODEL, D_MODEL)
    return pl.pallas_call(
        functools.partial(_mix_kernel, n_ctx_tiles=nct),
        grid=(n_tok // tm,),
        in_specs=[row, ctx_row, lat_row, ctx_row, lat_row,
                  _mod_spec(layer, tm, n_ctx_rows, rows_per_latent),
                  _fixed_spec((D_MODEL, D_RNN + 2 * D_MODEL), (layer,)),
                  _fixed_spec(sq, (layer,)), _fixed_spec(sq, (layer,)), _fixed_spec(sq, (layer,)),
                  _fixed_spec((1, D_MODEL), (layer,)), _fixed_spec((1, D_MODEL), (layer,))],
        out_specs=row,
        out_shape=jax.ShapeDtypeStruct((n_tok, D_MODEL), F32),
        scratch_shapes=[pltpu.VMEM(sq, BF16) for _ in range(3)],
        compiler_params=_params("arbitrary"),
        name="mix_out",
    )(x, hs_ctx, hs_lat, at_ctx, at_lat, mods, w_ygg, w_ro, w_ao, w_o, ln_g, ln_b)


def _ffn_kernel(x_ref, m_ref, w1_ref, w3_ref, w2_ref, g_ref, b_ref, o_ref):
    x = x_ref[...]
    h = (x * (1.0 + m_ref[4:5, :]) + m_ref[3:4, :]).astype(BF16)
    f = None
    for c in range(D_FF // _FF_CHUNK):
        cols = slice(c * _FF_CHUNK, (c + 1) * _FF_CHUNK)
        act = (jax.nn.silu(_dot(h, w1_ref[:, cols])) * _dot(h, w3_ref[:, cols])).astype(BF16)
        y = _dot(act, w2_ref[cols, :])
        f = y if f is None else f + y
    o_ref[...] = _layer_norm(ALPHA * x + m_ref[5:6, :] * f, g_ref[...], b_ref[...])


def _ffn_dense(layer, x, mods, w1, w3, w2, ln_g, ln_b, n_ctx_rows, rows_per_latent):
    n_tok = x.shape[0]
    tm = _ROW_TILE
    j = layer // 2
    row = pl.BlockSpec((tm, D_MODEL), lambda i: (i, 0))
    return pl.pallas_call(
        _ffn_kernel,
        grid=(n_tok // tm,),
        in_specs=[row, _mod_spec(layer, tm, n_ctx_rows, rows_per_latent),
                  _fixed_spec((D_MODEL, D_FF), (j,)), _fixed_spec((D_MODEL, D_FF), (j,)),
                  _fixed_spec((D_FF, D_MODEL), (j,)),
                  _fixed_spec((1, D_MODEL), (layer,)), _fixed_spec((1, D_MODEL), (layer,))],
        out_specs=row,
        out_shape=jax.ShapeDtypeStruct((n_tok, D_MODEL), F32),
        compiler_params=_params("arbitrary"),
        name="ffn_dense",
    )(x, mods, w1, w3, w2, ln_g, ln_b)


def _split_bf16(a):
    hi = a.astype(BF16)
    return hi, (a - hi.astype(F32)).astype(BF16)


def _route_kernel(x_ref, m_ref, wr_ref, br_ref, h_ref, ids_ref, gw_ref, cnt_ref, run_scr, tri_scr):
    tm = x_ref.shape[0]

    @pl.when(pl.program_id(0) == 0)
    def _():
        run_scr[...] = jnp.zeros_like(run_scr)
        r = lax.broadcasted_iota(jnp.int32, (tm, tm), 0)
        c = lax.broadcasted_iota(jnp.int32, (tm, tm), 1)
        tri_scr[...] = jnp.where(c < r, 1.0, 0.0).astype(BF16)

    h = x_ref[...] * (1.0 + m_ref[4:5, :]) + m_ref[3:4, :]
    h_ref[...] = h
    h_hi, h_lo = _split_bf16(h)

    w_hi, w_lo = _split_bf16(wr_ref[...])
    logits = _dot(h_hi, w_hi) + (_dot(h_hi, w_lo) + _dot(h_lo, w_hi)) + br_ref[...]
    lane = lax.broadcasted_iota(jnp.int32, logits.shape, 1)
    big = jnp.int32(N_EXPERTS)
    v1 = jnp.max(logits, axis=-1, keepdims=True)
    i1 = jnp.min(jnp.where(logits == v1, lane, big), axis=-1, keepdims=True)
    rest = jnp.where(lane == i1, -jnp.inf, logits)
    v2 = jnp.max(rest, axis=-1, keepdims=True)
    i2 = jnp.min(jnp.where(rest == v2, lane, big), axis=-1, keepdims=True)
    e2 = jnp.exp(v2 - v1)
    g1 = 1.0 / (1.0 + e2)
    g2 = e2 / (1.0 + e2)

    m1, m2 = lane == i1, lane == i2
    member = jnp.where(m1, 1.0, 0.0) + jnp.where(m2, 1.0, 0.0)
    before = _dot(tri_scr[...], member.astype(BF16)) + run_scr[...]
    r1 = jnp.sum(jnp.where(m1, before, 0.0), axis=-1, keepdims=True).astype(jnp.int32)
    r2 = jnp.sum(jnp.where(m2, before, 0.0), axis=-1, keepdims=True).astype(jnp.int32)
    run_scr[...] += jnp.sum(member, axis=0, keepdims=True)

    ids_ref[...] = jnp.where(lane == 0, i1, jnp.where(lane == 1, i2, jnp.where(lane == 2, r1, r2)))
    gw_ref[...] = jnp.where(lane == 0, g1, g2)
    cnt_ref[...] = run_scr[...].astype(jnp.int32)


def _moe_route(layer, x, mods, w_router, b_router, n_ctx_rows, rows_per_latent):
    n_tok = x.shape[0]
    tm = _ROW_TILE
    j = layer // 2
    row = lambda w: pl.BlockSpec((tm, w), lambda i: (i, 0))
    return pl.pallas_call(
        _route_kernel,
        grid=(n_tok // tm,),
        in_specs=[row(D_MODEL), _mod_spec(layer, tm, n_ctx_rows, rows_per_latent),
                  _fixed_spec((D_MODEL, N_EXPERTS), (j,)), _fixed_spec((1, N_EXPERTS), (j,))],
        out_specs=[row(D_MODEL), row(N_EXPERTS), row(N_EXPERTS),
                   pl.BlockSpec((1, N_EXPERTS), lambda i: (0, 0))],
        out_shape=[jax.ShapeDtypeStruct((n_tok, D_MODEL), F32),
                   jax.ShapeDtypeStruct((n_tok, N_EXPERTS), jnp.int32),
                   jax.ShapeDtypeStruct((n_tok, N_EXPERTS), F32),
                   jax.ShapeDtypeStruct((1, N_EXPERTS), jnp.int32)],
        scratch_shapes=[pltpu.VMEM((1, N_EXPERTS), F32), pltpu.VMEM((tm, tm), BF16)],
        compiler_params=_params("arbitrary"),
        name="moe_route",
    )(x, mods, w_router, b_router)


def _tables_kernel(er_ref, g_ref, cnt_ref, exp_ref, on_ref, tok_ref, gate_ref, start_scr, *, n_tok, tm):
    n_slots = tok_ref.shape[0]
    n_tiles = exp_ref.shape[0]
    ends = []
    total = jnp.int32(0)
    for e in range(N_EXPERTS):
        start_scr[e] = total
        total = total + ((cnt_ref[0, e] + (tm - 1)) // tm) * tm
        ends.append(total)

    def clear(s, carry):
        tok_ref[s] = jnp.int32(n_tok)
        gate_ref[s] = jnp.float32(0.0)
        return carry

    lax.fori_loop(0, n_slots, clear, 0, unroll=8)

    def place(t, carry):
        for k in range(TOP_K):
            slot = start_scr[er_ref[2 * TOP_K * t + k]] + er_ref[2 * TOP_K * t + TOP_K + k]
            tok_ref[slot] = t
            gate_ref[slot] = g_ref[TOP_K * t + k]
        return carry

    lax.fori_loop(0, n_tok, place, 0, unroll=4)

    def tiles(i, carry):
        first = i * tm
        owner = jnp.int32(0)
        for e in range(N_EXPERTS):
            owner = owner + jnp.where(first >= ends[e], 1, 0)
        exp_ref[i] = jnp.minimum(owner, N_EXPERTS - 1)
        on_ref[i] = jnp.where(first < total, 1, 0)
        return carry

    lax.fori_loop(0, n_tiles, tiles, 0)


def _slot_tables(ids, gw, counts, n_tok):
    tm = _MOE_SLOT_TILE
    n_slots = TOP_K * n_tok + N_EXPERTS * tm
    n_tiles = n_slots // tm
    smem = pl.BlockSpec(memory_space=pltpu.SMEM)
    return pl.pallas_call(
        functools.partial(_tables_kernel, n_tok=n_tok, tm=tm),
        in_specs=[smem, smem, smem],
        out_specs=[smem, smem, smem, smem],
        out_shape=[jax.ShapeDtypeStruct((n_tiles,), jnp.int32), jax.ShapeDtypeStruct((n_tiles,), jnp.int32),
                   jax.ShapeDtypeStruct((n_slots,), jnp.int32), jax.ShapeDtypeStruct((n_slots,), F32)],
        scratch_shapes=[pltpu.SMEM((N_EXPERTS,), jnp.int32)],
        name="moe_tables",
    )(ids[:, :2 * TOP_K].reshape(-1), gw[:, :TOP_K].reshape(-1), counts)


def _moe_up_kernel(exp_ref, on_ref, tok_ref, h_ref, w1_ref, w3_ref, act_ref, xg0_scr, xg1_scr, *,
                   n_tok, n_tiles):
    t = pl.program_id(0)
    tm = xg0_scr.shape[0]

    def gather_row(tile, r, dst):
        src = jnp.minimum(tok_ref[tile * tm + r], n_tok - 1)
        dst[pl.ds(r, 1), :] = h_ref[pl.ds(src, 1), :]

    @pl.when(t == 0)
    def _():
        def body(r, carry):
            gather_row(0, r, xg0_scr)
            return carry

        lax.fori_loop(0, tm, body, 0, unroll=8)

    for parity, (cur, nxt) in enumerate(((xg0_scr, xg1_scr), (xg1_scr, xg0_scr))):
        @pl.when(jnp.logical_and(on_ref[t] == 1, t % 2 == parity))
        def _():
            h = cur[...].astype(BF16)
            nxt_tile = jnp.minimum(t + 1, n_tiles - 1)
            for r in range(tm):
                gather_row(nxt_tile, r, nxt)
            act_ref[...] = (jax.nn.silu(_dot(h, w1_ref[...])) * _dot(h, w3_ref[...])).astype(BF16)

    @pl.when(on_ref[t] == 0)
    def _():
        act_ref[...] = jnp.zeros_like(act_ref)


def _moe_up(layer, tables, h, w1, w3):
    tile_exp, tile_on, slot_tok, _ = tables
    n_tok = h.shape[0]
    tm = _MOE_SLOT_TILE
    n_tiles = tile_exp.shape[0]
    j = layer // 2
    wspec = pl.BlockSpec((None, None, D_MODEL, D_EXPERT), lambda t, ex, on, tk: (j, ex[t], 0, 0),
                         pipeline_mode=pl.Buffered(1))
    return pl.pallas_call(
        functools.partial(_moe_up_kernel, n_tok=n_tok, n_tiles=n_tiles),
        grid_spec=pltpu.PrefetchScalarGridSpec(
            num_scalar_prefetch=3,
            grid=(n_tiles,),
            in_specs=[pl.BlockSpec((n_tok, D_MODEL), lambda t, ex, on, tk: (0, 0),
                                   pipeline_mode=pl.Buffered(1)),
                      wspec, wspec],
            out_specs=pl.BlockSpec((tm, D_EXPERT), lambda t, ex, on, tk: (t, 0)),
            scratch_shapes=[pltpu.VMEM((tm, D_MODEL), F32), pltpu.VMEM((tm, D_MODEL), F32)],
        ),
        out_shape=jax.ShapeDtypeStruct((n_tiles * tm, D_EXPERT), BF16),
        compiler_params=_params("arbitrary"),
        name="moe_up",
    )(tile_exp, tile_on, slot_tok, h, w1, w3)


def _moe_down_kernel(exp_ref, on_ref, tok_ref, gate_ref, act_ref, w2_ref, x_ref, m_ref, g_ref, b_ref,
                     o_ref, acc_scr, y0_scr, y1_scr, *, n_tiles, ln_rows):
    t = pl.program_id(0)
    tm = y0_scr.shape[0]

    @pl.when(t == 0)
    def _():
        acc_scr[...] = jnp.zeros_like(acc_scr)

    def project(dst):
        dst[...] = _dot(act_ref[...], w2_ref[...])

    def scatter_group(tile, base, ys):
        slots = [tile * tm + base + u for u in range(SUBLANE)]
        toks = [tok_ref[s] for s in slots]
        rows = [acc_scr[pl.ds(toks[u], 1), :] for u in range(SUBLANE)]
        for u in range(SUBLANE):
            acc_scr[pl.ds(toks[u], 1), :] = rows[u] + gate_ref[slots[u]] * ys[u:u + 1, :]

    last = n_tiles - 1
    project_now = jnp.logical_and(t < n_tiles, on_ref[jnp.minimum(t, last)] == 1)
    add_prev = jnp.logical_and(jnp.logical_and(t >= 1, t <= n_tiles), on_ref[jnp.clip(t - 1, 0, last)] == 1)
    for parity, (cur, prv) in enumerate(((y0_scr, y1_scr), (y1_scr, y0_scr))):
        mine = t % 2 == parity

        @pl.when(jnp.logical_and(mine, jnp.logical_and(project_now, add_prev)))
        def _():
            project(cur)
            for g in range(tm // SUBLANE):
                scatter_group(t - 1, g * SUBLANE, prv[g * SUBLANE:(g + 1) * SUBLANE, :])

        @pl.when(jnp.logical_and(mine, jnp.logical_and(project_now, jnp.logical_not(add_prev))))
        def _():
            project(cur)

        @pl.when(jnp.logical_and(mine, jnp.logical_and(add_prev, jnp.logical_not(project_now))))
        def _():
            def body(g, carry):
                base = pl.multiple_of(g * SUBLANE, SUBLANE)
                scatter_group(t - 1, base, prv[pl.ds(base, SUBLANE), :])
                return carry

            lax.fori_loop(0, tm // SUBLANE, body, 0)

    @pl.when(t > n_tiles)
    def _():
        r0 = pl.multiple_of((t - n_tiles - 1) * ln_rows, ln_rows)
        f = acc_scr[pl.ds(r0, ln_rows), :]
        o_ref[...] = _layer_norm(ALPHA * x_ref[...] + m_ref[5:6, :] * f, g_ref[...], b_ref[...])


def _moe_down(layer, tables, act, x, mods, w2, ln_g, ln_b, n_ctx_rows, rows_per_latent):
    tile_exp, tile_on, slot_tok, slot_gate = tables
    n_tok = x.shape[0]
    tm = _MOE_SLOT_TILE
    tl = _MOE_LN_TILE
    n_tiles = tile_exp.shape[0]
    j = layer // 2
    slot = lambda t: jnp.minimum(t, n_tiles - 1)
    ln = lambda t: jnp.maximum(t - n_tiles - 1, 0)
    n_ctx_tiles, per = n_ctx_rows // tl, rows_per_latent // tl

    def mod_index(t, *_):
        i = ln(t)
        return (layer, jnp.where(i < n_ctx_tiles, 0, 1 + (i - n_ctx_tiles) // per), 0, 0)

    fixed = lambda shape, lead: pl.BlockSpec((None,) + shape, lambda *_: (lead,) + (0,) * len(shape),
                                             pipeline_mode=pl.Buffered(1))
    return pl.pallas_call(
        functools.partial(_moe_down_kernel, n_tiles=n_tiles, ln_rows=tl),
        grid_spec=pltpu.PrefetchScalarGridSpec(
            num_scalar_prefetch=4,
            grid=(n_tiles + 1 + n_tok // tl,),
            in_specs=[pl.BlockSpec((tm, D_EXPERT), lambda t, ex, *_: (slot(t), 0)),
                      pl.BlockSpec((None, None, D_EXPERT, D_MODEL), lambda t, ex, *_: (j, ex[slot(t)], 0, 0)),
                      pl.BlockSpec((tl, D_MODEL), lambda t, *_: (ln(t), 0)),
                      pl.BlockSpec((None, None, 6, D_MODEL), mod_index),
                      fixed((1, D_MODEL), layer), fixed((1, D_MODEL), layer)],
            out_specs=pl.BlockSpec((tl, D_MODEL), lambda t, *_: (ln(t), 0)),
            scratch_shapes=[pltpu.VMEM((n_tok + SUBLANE, D_MODEL), F32), pltpu.VMEM((tm, D_MODEL), F32),
                            pltpu.VMEM((tm, D_MODEL), F32)],
        ),
        out_shape=jax.ShapeDtypeStruct((n_tok, D_MODEL), F32),
        compiler_params=_params("arbitrary"),
        name="moe_down",
    )(tile_exp, tile_on, slot_tok, slot_gate, act, w2, x, mods, ln_g, ln_b)


def _ffn_moe(layer, x, mods, w_router, b_router, w1, w3, w2, ln_g, ln_b, n_ctx_rows, rows_per_latent):
    h, ids, gw, counts = _moe_route(layer, x, mods, w_router, b_router, n_ctx_rows, rows_per_latent)
    tables = _slot_tables(ids, gw, counts, x.shape[0])
    act = _moe_up(layer, tables, h, w1, w3)
    return _moe_down(layer, tables, act, x, mods, w2, ln_g, ln_b, n_ctx_rows, rows_per_latent)


def _rot_cols(w):
    ws = w.reshape(w.shape[:-1] + (2, 2, ROPE_AXIS // 2))
    return jnp.stack([-ws[..., 1, :], ws[..., 0, :]], axis=-2).reshape(w.shape)


def _pad_rope_cols(w):
    return jnp.pad(w, ((0, 0),) * (w.ndim - 1) + ((ROPE_OFF, HEAD_PAD - ROPE_OFF - QK_ROPE),))


def _rope_tables(rows):
    r = jnp.repeat(jnp.arange(rows, dtype=F32), GRID_W)
    col = jnp.tile(jnp.arange(GRID_W, dtype=F32), rows)
    freqs = ROPE_THETA ** (-jnp.arange(0, ROPE_AXIS, 2, dtype=F32) / ROPE_AXIS)
    ar = r[:, None] * freqs
    ac = col[:, None] * freqs
    ang = jnp.concatenate([ar, ar, ac, ac], axis=-1)
    cos, sin = jnp.cos(ang), jnp.sin(ang)
    n = cos.shape[0]
    cos_p = jnp.concatenate([jnp.ones((n, ROPE_OFF), F32), cos,
                             jnp.zeros((n, HEAD_PAD - ROPE_OFF - QK_ROPE), F32)], axis=-1)
    return cos_p, _pad_rope_cols(sin)


def _prepared_weights(w_in, rg_wa, rg_ba, rg_wx, rg_bx, w_qb, w_kvb):
    bf = lambda a: a.astype(BF16)
    w_kr = w_in[:, :, _COL_KR:_COL_KR + QK_ROPE]
    w_s = jnp.concatenate([w_in[:, :, _COL_Q:_COL_KR], _pad_rope_cols(w_kr),
                           _pad_rope_cols(_rot_cols(w_kr))], axis=-1)
    w_x = w_in[:, :, :D_RNN]
    w_ygg = jnp.concatenate([w_in[:, :, D_RNN:2 * D_RNN], w_in[:, :, _COL_GATE:]], axis=-1)

    wq = w_qb.reshape(DEPTH, Q_LORA, N_HEADS, QK_NOPE + QK_ROPE)
    q_nope, q_rope = wq[..., :QK_NOPE], wq[..., QK_NOPE:]
    tail = jnp.zeros((DEPTH, Q_LORA, N_HEADS, HEAD_PAD - QK_NOPE - QK_ROPE), F32)
    hp = N_HEADS * HEAD_PAD
    w_q = jnp.concatenate([q_nope, q_rope, tail], axis=-1).reshape(DEPTH, Q_LORA, hp)
    w_q_rot = jnp.concatenate([jnp.zeros_like(q_nope), _rot_cols(q_rope), tail], axis=-1)
    w_q_rot = w_q_rot.reshape(DEPTH, Q_LORA, hp)
    wkv = w_kvb.reshape(DEPTH, KV_LORA, N_HEADS, QK_NOPE + V_HEAD)
    k_nope, v = wkv[..., :QK_NOPE], wkv[..., QK_NOPE:]
    w_k = jnp.concatenate([k_nope, jnp.zeros((DEPTH, KV_LORA, N_HEADS, HEAD_PAD - QK_NOPE), F32)], axis=-1)
    w_k = w_k.reshape(DEPTH, KV_LORA, hp)
    vz = jnp.zeros_like(v)
    odd = (jnp.arange(N_HEADS) % 2 == 1)[None, None, :, None]
    w_v = jnp.where(odd, jnp.concatenate([vz, v], axis=-1), jnp.concatenate([v, vz], axis=-1))
    w_v = w_v.reshape(DEPTH, KV_LORA, hp)

    nblk = RNN_GROUP // RNN_BW
    ng = D_RNN // RNN_GROUP
    eye = jnp.eye(nblk, dtype=F32)

    def block_diag(w):
        wg = w.reshape(DEPTH, ng, nblk, RNN_BW, RNN_BW)
        return jnp.einsum('lgnjk,nm->lgnjmk', wg, eye).reshape(DEPTH, ng, RNN_GROUP, RNN_GROUP)

    w_gate = jnp.concatenate([block_diag(rg_wa[:, 0]), block_diag(rg_wx[:, 0]),
                              block_diag(rg_wa[:, 1]), block_diag(rg_wx[:, 1])], axis=-1)
    grp = lambda b: b.reshape(DEPTH, ng, 1, RNN_GROUP)
    b_gate = jnp.concatenate([grp(rg_ba[:, 0]), grp(rg_bx[:, 0]), grp(rg_ba[:, 1]), grp(rg_bx[:, 1])],
                             axis=-1)
    return dict(w_x=bf(w_x), w_s=bf(w_s), w_ygg=bf(w_ygg), w_q=bf(w_q), w_q_rot=bf(w_q_rot), w_k=bf(w_k), w_v=bf(w_v),
                w_gate=bf(w_gate), b_gate=b_gate)


def kernel(x_prompt, x_sample, cache_ckv, cache_krope, state_rnn, c, c_ctx, w_ada, b_ada, w_in, conv_w, conv_b, rg_wa, rg_ba, rg_wx, rg_bx, rg_lambda, w_rnn_out, q_norm_g, w_qb, kv_norm_g, w_kvb, w_attn_out, w_out, ln1_g, ln1_b, ln2_g, ln2_b, w1_dense, w3_dense, w2_dense, w_router, b_router, w1_exp, w3_exp, w2_exp):
    nb, seq, _ = x_prompt.shape
    db, dseq, _ = x_sample.shape
    n_ctx = nb * seq
    n_lat = db * dseq
    assert n_ctx % _ROW_TILE == 0 and dseq % _ROW_TILE == 0 and db + 1 <= 8
    bf = lambda a: a.astype(BF16)
    vec = lambda a: a.reshape(a.shape[0], 1, a.shape[-1])

    cond8 = jnp.concatenate([c_ctx[None, :], c, jnp.zeros((8 - 1 - db, D_MODEL), F32)], axis=0)
    mods = _ada_mods(cond8, w_ada, b_ada)
    cos_p, sin_p = _rope_tables(dseq // GRID_W)
    past_kr = _pad_rope_cols(cache_krope)
    pw = _prepared_weights(w_in, rg_wa, rg_ba, rg_wx, rg_bx, w_qb, w_kvb)
    w1d, w3d, w2d = bf(w1_dense), bf(w3_dense), bf(w2_dense)
    w1e, w3e, w2e = bf(w1_exp), bf(w3_exp), bf(w2_exp)
    qg, kvg, cb = vec(q_norm_g), vec(kv_norm_g), vec(conv_b)
    l1g, l1b, l2g, l2b = vec(ln1_g), vec(ln1_b), vec(ln2_g), vec(ln2_b)
    br = vec(b_router)
    tiles = (n_ctx, dseq)

    x = jnp.concatenate([x_prompt.reshape(n_ctx, D_MODEL), x_sample.reshape(n_lat, D_MODEL)], axis=0)
    new_ckv, new_kr, new_h = [], [], []
    for l in range(DEPTH):
        xr, cq, ckv, kr, krr = _in_proj(l, x, mods, pw['w_x'], pw['w_s'], qg, kvg, *tiles)
        new_ckv.append(ckv[:n_ctx].reshape(nb, seq, KV_LORA))
        new_kr.append(kr[:n_ctx, ROPE_OFF:ROPE_OFF + QK_ROPE].reshape(nb, seq, QK_ROPE))

        rnn_args = (conv_w, cb, pw['w_gate'], pw['b_gate'], rg_lambda)
        hs_ctx, h_last = _rglru(l, xr, 0, nb, seq, *rnn_args, emit_last=True)
        (hs_lat,) = _rglru(l, xr, n_ctx, db, dseq, *rnn_args, h0=state_rnn)
        new_h.append(h_last)

        att_w = (pw['w_q'], pw['w_k'], pw['w_v'])
        at_ctx = _attention(l, cq, ckv, kr, 0, nb, seq, *att_w)
        at_lat = _attention(l, cq, ckv, kr, n_ctx, db, dseq, *att_w,
                            lat=(krr, cache_ckv, past_kr, cos_p, sin_p, pw['w_q_rot']))

        x = _mix(l, x, hs_ctx, hs_lat, at_ctx, at_lat, mods, pw['w_ygg'], w_rnn_out, w_attn_out, w_out,
                 l1g, l1b, *tiles)
        if l % 2 == 0:
            x = _ffn_dense(l, x, mods, w1d, w3d, w2d, l2g, l2b, *tiles)
        else:
            x = _ffn_moe(l, x, mods, w_router, br, w1e, w3e, w2e, l2g, l2b, *tiles)

    return (x[:n_ctx].reshape(nb, seq, D_MODEL), x[n_ctx:].reshape(db, dseq, D_MODEL),
            jnp.stack(new_ckv, axis=1), jnp.stack(new_kr, axis=1), jnp.stack(new_h, axis=1))
```

```python
import functools
import math

import jax
import jax.numpy as jnp
from jax import lax
from jax.experimental import pallas as pl
from jax.experimental.pallas import tpu as pltpu

F32 = jnp.float32
BF16 = jnp.bfloat16

D_MODEL = 1024
DEPTH = 4
GRID_W = 64
D_RNN = D_MODEL
RNN_BLOCKS = 16
RNN_BW = D_RNN // RNN_BLOCKS
CONV_W = 4
RG_C = 8.0
N_HEADS = 16
QK_NOPE = 64
QK_ROPE = 32
V_HEAD = 64
Q_LORA = 256
KV_LORA = 128
ROPE_AXIS = QK_ROPE // 2
ROPE_THETA = 10000.0
D_FF = 2816
N_EXPERTS = 8
TOP_K = 2
D_EXPERT = 1408
ALPHA = (2 * DEPTH) ** 0.25
LN_EPS = 1e-5
RMS_EPS = 1e-6

LANE = 128
SUBLANE = 8
HEAD_PAD = LANE
ROPE_OFF = QK_NOPE
RNN_GROUP = 256
VMEM_LIMIT = 56 * 1024 * 1024

_ROW_TILE = 512
_MOE_SLOT_TILE = 512
_MOE_LN_TILE = 256
_ATT_Q_TILE = 512
_FF_CHUNK = 1408

_COL_Q = 2 * D_RNN
_COL_KR = _COL_Q + Q_LORA + KV_LORA
_COL_GATE = _COL_KR + QK_ROPE


def _dot(a, b):
    return jnp.dot(a, b, preferred_element_type=F32)


def _layer_norm(y, g, b):
    mu = jnp.mean(y, axis=-1, keepdims=True)
    d = y - mu
    var = jnp.mean(d * d, axis=-1, keepdims=True)
    return d * lax.rsqrt(var + LN_EPS) * g + b


def _rms_norm(y, g):
    return y * lax.rsqrt(jnp.mean(y * y, axis=-1, keepdims=True) + RMS_EPS) * g


def _params(*sem):
    return pltpu.CompilerParams(dimension_semantics=sem, vmem_limit_bytes=VMEM_LIMIT)


def _fixed_spec(shape, lead=(), tail=None):
    tail = (0,) * len(shape) if tail is None else tuple(tail)
    index = tuple(lead) + tail
    return pl.BlockSpec((None,) * len(lead) + tuple(shape), lambda *_: index,
                        pipeline_mode=pl.Buffered(1))


def _mod_spec(layer, tm, n_ctx_rows, rows_per_latent):
    n_ctx_tiles = n_ctx_rows // tm
    per = rows_per_latent // tm

    def index(i, *_):
        return (layer, jnp.where(i < n_ctx_tiles, 0, 1 + (i - n_ctx_tiles) // per), 0, 0)

    return pl.BlockSpec((None, None, 6, D_MODEL), index)


def _cast_once(pairs):
    @pl.when(pl.program_id(0) == 0)
    def _():
        for src, dst in pairs:
            dst[...] = src[...].astype(BF16)


def _stream_specs(x, tm, n_ctx_rows):
    if isinstance(x, tuple):
        nct = n_ctx_rows // tm
        return ([pl.BlockSpec((tm, D_MODEL), lambda i, *_: (jnp.minimum(i, nct - 1), 0)),
                 pl.BlockSpec((tm, D_MODEL), lambda i, *_: (jnp.maximum(i - nct, 0), 0))], list(x))
    return [pl.BlockSpec((tm, D_MODEL), lambda i, *_: (i, 0))], [x]


def _load_stream(refs, n_ctx_tiles):
    if len(refs) == 2:
        return jnp.where(pl.program_id(0) < n_ctx_tiles, refs[0][...], refs[1][...])
    return refs[0][...]


def _mods_kernel(c_ref, w_ref, b_ref, o_ref):
    s = jax.nn.silu(c_ref[...]).astype(BF16)
    o_ref[...] = _dot(s, w_ref[...].astype(BF16)) + b_ref[...]


def _ada_mods(cond8, w_ada, b_ada):
    out = pl.pallas_call(
        _mods_kernel,
        grid=(DEPTH, 6),
        in_specs=[
            pl.BlockSpec((8, D_MODEL), lambda l, j: (0, 0)),
            pl.BlockSpec((None, D_MODEL, D_MODEL), lambda l, j: (l, 0, j)),
            pl.BlockSpec((None, 1, D_MODEL), lambda l, j: (l, 0, j)),
        ],
        out_specs=pl.BlockSpec((None, 8, D_MODEL), lambda l, j: (l, 0, j)),
        out_shape=jax.ShapeDtypeStruct((DEPTH, 8, 6 * D_MODEL), F32),
        compiler_params=_params("arbitrary", "arbitrary"),
        name="ada_mods",
    )(cond8, w_ada, b_ada.reshape(DEPTH, 1, 6 * D_MODEL))
    return out.reshape(DEPTH, 8, 6, D_MODEL)


_S_COLS = Q_LORA + KV_LORA + 2 * HEAD_PAD


def _inproj_kernel(*refs, n_x, n_ctx_tiles):
    m_ref, wx_ref, ws_ref, qg_ref, kvg_ref, xr_ref, cq_ref, ckv_ref, kr_ref, krr_ref = refs[n_x:]
    x = _load_stream(refs[:n_x], n_ctx_tiles)
    h = (x * (1.0 + m_ref[1:2, :]) + m_ref[0:1, :]).astype(BF16)
    xr_ref[...] = _dot(h, wx_ref[...])
    s = _dot(h, ws_ref[...])
    cq_ref[...] = _rms_norm(s[:, :Q_LORA], qg_ref[...]).astype(BF16)
    ckv_ref[...] = _rms_norm(s[:, Q_LORA:Q_LORA + KV_LORA], kvg_ref[...])
    kr_ref[...] = s[:, Q_LORA + KV_LORA:Q_LORA + KV_LORA + HEAD_PAD]
    krr_ref[...] = s[:, Q_LORA + KV_LORA + HEAD_PAD:]


def _in_proj(layer, x, mods, w_x, w_s, q_g, kv_g, n_ctx_rows, rows_per_latent):
    tm = _ROW_TILE
    x_specs, xs = _stream_specs(x, tm, n_ctx_rows)
    n_tok = sum(a.shape[0] for a in xs)
    row = lambda w: pl.BlockSpec((tm, w), lambda i: (i, 0))
    return pl.pallas_call(
        functools.partial(_inproj_kernel, n_x=len(xs), n_ctx_tiles=n_ctx_rows // tm),
        grid=(n_tok // tm,),
        in_specs=x_specs + [
            _mod_spec(layer, tm, n_ctx_rows, rows_per_latent),
            _fixed_spec((D_MODEL, D_RNN), (layer,)),
            _fixed_spec((D_MODEL, _S_COLS), (layer,)),
            _fixed_spec((1, Q_LORA), (layer,)),
            _fixed_spec((1, KV_LORA), (layer,)),
        ],
        out_specs=[row(D_RNN), row(Q_LORA), row(KV_LORA), row(HEAD_PAD), row(HEAD_PAD)],
        out_shape=[
            jax.ShapeDtypeStruct((n_tok, D_RNN), F32),
            jax.ShapeDtypeStruct((n_tok, Q_LORA), BF16),
            jax.ShapeDtypeStruct((n_tok, KV_LORA), F32),
            jax.ShapeDtypeStruct((n_tok, HEAD_PAD), F32),
            jax.ShapeDtypeStruct((n_tok, HEAD_PAD), F32),
        ],
        compiler_params=_params("arbitrary"),
        name="in_proj",
    )(*xs, mods, w_x, w_s, q_g, kv_g)


def _segment_pitch(seg):
    assert seg % SUBLANE == 0
    return seg if (seg // SUBLANE) % 2 == 1 else seg + SUBLANE


def _rglru_kernel(*refs, seq, bt, has_h0, emit_last):
    it = iter(refs)
    x_ref, cw_ref, cb_ref, wg_ref, bg_ref, lam_ref = (next(it) for _ in range(6))
    h0_ref = next(it) if has_h0 else None
    hs_ref = next(it)
    hl_ref = next(it) if emit_last else None
    a_refs = (next(it), next(it))
    b_refs = (next(it), next(it))
    dg = RNN_GROUP
    nslab = dg // LANE
    seg = seq // SUBLANE
    pitch = _segment_pitch(seg)

    row = lax.broadcasted_iota(jnp.int32, (seq, dg), 0)
    decay_rate = [RG_C * jax.nn.softplus(-lam_ref[d:d + 1, :]) for d in range(2)]
    for b in range(bt):
        rows = slice(b * seq, (b + 1) * seq)
        x = x_ref[rows, :]
        xm1 = jnp.where(row >= 1, pltpu.roll(x, 1, 0), 0.0)
        xp1 = jnp.where(row < seq - 1, pltpu.roll(x, seq - 1, 0), 0.0)
        xp2 = jnp.where(row < seq - 2, pltpu.roll(x, seq - 2, 0), 0.0)
        xc = cb_ref[...] + xm1 * cw_ref[0:1, :]
        xc = xc + x * cw_ref[1:2, :]
        xc = xc + xp1 * cw_ref[2:3, :]
        xc = xc + xp2 * cw_ref[3:4, :]
        gates = _dot(xc.astype(BF16), wg_ref[...]) + bg_ref[...]
        for d in range(2):
            r = jax.nn.sigmoid(gates[:, (2 * d) * dg:(2 * d + 1) * dg])
            i = jax.nn.sigmoid(gates[:, (2 * d + 1) * dg:(2 * d + 2) * dg])
            neg_log_a = r * decay_rate[d]
            a = jnp.exp(-neg_log_a)
            y = jnp.tanh(neg_log_a) * (a * a + 1.0)
            bq = jnp.where(y > 0.0, y * lax.rsqrt(y), 0.0) * (i * xc)
            for s in range(SUBLANE):
                for c in range(nslab):
                    src = (slice(s * seg, (s + 1) * seg), slice(c * LANE, (c + 1) * LANE))
                    dst = slice(s * pitch, s * pitch + seg)
                    a_refs[d][b * nslab + c, dst, :] = a[src]
                    b_refs[d][b * nslab + c, dst, :] = bq[src]

    nch = bt * nslab
    zeros = tuple(jnp.zeros((SUBLANE, LANE), F32) for _ in range(nch))
    ones = tuple(jnp.ones((SUBLANE, LANE), F32) for _ in range(nch))

    def step(j, carry):
        out = []
        for d, t in ((0, j), (1, seg - 1 - j)):
            h_prev, p_prev = carry[2 * d], carry[2 * d + 1]
            h_new, p_new = [], []
            idx = pl.ds(t, SUBLANE, stride=pitch)
            for ch in range(nch):
                a = a_refs[d].at[ch][idx, :]
                h = a * h_prev[ch] + b_refs[d].at[ch][idx, :]
                p = a * p_prev[ch]
                b_refs[d].at[ch][idx, :] = h
                a_refs[d].at[ch][idx, :] = p
                h_new.append(h)
                p_new.append(p)
            out += [tuple(h_new), tuple(p_new)]
        return tuple(out)

    h_f, p_f, h_b, p_b = lax.fori_loop(0, seg, step, (zeros, ones, zeros, ones), unroll=2)

    for b in range(bt):
        for c in range(nslab):
            ch = b * nslab + c
            lanes = slice(c * LANE, (c + 1) * LANE)
            row_of = lambda v, s: v[s:s + 1, :]
            if has_h0:
                enter_f, enter_b = h0_ref[b, 0:1, lanes], h0_ref[b, 1:2, lanes]
            else:
                enter_f = enter_b = jnp.zeros((1, LANE), F32)
            ent_f, ent_b = [None] * SUBLANE, [None] * SUBLANE
            for s in range(SUBLANE):
                ent_f[s] = enter_f
                enter_f = row_of(h_f[ch], s) + row_of(p_f[ch], s) * enter_f
            for s in reversed(range(SUBLANE)):
                ent_b[s] = enter_b
                enter_b = row_of(h_b[ch], s) + row_of(p_b[ch], s) * enter_b
            if emit_last:
                hl_ref[b, 0:1, lanes] = enter_f
                hl_ref[b, 1:2, lanes] = enter_b
            for s in range(SUBLANE):
                src = slice(s * pitch, s * pitch + seg)
                fwd = b_refs[0][ch, src, :] + a_refs[0][ch, src, :] * ent_f[s]
                bwd = b_refs[1][ch, src, :] + a_refs[1][ch, src, :] * ent_b[s]
                hs_ref[b * seq + s * seg:b * seq + (s + 1) * seg, lanes] = fwd + bwd


def _rglru(layer, xr, row_start, nseq, seq, conv_w, conv_b, w_gate, b_gate, lam, h0=None, emit_last=False):
    bt = 4
    dg = RNN_GROUP
    ng = D_RNN // dg
    rows = bt * seq
    assert nseq % bt == 0 and row_start % rows == 0
    off = row_start // rows
    in_specs = [
        pl.BlockSpec((rows, dg), lambda bi, g: (off + bi, g)),
        pl.BlockSpec((None, CONV_W, dg), lambda bi, g: (layer, 0, g)),
        pl.BlockSpec((None, 1, dg), lambda bi, g: (layer, 0, g)),
        pl.BlockSpec((None, None, dg, 4 * dg), lambda bi, g: (layer, g, 0, 0)),
        pl.BlockSpec((None, None, 1, 4 * dg), lambda bi, g: (layer, g, 0, 0)),
        pl.BlockSpec((None, 2, dg), lambda bi, g: (layer, 0, g)),
    ]
    args = [xr, conv_w, conv_b, w_gate, b_gate, lam]
    if h0 is not None:
        in_specs.append(pl.BlockSpec((bt, None, 2, dg), lambda bi, g: (bi, layer, 0, g)))
        args.append(h0)
    out_specs = [pl.BlockSpec((rows, dg), lambda bi, g: (bi, g))]
    out_shape = [jax.ShapeDtypeStruct((nseq * seq, D_RNN), F32)]
    if emit_last:
        out_specs.append(pl.BlockSpec((bt, 2, dg), lambda bi, g: (bi, 0, g)))
        out_shape.append(jax.ShapeDtypeStruct((nseq, 2, D_RNN), F32))
    return pl.pallas_call(
        functools.partial(_rglru_kernel, seq=seq, bt=bt, has_h0=h0 is not None, emit_last=emit_last),
        grid=(nseq // bt, ng),
        in_specs=in_specs,
        out_specs=out_specs,
        out_shape=out_shape,
        scratch_shapes=[pltpu.VMEM((bt * (dg // LANE), SUBLANE * _segment_pitch(seq // SUBLANE), LANE), F32)
                        for _ in range(4)],
        compiler_params=_params("arbitrary", "arbitrary"),
        name=f"rglru_{seq}",
    )(*args)


def _attn_kernel(*refs, past, tq, rope):
    it = iter(refs)
    cq_ref, ckv_ref, kr_ref = (next(it) for _ in range(3))
    if rope:
        krr_ref, pckv_ref, pkr_ref, cos_ref, sin_ref = (next(it) for _ in range(5))
    wq_ref = next(it)
    wqr_ref = next(it) if rope else None
    wk_ref, wv_ref = next(it), next(it)
    o_ref = next(it)
    k_scr, v_scr = next(it), next(it)
    qi = pl.program_id(1)
    scale = 1.0 / math.sqrt(QK_NOPE + QK_ROPE)

    def fill(rows, ckv, k_rope):
        c = ckv.astype(BF16)
        k = _dot(c, wk_ref[...]) + jnp.tile(k_rope, (1, N_HEADS))
        k_scr[rows, :] = k.astype(BF16)
        v_scr[rows, :] = _dot(c, wv_ref[...]).astype(BF16)

    @pl.when(qi == 0)
    def _():
        seq = ckv_ref.shape[0]
        if rope:
            fill(pl.ds(0, past), pckv_ref[...], pkr_ref[...])
            fill(pl.ds(past, seq), ckv_ref[...],
                 kr_ref[...] * cos_ref[...] + krr_ref[...] * sin_ref[...])
        else:
            fill(pl.ds(0, seq), ckv_ref[...], kr_ref[...])

    cq = cq_ref[...]
    qa = _dot(cq, wq_ref[...])
    if rope:
        qb = _dot(cq, wqr_ref[...])
        q0 = pl.multiple_of(qi * tq, tq)
        cos_t = cos_ref[pl.ds(q0, tq), :]
        sin_t = sin_ref[pl.ds(q0, tq), :]
    for j in range(N_HEADS // 2):
        pair = None
        for h in (2 * j, 2 * j + 1):
            cols = slice(h * HEAD_PAD, (h + 1) * HEAD_PAD)
            qh = qa[:, cols]
            if rope:
                qh = qh * cos_t + qb[:, cols] * sin_t
            s = lax.dot_general(qh.astype(BF16), k_scr[:, cols], (((1,), (1,)), ((), ())),
                                preferred_element_type=F32)
            e = jnp.exp((s - jnp.max(s, axis=-1, keepdims=True)) * scale)
            denom = jnp.sum(e, axis=-1, keepdims=True)
            o = _dot(e.astype(BF16), v_scr[:, cols]) * (1.0 / denom)
            pair = o if pair is None else pair + o
        o_ref[:, j * LANE:(j + 1) * LANE] = pair.astype(BF16)


def _attention(layer, cq, ckv, kr, row_start, nseq, seq, w_q, w_k, w_v, lat=None):
    rope = lat is not None
    tq = min(_ATT_Q_TILE, seq)
    assert seq % tq == 0 and row_start % seq == 0
    past = lat[1].shape[2] if rope else 0
    total = past + seq
    hp = N_HEADS * HEAD_PAD
    nq = seq // tq
    s_off, q_off = row_start // seq, row_start // tq
    full = lambda w: pl.BlockSpec((seq, w), lambda b, q: (s_off + b, 0))
    q_rows = lambda w: pl.BlockSpec((tq, w), lambda b, q: (q_off + b * nq + q, 0))
    in_specs = [q_rows(Q_LORA), full(KV_LORA), full(HEAD_PAD)]
    args = [cq, ckv, kr]
    if rope:
        krr, pckv, pkr, cos, sin, w_q_rot = lat
        in_specs += [full(HEAD_PAD),
                     pl.BlockSpec((None, None, past, KV_LORA), lambda b, q: (b, layer, 0, 0)),
                     pl.BlockSpec((None, None, past, HEAD_PAD), lambda b, q: (b, layer, 0, 0)),
                     _fixed_spec((seq, HEAD_PAD)), _fixed_spec((seq, HEAD_PAD))]
        args += [krr, pckv, pkr, cos, sin]
    in_specs.append(_fixed_spec((Q_LORA, hp), (layer,)))
    args.append(w_q)
    if rope:
        in_specs.append(_fixed_spec((Q_LORA, hp), (layer,)))
        args.append(w_q_rot)
    in_specs += [_fixed_spec((KV_LORA, hp), (layer,)), _fixed_spec((KV_LORA, hp), (layer,))]
    args += [w_k, w_v]
    return pl.pallas_call(
        functools.partial(_attn_kernel, past=past, tq=tq, rope=rope),
        grid=(nseq, nq),
        in_specs=in_specs,
        out_specs=pl.BlockSpec((tq, N_HEADS * V_HEAD), lambda b, q: (b * nq + q, 0)),
        out_shape=jax.ShapeDtypeStruct((nseq * seq, N_HEADS * V_HEAD), BF16),
        scratch_shapes=[pltpu.VMEM((total, hp), BF16), pltpu.VMEM((total, hp), BF16)],
        compiler_params=_params("arbitrary", "arbitrary"),
        name=f"attn_{seq}",
    )(*args)


def _mix_kernel(*refs, n_x, n_ctx_tiles):
    (hsc_ref, hsl_ref, atc_ref, atl_ref, m_ref, wygg_ref, wro_ref, wao_ref, wo_ref,
     g_ref, b_ref, o_ref, wro_bf, wao_bf, wo_bf) = refs[n_x:]
    _cast_once([(wro_ref, wro_bf), (wao_ref, wao_bf), (wo_ref, wo_bf)])
    is_ctx = pl.program_id(0) < n_ctx_tiles
    hs = jnp.where(is_ctx, hsc_ref[...], hsl_ref[...])
    at = jnp.where(is_ctx, atc_ref[...], atl_ref[...])
    x = _load_stream(refs[:n_x], n_ctx_tiles)
    h = (x * (1.0 + m_ref[1:2, :]) + m_ref[0:1, :]).astype(BF16)
    ygg = _dot(h, wygg_ref[...])
    u = (hs * jax.nn.gelu(ygg[:, :D_RNN])).astype(BF16)
    rnn_out = _dot(u, wro_bf[...])
    att_out = _dot(at, wao_bf[...])
    merged = (jax.nn.sigmoid(ygg[:, D_RNN:D_RNN + D_MODEL]) * rnn_out
              + jax.nn.sigmoid(ygg[:, D_RNN + D_MODEL:]) * att_out)
    m = _dot(merged.astype(BF16), wo_bf[...])
    o_ref[...] = _layer_norm(ALPHA * x + m_ref[2:3, :] * m, g_ref[...], b_ref[...])


def _mix(layer, x, hs_ctx, hs_lat, at_ctx, at_lat, mods, w_ygg, w_ro, w_ao, w_o, ln_g, ln_b, n_ctx_rows,
         rows_per_latent):
    tm = _ROW_TILE
    nct = n_ctx_rows // tm
    x_specs, xs = _stream_specs(x, tm, n_ctx_rows)
    n_tok = sum(a.shape[0] for a in xs)
    row = pl.BlockSpec((tm, D_MODEL), lambda i: (i, 0))
    ctx_row = pl.BlockSpec((tm, D_MODEL), lambda i: (jnp.minimum(i, nct - 1), 0))
    lat_row = pl.BlockSpec((tm, D_MODEL), lambda i: (jnp.maximum(i - nct, 0), 0))
    sq = (D_MODEL, D_MODEL)
    return pl.pallas_call(
        functools.partial(_mix_kernel, n_x=len(xs), n_ctx_tiles=nct),
        grid=(n_tok // tm,),
        in_specs=x_specs + [ctx_row, lat_row, ctx_row, lat_row,
                  _mod_spec(layer, tm, n_ctx_rows, rows_per_latent),
                  _fixed_spec((D_MODEL, D_RNN + 2 * D_MODEL), (layer,)),
                  _fixed_spec(sq, (layer,)), _fixed_spec(sq, (layer,)), _fixed_spec(sq, (layer,)),
                  _fixed_spec((1, D_MODEL), (layer,)), _fixed_spec((1, D_MODEL), (layer,))],
        out_specs=row,
        out_shape=jax.ShapeDtypeStruct((n_tok, D_MODEL), F32),
        scratch_shapes=[pltpu.VMEM(sq, BF16) for _ in range(3)],
        compiler_params=_params("arbitrary"),
        name="mix_out",
    )(*xs, hs_ctx, hs_lat, at_ctx, at_lat, mods, w_ygg, w_ro, w_ao, w_o, ln_g, ln_b)


def _ffn_kernel(x_ref, m_ref, w1_ref, w3_ref, w2_ref, g_ref, b_ref, o_ref):
    x = x_ref[...]
    h = (x * (1.0 + m_ref[4:5, :]) + m_ref[3:4, :]).astype(BF16)
    f = None
    for c in range(D_FF // _FF_CHUNK):
        cols = slice(c * _FF_CHUNK, (c + 1) * _FF_CHUNK)
        act = (jax.nn.silu(_dot(h, w1_ref[:, cols])) * _dot(h, w3_ref[:, cols])).astype(BF16)
        y = _dot(act, w2_ref[cols, :])
        f = y if f is None else f + y
    o_ref[...] = _layer_norm(ALPHA * x + m_ref[5:6, :] * f, g_ref[...], b_ref[...])


def _ffn_dense(layer, x, mods, w1, w3, w2, ln_g, ln_b, n_ctx_rows, rows_per_latent):
    n_tok = x.shape[0]
    tm = _ROW_TILE
    j = layer // 2
    row = pl.BlockSpec((tm, D_MODEL), lambda i: (i, 0))
    return pl.pallas_call(
        _ffn_kernel,
        grid=(n_tok // tm,),
        in_specs=[row, _mod_spec(layer, tm, n_ctx_rows, rows_per_latent),
                  _fixed_spec((D_MODEL, D_FF), (j,)), _fixed_spec((D_MODEL, D_FF), (j,)),
                  _fixed_spec((D_FF, D_MODEL), (j,)),
                  _fixed_spec((1, D_MODEL), (layer,)), _fixed_spec((1, D_MODEL), (layer,))],
        out_specs=row,
        out_shape=jax.ShapeDtypeStruct((n_tok, D_MODEL), F32),
        compiler_params=_params("arbitrary"),
        name="ffn_dense",
    )(x, mods, w1, w3, w2, ln_g, ln_b)


def _split_bf16(a):
    hi = a.astype(BF16)
    return hi, (a - hi.astype(F32)).astype(BF16)


def _route_kernel(x_ref, m_ref, wr_ref, br_ref, h_ref, ids_ref, gw_ref, cnt_ref, run_scr, tri_scr):
    tm = x_ref.shape[0]

    @pl.when(pl.program_id(0) == 0)
    def _():
        run_scr[...] = jnp.zeros_like(run_scr)
        r = lax.broadcasted_iota(jnp.int32, (tm, tm), 0)
        c = lax.broadcasted_iota(jnp.int32, (tm, tm), 1)
        tri_scr[...] = jnp.where(c < r, 1.0, 0.0).astype(BF16)

    h = x_ref[...] * (1.0 + m_ref[4:5, :]) + m_ref[3:4, :]
    h_ref[...] = h
    h_hi, h_lo = _split_bf16(h)

    w_hi, w_lo = _split_bf16(wr_ref[...])
    logits = _dot(h_hi, w_hi) + (_dot(h_hi, w_lo) + _dot(h_lo, w_hi)) + br_ref[...]
    lane = lax.broadcasted_iota(jnp.int32, logits.shape, 1)
    big = jnp.int32(N_EXPERTS)
    v1 = jnp.max(logits, axis=-1, keepdims=True)
    i1 = jnp.min(jnp.where(logits == v1, lane, big), axis=-1, keepdims=True)
    rest = jnp.where(lane == i1, -jnp.inf, logits)
    v2 = jnp.max(rest, axis=-1, keepdims=True)
    i2 = jnp.min(jnp.where(rest == v2, lane, big), axis=-1, keepdims=True)
    e2 = jnp.exp(v2 - v1)
    g1 = 1.0 / (1.0 + e2)
    g2 = e2 / (1.0 + e2)

    m1, m2 = lane == i1, lane == i2
    member = jnp.where(m1, 1.0, 0.0) + jnp.where(m2, 1.0, 0.0)
    before = _dot(tri_scr[...], member.astype(BF16)) + run_scr[...]
    r1 = jnp.sum(jnp.where(m1, before, 0.0), axis=-1, keepdims=True).astype(jnp.int32)
    r2 = jnp.sum(jnp.where(m2, before, 0.0), axis=-1, keepdims=True).astype(jnp.int32)
    run_scr[...] += jnp.sum(member, axis=0, keepdims=True)

    ids_ref[...] = jnp.where(lane == 0, i1, jnp.where(lane == 1, i2, jnp.where(lane == 2, r1, r2)))
    gw_ref[...] = jnp.where(lane == 0, g1, g2)
    cnt_ref[...] = run_scr[...].astype(jnp.int32)


def _moe_route(layer, x, mods, w_router, b_router, n_ctx_rows, rows_per_latent):
    n_tok = x.shape[0]
    tm = _ROW_TILE
    j = layer // 2
    row = lambda w: pl.BlockSpec((tm, w), lambda i: (i, 0))
    return pl.pallas_call(
        _route_kernel,
        grid=(n_tok // tm,),
        in_specs=[row(D_MODEL), _mod_spec(layer, tm, n_ctx_rows, rows_per_latent),
                  _fixed_spec((D_MODEL, N_EXPERTS), (j,)), _fixed_spec((1, N_EXPERTS), (j,))],
        out_specs=[row(D_MODEL), row(N_EXPERTS), row(N_EXPERTS),
                   pl.BlockSpec((1, N_EXPERTS), lambda i: (0, 0))],
        out_shape=[jax.ShapeDtypeStruct((n_tok, D_MODEL), F32),
                   jax.ShapeDtypeStruct((n_tok, N_EXPERTS), jnp.int32),
                   jax.ShapeDtypeStruct((n_tok, N_EXPERTS), F32),
                   jax.ShapeDtypeStruct((1, N_EXPERTS), jnp.int32)],
        scratch_shapes=[pltpu.VMEM((1, N_EXPERTS), F32), pltpu.VMEM((tm, tm), BF16)],
        compiler_params=_params("arbitrary"),
        name="moe_route",
    )(x, mods, w_router, b_router)


def _tables_kernel(er_ref, g_ref, cnt_ref, exp_ref, on_ref, tok_ref, gate_ref, start_scr, *, n_tok, tm):
    n_tiles = exp_ref.shape[0]
    ends = []
    total = jnp.int32(0)
    for e in range(N_EXPERTS):
        start_scr[e] = total
        total = total + ((cnt_ref[0, e] + (tm - 1)) // tm) * tm
        ends.append(total)

    tok_ref[...] = jnp.full(tok_ref.shape, n_tok, jnp.int32)
    gate_ref[...] = jnp.zeros(gate_ref.shape, F32)

    def place(t, carry):
        for k in range(TOP_K):
            slot = start_scr[er_ref[2 * TOP_K * t + k]] + er_ref[2 * TOP_K * t + TOP_K + k]
            tok_ref[pl.ds(slot, 1), :] = jnp.full((1, LANE), t, jnp.int32)
            gate_ref[pl.ds(slot, 1), :] = jnp.full((1, LANE), g_ref[TOP_K * t + k], F32)
        return carry

    lax.fori_loop(0, n_tok, place, 0, unroll=8)

    def tiles(i, carry):
        first = i * tm
        owner = jnp.int32(0)
        for e in range(N_EXPERTS):
            owner = owner + jnp.where(first >= ends[e], 1, 0)
        exp_ref[i] = jnp.minimum(owner, N_EXPERTS - 1)
        on_ref[i] = jnp.where(first < total, 1, 0)
        return carry

    lax.fori_loop(0, n_tiles, tiles, 0)


def _slot_tables(ids, gw, counts, n_tok):
    tm = _MOE_SLOT_TILE
    n_slots = TOP_K * n_tok + N_EXPERTS * tm
    n_tiles = n_slots // tm
    smem = pl.BlockSpec(memory_space=pltpu.SMEM)
    vmem = pl.BlockSpec(memory_space=pltpu.VMEM)
    tile_exp, tile_on, tok_rows, gate_rows = pl.pallas_call(
        functools.partial(_tables_kernel, n_tok=n_tok, tm=tm),
        in_specs=[smem, smem, smem],
        out_specs=[smem, smem, vmem, vmem],
        out_shape=[jax.ShapeDtypeStruct((n_tiles,), jnp.int32), jax.ShapeDtypeStruct((n_tiles,), jnp.int32),
                   jax.ShapeDtypeStruct((n_slots, LANE), jnp.int32), jax.ShapeDtypeStruct((n_slots, LANE), F32)],
        scratch_shapes=[pltpu.SMEM((N_EXPERTS,), jnp.int32)],
        compiler_params=pltpu.CompilerParams(vmem_limit_bytes=VMEM_LIMIT),
        name="moe_tables",
    )(ids[:, :2 * TOP_K].reshape(-1), gw[:, :TOP_K].reshape(-1), counts)
    return tile_exp, tile_on, tok_rows[:, 0], gate_rows


def _moe_up_kernel(exp_ref, on_ref, tok_ref, h_ref, w1_ref, w3_ref, act_ref, xg0_scr, xg1_scr, *,
                   n_tok, n_tiles):
    t = pl.program_id(0)
    tm = xg0_scr.shape[0]

    def gather_row(tile, r, dst):
        src = jnp.minimum(tok_ref[tile * tm + r], n_tok - 1)
        dst[pl.ds(r, 1), :] = h_ref[pl.ds(src, 1), :]

    @pl.when(t == 0)
    def _():
        def body(r, carry):
            gather_row(0, r, xg0_scr)
            return carry

        lax.fori_loop(0, tm, body, 0, unroll=8)

    for parity, (cur, nxt) in enumerate(((xg0_scr, xg1_scr), (xg1_scr, xg0_scr))):
        @pl.when(jnp.logical_and(on_ref[t] == 1, t % 2 == parity))
        def _():
            h = cur[...].astype(BF16)
            nxt_tile = jnp.minimum(t + 1, n_tiles - 1)
            for r in range(tm):
                gather_row(nxt_tile, r, nxt)
            act_ref[...] = (jax.nn.silu(_dot(h, w1_ref[...])) * _dot(h, w3_ref[...])).astype(BF16)

    @pl.when(on_ref[t] == 0)
    def _():
        act_ref[...] = jnp.zeros_like(act_ref)


def _moe_up(layer, tables, h, w1, w3):
    tile_exp, tile_on, slot_tok, _ = tables
    n_tok = h.shape[0]
    tm = _MOE_SLOT_TILE
    n_tiles = tile_exp.shape[0]
    j = layer // 2
    wspec = pl.BlockSpec((None, None, D_MODEL, D_EXPERT), lambda t, ex, on, tk: (j, ex[t], 0, 0),
                         pipeline_mode=pl.Buffered(1))
    return pl.pallas_call(
        functools.partial(_moe_up_kernel, n_tok=n_tok, n_tiles=n_tiles),
        grid_spec=pltpu.PrefetchScalarGridSpec(
            num_scalar_prefetch=3,
            grid=(n_tiles,),
            in_specs=[pl.BlockSpec((n_tok, D_MODEL), lambda t, ex, on, tk: (0, 0),
                                   pipeline_mode=pl.Buffered(1)),
                      wspec, wspec],
            out_specs=pl.BlockSpec((tm, D_EXPERT), lambda t, ex, on, tk: (t, 0)),
            scratch_shapes=[pltpu.VMEM((tm, D_MODEL), F32), pltpu.VMEM((tm, D_MODEL), F32)],
        ),
        out_shape=jax.ShapeDtypeStruct((n_tiles * tm, D_EXPERT), BF16),
        compiler_params=_params("arbitrary"),
        name="moe_up",
    )(tile_exp, tile_on, slot_tok, h, w1, w3)


def _moe_down_kernel(exp_ref, on_ref, tok_ref, act_ref, gate_ref, w2_ref, x_ref, m_ref, g_ref, b_ref,
                     *rest, n_tiles, ln_rows, n_ctx_ln_tiles, split):
    o_refs, (acc_scr, y0_scr, y1_scr) = rest[:-3], rest[-3:]
    t = pl.program_id(0)
    tm = y0_scr.shape[0]

    @pl.when(t == 0)
    def _():
        acc_scr[...] = jnp.zeros_like(acc_scr)

    def project(dst):
        dst[...] = _dot(act_ref[...], w2_ref[...]) * gate_ref[:, 0:1]

    def scatter_group(tile, base, ys):
        toks = [tok_ref[tile * tm + base + u] for u in range(SUBLANE)]
        rows = [acc_scr[pl.ds(toks[u], 1), :] for u in range(SUBLANE)]
        for u in range(SUBLANE):
            acc_scr[pl.ds(toks[u], 1), :] = rows[u] + ys[u:u + 1, :]

    last = n_tiles - 1
    project_now = jnp.logical_and(t < n_tiles, on_ref[jnp.minimum(t, last)] == 1)
    add_prev = jnp.logical_and(jnp.logical_and(t >= 1, t <= n_tiles), on_ref[jnp.clip(t - 1, 0, last)] == 1)
    for parity, (cur, prv) in enumerate(((y0_scr, y1_scr), (y1_scr, y0_scr))):
        mine = t % 2 == parity

        @pl.when(jnp.logical_and(mine, jnp.logical_and(project_now, add_prev)))
        def _():
            project(cur)
            for g in range(tm // SUBLANE):
                scatter_group(t - 1, g * SUBLANE, prv[g * SUBLANE:(g + 1) * SUBLANE, :])

        @pl.when(jnp.logical_and(mine, jnp.logical_and(project_now, jnp.logical_not(add_prev))))
        def _():
            project(cur)

        @pl.when(jnp.logical_and(mine, jnp.logical_and(add_prev, jnp.logical_not(project_now))))
        def _():
            def body(g, carry):
                base = pl.multiple_of(g * SUBLANE, SUBLANE)
                scatter_group(t - 1, base, prv[pl.ds(base, SUBLANE), :])
                return carry

            lax.fori_loop(0, tm // SUBLANE, body, 0)

    @pl.when(t > n_tiles)
    def _():
        r0 = pl.multiple_of((t - n_tiles - 1) * ln_rows, ln_rows)
        f = acc_scr[pl.ds(r0, ln_rows), :]
        out = _layer_norm(ALPHA * x_ref[...] + m_ref[5:6, :] * f, g_ref[...], b_ref[...])
        if split:
            is_ctx = t - n_tiles - 1 < n_ctx_ln_tiles

            @pl.when(is_ctx)
            def _():
                o_refs[0][...] = out

            @pl.when(jnp.logical_not(is_ctx))
            def _():
                o_refs[1][...] = out
        else:
            o_refs[0][...] = out


def _moe_down(layer, tables, act, x, mods, w2, ln_g, ln_b, n_ctx_rows, rows_per_latent, split):
    tile_exp, tile_on, slot_tok, slot_gate = tables
    n_tok = x.shape[0]
    tm = _MOE_SLOT_TILE
    tl = _MOE_LN_TILE
    n_tiles = tile_exp.shape[0]
    j = layer // 2
    slot = lambda t: jnp.minimum(t, n_tiles - 1)
    ln = lambda t: jnp.maximum(t - n_tiles - 1, 0)
    n_ctx_tiles, per = n_ctx_rows // tl, rows_per_latent // tl

    def mod_index(t, *_):
        i = ln(t)
        return (layer, jnp.where(i < n_ctx_tiles, 0, 1 + (i - n_ctx_tiles) // per), 0, 0)

    fixed = lambda shape, lead: pl.BlockSpec((None,) + shape, lambda *_: (lead,) + (0,) * len(shape),
                                             pipeline_mode=pl.Buffered(1))
    if split:
        out_specs = [pl.BlockSpec((tl, D_MODEL), lambda t, *_: (jnp.minimum(ln(t), n_ctx_tiles - 1), 0)),
                     pl.BlockSpec((tl, D_MODEL), lambda t, *_: (jnp.maximum(ln(t) - n_ctx_tiles, 0), 0))]
        out_shape = [jax.ShapeDtypeStruct((n_ctx_rows, D_MODEL), F32),
                     jax.ShapeDtypeStruct((n_tok - n_ctx_rows, D_MODEL), F32)]
    else:
        out_specs = pl.BlockSpec((tl, D_MODEL), lambda t, *_: (ln(t), 0))
        out_shape = jax.ShapeDtypeStruct((n_tok, D_MODEL), F32)
    return pl.pallas_call(
        functools.partial(_moe_down_kernel, n_tiles=n_tiles, ln_rows=tl, n_ctx_ln_tiles=n_ctx_tiles,
                          split=split),
        grid_spec=pltpu.PrefetchScalarGridSpec(
            num_scalar_prefetch=3,
            grid=(n_tiles + 1 + n_tok // tl,),
            in_specs=[pl.BlockSpec((tm, D_EXPERT), lambda t, ex, *_: (slot(t), 0)),
                      pl.BlockSpec((tm, LANE), lambda t, ex, *_: (slot(t), 0)),
                      pl.BlockSpec((None, None, D_EXPERT, D_MODEL), lambda t, ex, *_: (j, ex[slot(t)], 0, 0)),
                      pl.BlockSpec((tl, D_MODEL), lambda t, *_: (ln(t), 0)),
                      pl.BlockSpec((None, None, 6, D_MODEL), mod_index),
                      fixed((1, D_MODEL), layer), fixed((1, D_MODEL), layer)],
            out_specs=out_specs,
            scratch_shapes=[pltpu.VMEM((n_tok + SUBLANE, D_MODEL), F32), pltpu.VMEM((tm, D_MODEL), F32),
                            pltpu.VMEM((tm, D_MODEL), F32)],
        ),
        out_shape=out_shape,
        compiler_params=_params("arbitrary"),
        name="moe_down",
    )(tile_exp, tile_on, slot_tok, act, slot_gate, w2, x, mods, ln_g, ln_b)


def _ffn_moe(layer, x, mods, w_router, b_router, w1, w3, w2, ln_g, ln_b, n_ctx_rows, rows_per_latent,
             split=False):
    h, ids, gw, counts = _moe_route(layer, x, mods, w_router, b_router, n_ctx_rows, rows_per_latent)
    tables = _slot_tables(ids, gw, counts, x.shape[0])
    act = _moe_up(layer, tables, h, w1, w3)
    return _moe_down(layer, tables, act, x, mods, w2, ln_g, ln_b, n_ctx_rows, rows_per_latent, split)


def _rot_cols(w):
    ws = w.reshape(w.shape[:-1] + (2, 2, ROPE_AXIS // 2))
    return jnp.stack([-ws[..., 1, :], ws[..., 0, :]], axis=-2).reshape(w.shape)


def _pad_rope_cols(w):
    return jnp.pad(w, ((0, 0),) * (w.ndim - 1) + ((ROPE_OFF, HEAD_PAD - ROPE_OFF - QK_ROPE),))


def _rope_tables(rows):
    r = jnp.repeat(jnp.arange(rows, dtype=F32), GRID_W)
    col = jnp.tile(jnp.arange(GRID_W, dtype=F32), rows)
    freqs = ROPE_THETA ** (-jnp.arange(0, ROPE_AXIS, 2, dtype=F32) / ROPE_AXIS)
    ar = r[:, None] * freqs
    ac = col[:, None] * freqs
    ang = jnp.concatenate([ar, ar, ac, ac], axis=-1)
    cos, sin = jnp.cos(ang), jnp.sin(ang)
    n = cos.shape[0]
    cos_p = jnp.concatenate([jnp.ones((n, ROPE_OFF), F32), cos,
                             jnp.zeros((n, HEAD_PAD - ROPE_OFF - QK_ROPE), F32)], axis=-1)
    return cos_p, _pad_rope_cols(sin)


def _prepared_weights(w_in, rg_wa, rg_ba, rg_wx, rg_bx, w_qb, w_kvb):
    bf = lambda a: a.astype(BF16)
    w_kr = w_in[:, :, _COL_KR:_COL_KR + QK_ROPE]
    w_s = jnp.concatenate([w_in[:, :, _COL_Q:_COL_KR], _pad_rope_cols(w_kr),
                           _pad_rope_cols(_rot_cols(w_kr))], axis=-1)
    w_x = w_in[:, :, :D_RNN]
    w_ygg = jnp.concatenate([w_in[:, :, D_RNN:2 * D_RNN], w_in[:, :, _COL_GATE:]], axis=-1)

    wq = w_qb.reshape(DEPTH, Q_LORA, N_HEADS, QK_NOPE + QK_ROPE)
    q_nope, q_rope = wq[..., :QK_NOPE], wq[..., QK_NOPE:]
    tail = jnp.zeros((DEPTH, Q_LORA, N_HEADS, HEAD_PAD - QK_NOPE - QK_ROPE), F32)
    hp = N_HEADS * HEAD_PAD
    w_q = jnp.concatenate([q_nope, q_rope, tail], axis=-1).reshape(DEPTH, Q_LORA, hp)
    w_q_rot = jnp.concatenate([jnp.zeros_like(q_nope), _rot_cols(q_rope), tail], axis=-1)
    w_q_rot = w_q_rot.reshape(DEPTH, Q_LORA, hp)
    wkv = w_kvb.reshape(DEPTH, KV_LORA, N_HEADS, QK_NOPE + V_HEAD)
    k_nope, v = wkv[..., :QK_NOPE], wkv[..., QK_NOPE:]
    w_k = jnp.concatenate([k_nope, jnp.zeros((DEPTH, KV_LORA, N_HEADS, HEAD_PAD - QK_NOPE), F32)], axis=-1)
    w_k = w_k.reshape(DEPTH, KV_LORA, hp)
    vz = jnp.zeros_like(v)
    odd = (jnp.arange(N_HEADS) % 2 == 1)[None, None, :, None]
    w_v = jnp.where(odd, jnp.concatenate([vz, v], axis=-1), jnp.concatenate([v, vz], axis=-1))
    w_v = w_v.reshape(DEPTH, KV_LORA, hp)

    nblk = RNN_GROUP // RNN_BW
    ng = D_RNN // RNN_GROUP
    eye = jnp.eye(nblk, dtype=F32)

    def block_diag(w):
        wg = w.reshape(DEPTH, ng, nblk, RNN_BW, RNN_BW)
        return jnp.einsum('lgnjk,nm->lgnjmk', wg, eye).reshape(DEPTH, ng, RNN_GROUP, RNN_GROUP)

    w_gate = jnp.concatenate([block_diag(rg_wa[:, 0]), block_diag(rg_wx[:, 0]),
                              block_diag(rg_wa[:, 1]), block_diag(rg_wx[:, 1])], axis=-1)
    grp = lambda b: b.reshape(DEPTH, ng, 1, RNN_GROUP)
    b_gate = jnp.concatenate([grp(rg_ba[:, 0]), grp(rg_bx[:, 0]), grp(rg_ba[:, 1]), grp(rg_bx[:, 1])],
                             axis=-1)
    return dict(w_x=bf(w_x), w_s=bf(w_s), w_ygg=bf(w_ygg), w_q=bf(w_q), w_q_rot=bf(w_q_rot), w_k=bf(w_k), w_v=bf(w_v),
                w_gate=bf(w_gate), b_gate=b_gate)


def kernel(x_prompt, x_sample, cache_ckv, cache_krope, state_rnn, c, c_ctx, w_ada, b_ada, w_in, conv_w, conv_b, rg_wa, rg_ba, rg_wx, rg_bx, rg_lambda, w_rnn_out, q_norm_g, w_qb, kv_norm_g, w_kvb, w_attn_out, w_out, ln1_g, ln1_b, ln2_g, ln2_b, w1_dense, w3_dense, w2_dense, w_router, b_router, w1_exp, w3_exp, w2_exp):
    nb, seq, _ = x_prompt.shape
    db, dseq, _ = x_sample.shape
    n_ctx = nb * seq
    n_lat = db * dseq
    assert n_ctx % _ROW_TILE == 0 and dseq % _ROW_TILE == 0 and db + 1 <= 8
    assert DEPTH % 2 == 0
    bf = lambda a: a.astype(BF16)
    vec = lambda a: a.reshape(a.shape[0], 1, a.shape[-1])

    cond8 = jnp.concatenate([c_ctx[None, :], c, jnp.zeros((8 - 1 - db, D_MODEL), F32)], axis=0)
    mods = _ada_mods(cond8, w_ada, b_ada)
    cos_p, sin_p = _rope_tables(dseq // GRID_W)
    past_kr = _pad_rope_cols(cache_krope)
    pw = _prepared_weights(w_in, rg_wa, rg_ba, rg_wx, rg_bx, w_qb, w_kvb)
    w1d, w3d, w2d = bf(w1_dense), bf(w3_dense), bf(w2_dense)
    w1e, w3e, w2e = bf(w1_exp), bf(w3_exp), bf(w2_exp)
    qg, kvg, cb = vec(q_norm_g), vec(kv_norm_g), vec(conv_b)
    l1g, l1b, l2g, l2b = vec(ln1_g), vec(ln1_b), vec(ln2_g), vec(ln2_b)
    br = vec(b_router)
    tiles = (n_ctx, dseq)

    x = (x_prompt.reshape(n_ctx, D_MODEL), x_sample.reshape(n_lat, D_MODEL))
    new_ckv, new_kr, new_h = [], [], []
    for l in range(DEPTH):
        xr, cq, ckv, kr, krr = _in_proj(l, x, mods, pw['w_x'], pw['w_s'], qg, kvg, *tiles)
        new_ckv.append(ckv[:n_ctx].reshape(nb, seq, KV_LORA))
        new_kr.append(kr[:n_ctx, ROPE_OFF:ROPE_OFF + QK_ROPE].reshape(nb, seq, QK_ROPE))

        rnn_args = (conv_w, cb, pw['w_gate'], pw['b_gate'], rg_lambda)
        hs_ctx, h_last = _rglru(l, xr, 0, nb, seq, *rnn_args, emit_last=True)
        (hs_lat,) = _rglru(l, xr, n_ctx, db, dseq, *rnn_args, h0=state_rnn)
        new_h.append(h_last)

        att_w = (pw['w_q'], pw['w_k'], pw['w_v'])
        at_ctx = _attention(l, cq, ckv, kr, 0, nb, seq, *att_w)
        at_lat = _attention(l, cq, ckv, kr, n_ctx, db, dseq, *att_w,
                            lat=(krr, cache_ckv, past_kr, cos_p, sin_p, pw['w_q_rot']))

        x = _mix(l, x, hs_ctx, hs_lat, at_ctx, at_lat, mods, pw['w_ygg'], w_rnn_out, w_attn_out, w_out,
                 l1g, l1b, *tiles)
        if l % 2 == 0:
            x = _ffn_dense(l, x, mods, w1d, w3d, w2d, l2g, l2b, *tiles)
        else:
            x = _ffn_moe(l, x, mods, w_router, br, w1e, w3e, w2e, l2g, l2b, *tiles, split=l == DEPTH - 1)

    return (x[0].reshape(nb, seq, D_MODEL), x[1].reshape(db, dseq, D_MODEL),
            jnp.stack(new_ckv, axis=1), jnp.stack(new_kr, axis=1), jnp.stack(new_h, axis=1))
```

```python
import functools
import math

import jax
import jax.numpy as jnp
from jax import lax
from jax.experimental import pallas as pl
from jax.experimental.pallas import tpu as pltpu

F32 = jnp.float32
BF16 = jnp.bfloat16

D_MODEL = 1024
DEPTH = 4
GRID_W = 64
D_RNN = D_MODEL
RNN_BLOCKS = 16
RNN_BW = D_RNN // RNN_BLOCKS
CONV_W = 4
RG_C = 8.0
N_HEADS = 16
QK_NOPE = 64
QK_ROPE = 32
V_HEAD = 64
Q_LORA = 256
KV_LORA = 128
ROPE_AXIS = QK_ROPE // 2
ROPE_THETA = 10000.0
D_FF = 2816
N_EXPERTS = 8
TOP_K = 2
D_EXPERT = 1408
ALPHA = (2 * DEPTH) ** 0.25
LN_EPS = 1e-5
RMS_EPS = 1e-6

LANE = 128
SUBLANE = 8
HEAD_PAD = LANE
ROPE_OFF = QK_NOPE
RNN_GROUP = 256
VMEM_LIMIT = 56 * 1024 * 1024

_ROW_TILE = 512
_MOE_SLOT_TILE = 512
_MOE_LN_TILE = 256
_ATT_Q_TILE = 512
_FF_CHUNK = 1408

_COL_Q = 2 * D_RNN
_COL_KR = _COL_Q + Q_LORA + KV_LORA
_COL_GATE = _COL_KR + QK_ROPE


def _dot(a, b):
    return jnp.dot(a, b, preferred_element_type=F32)


def _layer_norm(y, g, b):
    mu = jnp.mean(y, axis=-1, keepdims=True)
    d = y - mu
    var = jnp.mean(d * d, axis=-1, keepdims=True)
    return d * lax.rsqrt(var + LN_EPS) * g + b


def _rms_norm(y, g):
    return y * lax.rsqrt(jnp.mean(y * y, axis=-1, keepdims=True) + RMS_EPS) * g


def _params(*sem):
    return pltpu.CompilerParams(dimension_semantics=sem, vmem_limit_bytes=VMEM_LIMIT)


def _fixed_spec(shape, lead=(), tail=None):
    tail = (0,) * len(shape) if tail is None else tuple(tail)
    index = tuple(lead) + tail
    return pl.BlockSpec((None,) * len(lead) + tuple(shape), lambda *_: index,
                        pipeline_mode=pl.Buffered(1))


def _mod_spec(layer, tm, n_ctx_rows, rows_per_latent):
    n_ctx_tiles = n_ctx_rows // tm
    per = rows_per_latent // tm

    def index(i, *_):
        return (layer, jnp.where(i < n_ctx_tiles, 0, 1 + (i - n_ctx_tiles) // per), 0, 0)

    return pl.BlockSpec((None, None, 6, D_MODEL), index)


def _cast_once(pairs):
    @pl.when(pl.program_id(0) == 0)
    def _():
        for src, dst in pairs:
            dst[...] = src[...].astype(BF16)


def _stream_specs(x, tm, n_ctx_rows):
    if isinstance(x, tuple):
        nct = n_ctx_rows // tm
        return ([pl.BlockSpec((tm, D_MODEL), lambda i, *_: (jnp.minimum(i, nct - 1), 0)),
                 pl.BlockSpec((tm, D_MODEL), lambda i, *_: (jnp.maximum(i - nct, 0), 0))], list(x))
    return [pl.BlockSpec((tm, D_MODEL), lambda i, *_: (i, 0))], [x]


def _load_stream(refs, n_ctx_tiles):
    if len(refs) == 2:
        return jnp.where(pl.program_id(0) < n_ctx_tiles, refs[0][...], refs[1][...])
    return refs[0][...]


def _mods_kernel(c_ref, w_ref, b_ref, o_ref):
    s = jax.nn.silu(c_ref[...]).astype(BF16)
    o_ref[...] = _dot(s, w_ref[...].astype(BF16)) + b_ref[...]


def _ada_mods(cond8, w_ada, b_ada):
    out = pl.pallas_call(
        _mods_kernel,
        grid=(DEPTH, 6),
        in_specs=[
            pl.BlockSpec((8, D_MODEL), lambda l, j: (0, 0)),
            pl.BlockSpec((None, D_MODEL, D_MODEL), lambda l, j: (l, 0, j)),
            pl.BlockSpec((None, 1, D_MODEL), lambda l, j: (l, 0, j)),
        ],
        out_specs=pl.BlockSpec((None, 8, D_MODEL), lambda l, j: (l, 0, j)),
        out_shape=jax.ShapeDtypeStruct((DEPTH, 8, 6 * D_MODEL), F32),
        compiler_params=_params("arbitrary", "arbitrary"),
        name="ada_mods",
    )(cond8, w_ada, b_ada.reshape(DEPTH, 1, 6 * D_MODEL))
    return out.reshape(DEPTH, 8, 6, D_MODEL)


_S_COLS = Q_LORA + KV_LORA + 2 * HEAD_PAD


def _inproj_kernel(*refs, n_x, n_ctx_tiles):
    m_ref, wx_ref, ws_ref, qg_ref, kvg_ref, xr_ref, cq_ref, ckv_ref, kr_ref, krr_ref = refs[n_x:]
    x = _load_stream(refs[:n_x], n_ctx_tiles)
    h = (x * (1.0 + m_ref[1:2, :]) + m_ref[0:1, :]).astype(BF16)
    xr_ref[...] = _dot(h, wx_ref[...])
    s = _dot(h, ws_ref[...])
    cq_ref[...] = _rms_norm(s[:, :Q_LORA], qg_ref[...]).astype(BF16)
    ckv_ref[...] = _rms_norm(s[:, Q_LORA:Q_LORA + KV_LORA], kvg_ref[...])
    kr_ref[...] = s[:, Q_LORA + KV_LORA:Q_LORA + KV_LORA + HEAD_PAD]
    krr_ref[...] = s[:, Q_LORA + KV_LORA + HEAD_PAD:]


def _in_proj(layer, x, mods, w_x, w_s, q_g, kv_g, n_ctx_rows, rows_per_latent):
    tm = _ROW_TILE
    x_specs, xs = _stream_specs(x, tm, n_ctx_rows)
    n_tok = sum(a.shape[0] for a in xs)
    row = lambda w: pl.BlockSpec((tm, w), lambda i: (i, 0))
    return pl.pallas_call(
        functools.partial(_inproj_kernel, n_x=len(xs), n_ctx_tiles=n_ctx_rows // tm),
        grid=(n_tok // tm,),
        in_specs=x_specs + [
            _mod_spec(layer, tm, n_ctx_rows, rows_per_latent),
            _fixed_spec((D_MODEL, D_RNN), (layer,)),
            _fixed_spec((D_MODEL, _S_COLS), (layer,)),
            _fixed_spec((1, Q_LORA), (layer,)),
            _fixed_spec((1, KV_LORA), (layer,)),
        ],
        out_specs=[row(D_RNN), row(Q_LORA), row(KV_LORA), row(HEAD_PAD), row(HEAD_PAD)],
        out_shape=[
            jax.ShapeDtypeStruct((n_tok, D_RNN), F32),
            jax.ShapeDtypeStruct((n_tok, Q_LORA), BF16),
            jax.ShapeDtypeStruct((n_tok, KV_LORA), F32),
            jax.ShapeDtypeStruct((n_tok, HEAD_PAD), F32),
            jax.ShapeDtypeStruct((n_tok, HEAD_PAD), F32),
        ],
        compiler_params=_params("arbitrary"),
        name="in_proj",
    )(*xs, mods, w_x, w_s, q_g, kv_g)


def _segment_pitch(seg):
    assert seg % SUBLANE == 0
    return seg if (seg // SUBLANE) % 2 == 1 else seg + SUBLANE


def _rglru_kernel(*refs, seq, bt, has_h0, emit_last):
    it = iter(refs)
    x_ref, cw_ref, cb_ref, wg_ref, bg_ref, lam_ref = (next(it) for _ in range(6))
    h0_ref = next(it) if has_h0 else None
    hs_ref = next(it)
    hl_ref = next(it) if emit_last else None
    a_refs = (next(it), next(it))
    b_refs = (next(it), next(it))
    dg = RNN_GROUP
    nslab = dg // LANE
    seg = seq // SUBLANE
    pitch = _segment_pitch(seg)

    row = lax.broadcasted_iota(jnp.int32, (seq, dg), 0)
    decay_rate = [RG_C * jax.nn.softplus(-lam_ref[d:d + 1, :]) for d in range(2)]
    for b in range(bt):
        rows = slice(b * seq, (b + 1) * seq)
        x = x_ref[rows, :]
        xm1 = jnp.where(row >= 1, pltpu.roll(x, 1, 0), 0.0)
        xp1 = jnp.where(row < seq - 1, pltpu.roll(x, seq - 1, 0), 0.0)
        xp2 = jnp.where(row < seq - 2, pltpu.roll(x, seq - 2, 0), 0.0)
        xc = cb_ref[...] + xm1 * cw_ref[0:1, :]
        xc = xc + x * cw_ref[1:2, :]
        xc = xc + xp1 * cw_ref[2:3, :]
        xc = xc + xp2 * cw_ref[3:4, :]
        gates = _dot(xc.astype(BF16), wg_ref[...]) + bg_ref[...]
        for d in range(2):
            r = jax.nn.sigmoid(gates[:, (2 * d) * dg:(2 * d + 1) * dg])
            i = jax.nn.sigmoid(gates[:, (2 * d + 1) * dg:(2 * d + 2) * dg])
            neg_log_a = r * decay_rate[d]
            a = jnp.exp(-neg_log_a)
            y = jnp.tanh(neg_log_a) * (a * a + 1.0)
            bq = jnp.where(y > 0.0, y * lax.rsqrt(y), 0.0) * (i * xc)
            for s in range(SUBLANE):
                for c in range(nslab):
                    src = (slice(s * seg, (s + 1) * seg), slice(c * LANE, (c + 1) * LANE))
                    dst = slice(s * pitch, s * pitch + seg)
                    a_refs[d][b * nslab + c, dst, :] = a[src]
                    b_refs[d][b * nslab + c, dst, :] = bq[src]

    nch = bt * nslab
    zeros = tuple(jnp.zeros((SUBLANE, LANE), F32) for _ in range(nch))
    ones = tuple(jnp.ones((SUBLANE, LANE), F32) for _ in range(nch))

    def totals(j, carry):
        out = []
        for d, t in ((0, j), (1, seg - 1 - j)):
            h_prev, p_prev = carry[2 * d], carry[2 * d + 1]
            idx = pl.ds(t, SUBLANE, stride=pitch)
            a = [a_refs[d].at[ch][idx, :] for ch in range(nch)]
            out.append(tuple(a[ch] * h_prev[ch] + b_refs[d].at[ch][idx, :] for ch in range(nch)))
            out.append(tuple(a[ch] * p_prev[ch] for ch in range(nch)))
        return tuple(out)

    h_f, p_f, h_b, p_b = lax.fori_loop(0, seg, totals, (zeros, ones, zeros, ones), unroll=2)

    enter = ([], [])
    for b in range(bt):
        for c in range(nslab):
            ch = b * nslab + c
            lanes = slice(c * LANE, (c + 1) * LANE)
            row_of = lambda v, s: v[s:s + 1, :]
            if has_h0:
                enter_f, enter_b = h0_ref[b, 0:1, lanes], h0_ref[b, 1:2, lanes]
            else:
                enter_f = enter_b = jnp.zeros((1, LANE), F32)
            ent_f, ent_b = [None] * SUBLANE, [None] * SUBLANE
            for s in range(SUBLANE):
                ent_f[s] = enter_f
                enter_f = row_of(h_f[ch], s) + row_of(p_f[ch], s) * enter_f
            for s in reversed(range(SUBLANE)):
                ent_b[s] = enter_b
                enter_b = row_of(h_b[ch], s) + row_of(p_b[ch], s) * enter_b
            if emit_last:
                hl_ref[b, 0:1, lanes] = enter_f
                hl_ref[b, 1:2, lanes] = enter_b
            enter[0].append(jnp.concatenate(ent_f, axis=0))
            enter[1].append(jnp.concatenate(ent_b, axis=0))

    def rescan(j, carry):
        out = []
        for d, t in ((0, j), (1, seg - 1 - j)):
            idx = pl.ds(t, SUBLANE, stride=pitch)
            h_new = []
            for ch in range(nch):
                h = a_refs[d].at[ch][idx, :] * carry[d][ch] + b_refs[d].at[ch][idx, :]
                b_refs[d].at[ch][idx, :] = h
                h_new.append(h)
            out.append(tuple(h_new))
        return tuple(out)

    lax.fori_loop(0, seg, rescan, (tuple(enter[0]), tuple(enter[1])), unroll=2)

    for b in range(bt):
        for c in range(nslab):
            ch = b * nslab + c
            for s in range(SUBLANE):
                src = slice(s * pitch, s * pitch + seg)
                hs_ref[b * seq + s * seg:b * seq + (s + 1) * seg, c * LANE:(c + 1) * LANE] = (
                    b_refs[0][ch, src, :] + b_refs[1][ch, src, :])


def _rglru(layer, xr, row_start, nseq, seq, conv_w, conv_b, w_gate, b_gate, lam, h0=None, emit_last=False):
    bt = 4
    dg = RNN_GROUP
    ng = D_RNN // dg
    rows = bt * seq
    assert nseq % bt == 0 and row_start % rows == 0
    off = row_start // rows
    in_specs = [
        pl.BlockSpec((rows, dg), lambda bi, g: (off + bi, g)),
        pl.BlockSpec((None, CONV_W, dg), lambda bi, g: (layer, 0, g)),
        pl.BlockSpec((None, 1, dg), lambda bi, g: (layer, 0, g)),
        pl.BlockSpec((None, None, dg, 4 * dg), lambda bi, g: (layer, g, 0, 0)),
        pl.BlockSpec((None, None, 1, 4 * dg), lambda bi, g: (layer, g, 0, 0)),
        pl.BlockSpec((None, 2, dg), lambda bi, g: (layer, 0, g)),
    ]
    args = [xr, conv_w, conv_b, w_gate, b_gate, lam]
    if h0 is not None:
        in_specs.append(pl.BlockSpec((bt, None, 2, dg), lambda bi, g: (bi, layer, 0, g)))
        args.append(h0)
    out_specs = [pl.BlockSpec((rows, dg), lambda bi, g: (bi, g))]
    out_shape = [jax.ShapeDtypeStruct((nseq * seq, D_RNN), F32)]
    if emit_last:
        out_specs.append(pl.BlockSpec((bt, 2, dg), lambda bi, g: (bi, 0, g)))
        out_shape.append(jax.ShapeDtypeStruct((nseq, 2, D_RNN), F32))
    return pl.pallas_call(
        functools.partial(_rglru_kernel, seq=seq, bt=bt, has_h0=h0 is not None, emit_last=emit_last),
        grid=(nseq // bt, ng),
        in_specs=in_specs,
        out_specs=out_specs,
        out_shape=out_shape,
        scratch_shapes=[pltpu.VMEM((bt * (dg // LANE), SUBLANE * _segment_pitch(seq // SUBLANE), LANE), F32)
                        for _ in range(4)],
        compiler_params=_params("arbitrary", "arbitrary"),
        name=f"rglru_{seq}",
    )(*args)


def _attn_kernel(*refs, past, tq, rope):
    it = iter(refs)
    cq_ref, ckv_ref, kr_ref = (next(it) for _ in range(3))
    if rope:
        krr_ref, pckv_ref, pkr_ref, cos_ref, sin_ref = (next(it) for _ in range(5))
    wq_ref = next(it)
    wqr_ref = next(it) if rope else None
    wk_ref, wv_ref = next(it), next(it)
    o_ref = next(it)
    k_scr, v_scr = next(it), next(it)
    qi = pl.program_id(1)
    scale = 1.0 / math.sqrt(QK_NOPE + QK_ROPE)

    def fill(rows, ckv, k_rope):
        c = ckv.astype(BF16)
        k = _dot(c, wk_ref[...]) + jnp.tile(k_rope, (1, N_HEADS))
        k_scr[rows, :] = k.astype(BF16)
        v_scr[rows, :] = _dot(c, wv_ref[...]).astype(BF16)

    @pl.when(qi == 0)
    def _():
        seq = ckv_ref.shape[0]
        if rope:
            fill(pl.ds(0, past), pckv_ref[...], pkr_ref[...])
            fill(pl.ds(past, seq), ckv_ref[...],
                 kr_ref[...] * cos_ref[...] + krr_ref[...] * sin_ref[...])
        else:
            fill(pl.ds(0, seq), ckv_ref[...], kr_ref[...])

    cq = cq_ref[...]
    qa = _dot(cq, wq_ref[...])
    if rope:
        qb = _dot(cq, wqr_ref[...])
        q0 = pl.multiple_of(qi * tq, tq)
        cos_t = cos_ref[pl.ds(q0, tq), :]
        sin_t = sin_ref[pl.ds(q0, tq), :]
    for j in range(N_HEADS // 2):
        pair = None
        for h in (2 * j, 2 * j + 1):
            cols = slice(h * HEAD_PAD, (h + 1) * HEAD_PAD)
            qh = qa[:, cols]
            if rope:
                qh = qh * cos_t + qb[:, cols] * sin_t
            s = lax.dot_general(qh.astype(BF16), k_scr[:, cols], (((1,), (1,)), ((), ())),
                                preferred_element_type=F32)
            e = jnp.exp((s - jnp.max(s, axis=-1, keepdims=True)) * scale)
            denom = jnp.sum(e, axis=-1, keepdims=True)
            o = _dot(e.astype(BF16), v_scr[:, cols]) * (1.0 / denom)
            pair = o if pair is None else pair + o
        o_ref[:, j * LANE:(j + 1) * LANE] = pair.astype(BF16)


def _attention(layer, cq, ckv, kr, row_start, nseq, seq, w_q, w_k, w_v, lat=None):
    rope = lat is not None
    tq = min(_ATT_Q_TILE, seq)
    assert seq % tq == 0 and row_start % seq == 0
    past = lat[1].shape[2] if rope else 0
    total = past + seq
    hp = N_HEADS * HEAD_PAD
    nq = seq // tq
    s_off, q_off = row_start // seq, row_start // tq
    full = lambda w: pl.BlockSpec((seq, w), lambda b, q: (s_off + b, 0))
    q_rows = lambda w: pl.BlockSpec((tq, w), lambda b, q: (q_off + b * nq + q, 0))
    in_specs = [q_rows(Q_LORA), full(KV_LORA), full(HEAD_PAD)]
    args = [cq, ckv, kr]
    if rope:
        krr, pckv, pkr, cos, sin, w_q_rot = lat
        in_specs += [full(HEAD_PAD),
                     pl.BlockSpec((None, None, past, KV_LORA), lambda b, q: (b, layer, 0, 0)),
                     pl.BlockSpec((None, None, past, HEAD_PAD), lambda b, q: (b, layer, 0, 0)),
                     _fixed_spec((seq, HEAD_PAD)), _fixed_spec((seq, HEAD_PAD))]
        args += [krr, pckv, pkr, cos, sin]
    in_specs.append(_fixed_spec((Q_LORA, hp), (layer,)))
    args.append(w_q)
    if rope:
        in_specs.append(_fixed_spec((Q_LORA, hp), (layer,)))
        args.append(w_q_rot)
    in_specs += [_fixed_spec((KV_LORA, hp), (layer,)), _fixed_spec((KV_LORA, hp), (layer,))]
    args += [w_k, w_v]
    return pl.pallas_call(
        functools.partial(_attn_kernel, past=past, tq=tq, rope=rope),
        grid=(nseq, nq),
        in_specs=in_specs,
        out_specs=pl.BlockSpec((tq, N_HEADS * V_HEAD), lambda b, q: (b * nq + q, 0)),
        out_shape=jax.ShapeDtypeStruct((nseq * seq, N_HEADS * V_HEAD), BF16),
        scratch_shapes=[pltpu.VMEM((total, hp), BF16), pltpu.VMEM((total, hp), BF16)],
        compiler_params=_params("arbitrary", "arbitrary"),
        name=f"attn_{seq}",
    )(*args)


def _mix_kernel(*refs, n_x, n_ctx_tiles):
    (hsc_ref, hsl_ref, atc_ref, atl_ref, m_ref, wygg_ref, wro_ref, wao_ref, wo_ref,
     g_ref, b_ref, o_ref, wro_bf, wao_bf, wo_bf) = refs[n_x:]
    _cast_once([(wro_ref, wro_bf), (wao_ref, wao_bf), (wo_ref, wo_bf)])
    is_ctx = pl.program_id(0) < n_ctx_tiles
    hs = jnp.where(is_ctx, hsc_ref[...], hsl_ref[...])
    at = jnp.where(is_ctx, atc_ref[...], atl_ref[...])
    x = _load_stream(refs[:n_x], n_ctx_tiles)
    h = (x * (1.0 + m_ref[1:2, :]) + m_ref[0:1, :]).astype(BF16)
    ygg = _dot(h, wygg_ref[...])
    u = (hs * jax.nn.gelu(ygg[:, :D_RNN])).astype(BF16)
    rnn_out = _dot(u, wro_bf[...])
    att_out = _dot(at, wao_bf[...])
    merged = (jax.nn.sigmoid(ygg[:, D_RNN:D_RNN + D_MODEL]) * rnn_out
              + jax.nn.sigmoid(ygg[:, D_RNN + D_MODEL:]) * att_out)
    m = _dot(merged.astype(BF16), wo_bf[...])
    o_ref[...] = _layer_norm(ALPHA * x + m_ref[2:3, :] * m, g_ref[...], b_ref[...])


def _mix(layer, x, hs_ctx, hs_lat, at_ctx, at_lat, mods, w_ygg, w_ro, w_ao, w_o, ln_g, ln_b, n_ctx_rows,
         rows_per_latent):
    tm = _ROW_TILE
    nct = n_ctx_rows // tm
    x_specs, xs = _stream_specs(x, tm, n_ctx_rows)
    n_tok = sum(a.shape[0] for a in xs)
    row = pl.BlockSpec((tm, D_MODEL), lambda i: (i, 0))
    ctx_row = pl.BlockSpec((tm, D_MODEL), lambda i: (jnp.minimum(i, nct - 1), 0))
    lat_row = pl.BlockSpec((tm, D_MODEL), lambda i: (jnp.maximum(i - nct, 0), 0))
    sq = (D_MODEL, D_MODEL)
    return pl.pallas_call(
        functools.partial(_mix_kernel, n_x=len(xs), n_ctx_tiles=nct),
        grid=(n_tok // tm,),
        in_specs=x_specs + [ctx_row, lat_row, ctx_row, lat_row,
                  _mod_spec(layer, tm, n_ctx_rows, rows_per_latent),
                  _fixed_spec((D_MODEL, D_RNN + 2 * D_MODEL), (layer,)),
                  _fixed_spec(sq, (layer,)), _fixed_spec(sq, (layer,)), _fixed_spec(sq, (layer,)),
                  _fixed_spec((1, D_MODEL), (layer,)), _fixed_spec((1, D_MODEL), (layer,))],
        out_specs=row,
        out_shape=jax.ShapeDtypeStruct((n_tok, D_MODEL), F32),
        scratch_shapes=[pltpu.VMEM(sq, BF16) for _ in range(3)],
        compiler_params=_params("arbitrary"),
        name="mix_out",
    )(*xs, hs_ctx, hs_lat, at_ctx, at_lat, mods, w_ygg, w_ro, w_ao, w_o, ln_g, ln_b)


def _ffn_kernel(x_ref, m_ref, w1_ref, w3_ref, w2_ref, g_ref, b_ref, o_ref):
    x = x_ref[...]
    h = (x * (1.0 + m_ref[4:5, :]) + m_ref[3:4, :]).astype(BF16)
    f = None
    for c in range(D_FF // _FF_CHUNK):
        cols = slice(c * _FF_CHUNK, (c + 1) * _FF_CHUNK)
        act = (jax.nn.silu(_dot(h, w1_ref[:, cols])) * _dot(h, w3_ref[:, cols])).astype(BF16)
        y = _dot(act, w2_ref[cols, :])
        f = y if f is None else f + y
    o_ref[...] = _layer_norm(ALPHA * x + m_ref[5:6, :] * f, g_ref[...], b_ref[...])


def _ffn_dense(layer, x, mods, w1, w3, w2, ln_g, ln_b, n_ctx_rows, rows_per_latent):
    n_tok = x.shape[0]
    tm = _ROW_TILE
    j = layer // 2
    row = pl.BlockSpec((tm, D_MODEL), lambda i: (i, 0))
    return pl.pallas_call(
        _ffn_kernel,
        grid=(n_tok // tm,),
        in_specs=[row, _mod_spec(layer, tm, n_ctx_rows, rows_per_latent),
                  _fixed_spec((D_MODEL, D_FF), (j,)), _fixed_spec((D_MODEL, D_FF), (j,)),
                  _fixed_spec((D_FF, D_MODEL), (j,)),
                  _fixed_spec((1, D_MODEL), (layer,)), _fixed_spec((1, D_MODEL), (layer,))],
        out_specs=row,
        out_shape=jax.ShapeDtypeStruct((n_tok, D_MODEL), F32),
        compiler_params=_params("arbitrary"),
        name="ffn_dense",
    )(x, mods, w1, w3, w2, ln_g, ln_b)


def _split_bf16(a):
    hi = a.astype(BF16)
    return hi, (a - hi.astype(F32)).astype(BF16)


def _route_kernel(x_ref, m_ref, wr_ref, br_ref, h_ref, ids_ref, gw_ref, cnt_ref, run_scr, tri_scr):
    tm = x_ref.shape[0]

    @pl.when(pl.program_id(0) == 0)
    def _():
        run_scr[...] = jnp.zeros_like(run_scr)
        r = lax.broadcasted_iota(jnp.int32, (tm, tm), 0)
        c = lax.broadcasted_iota(jnp.int32, (tm, tm), 1)
        tri_scr[...] = jnp.where(c < r, 1.0, 0.0).astype(BF16)

    h = x_ref[...] * (1.0 + m_ref[4:5, :]) + m_ref[3:4, :]
    h_ref[...] = h
    h_hi, h_lo = _split_bf16(h)

    w_hi, w_lo = _split_bf16(wr_ref[...])
    logits = _dot(h_hi, w_hi) + (_dot(h_hi, w_lo) + _dot(h_lo, w_hi)) + br_ref[...]
    lane = lax.broadcasted_iota(jnp.int32, logits.shape, 1)
    big = jnp.int32(N_EXPERTS)
    v1 = jnp.max(logits, axis=-1, keepdims=True)
    i1 = jnp.min(jnp.where(logits == v1, lane, big), axis=-1, keepdims=True)
    rest = jnp.where(lane == i1, -jnp.inf, logits)
    v2 = jnp.max(rest, axis=-1, keepdims=True)
    i2 = jnp.min(jnp.where(rest == v2, lane, big), axis=-1, keepdims=True)
    e2 = jnp.exp(v2 - v1)
    g1 = 1.0 / (1.0 + e2)
    g2 = e2 / (1.0 + e2)

    m1, m2 = lane == i1, lane == i2
    member = jnp.where(m1, 1.0, 0.0) + jnp.where(m2, 1.0, 0.0)
    before = _dot(tri_scr[...], member.astype(BF16)) + run_scr[...]
    r1 = jnp.sum(jnp.where(m1, before, 0.0), axis=-1, keepdims=True).astype(jnp.int32)
    r2 = jnp.sum(jnp.where(m2, before, 0.0), axis=-1, keepdims=True).astype(jnp.int32)
    run_scr[...] += jnp.sum(member, axis=0, keepdims=True)

    ids_ref[...] = jnp.where(lane == 0, i1, jnp.where(lane == 1, i2, jnp.where(lane == 2, r1, r2)))
    gw_ref[...] = jnp.where(lane == 0, g1, g2)
    cnt_ref[...] = run_scr[...].astype(jnp.int32)


def _moe_route(layer, x, mods, w_router, b_router, n_ctx_rows, rows_per_latent):
    n_tok = x.shape[0]
    tm = _ROW_TILE
    j = layer // 2
    row = lambda w: pl.BlockSpec((tm, w), lambda i: (i, 0))
    return pl.pallas_call(
        _route_kernel,
        grid=(n_tok // tm,),
        in_specs=[row(D_MODEL), _mod_spec(layer, tm, n_ctx_rows, rows_per_latent),
                  _fixed_spec((D_MODEL, N_EXPERTS), (j,)), _fixed_spec((1, N_EXPERTS), (j,))],
        out_specs=[row(D_MODEL), row(N_EXPERTS), row(N_EXPERTS),
                   pl.BlockSpec((1, N_EXPERTS), lambda i: (0, 0))],
        out_shape=[jax.ShapeDtypeStruct((n_tok, D_MODEL), F32),
                   jax.ShapeDtypeStruct((n_tok, N_EXPERTS), jnp.int32),
                   jax.ShapeDtypeStruct((n_tok, N_EXPERTS), F32),
                   jax.ShapeDtypeStruct((1, N_EXPERTS), jnp.int32)],
        scratch_shapes=[pltpu.VMEM((1, N_EXPERTS), F32), pltpu.VMEM((tm, tm), BF16)],
        compiler_params=_params("arbitrary"),
        name="moe_route",
    )(x, mods, w_router, b_router)


def _tables_kernel(er_ref, g_ref, cnt_ref, exp_ref, on_ref, tok_ref, gate_ref, start_scr, *, n_tok, tm):
    n_tiles = exp_ref.shape[0]
    ends = []
    total = jnp.int32(0)
    for e in range(N_EXPERTS):
        start_scr[e] = total
        total = total + ((cnt_ref[0, e] + (tm - 1)) // tm) * tm
        ends.append(total)

    tok_ref[...] = jnp.full(tok_ref.shape, n_tok, jnp.int32)
    gate_ref[...] = jnp.zeros(gate_ref.shape, F32)

    def place(t, carry):
        for k in range(TOP_K):
            slot = start_scr[er_ref[2 * TOP_K * t + k]] + er_ref[2 * TOP_K * t + TOP_K + k]
            tok_ref[pl.ds(slot, 1), :] = jnp.full((1, LANE), t, jnp.int32)
            gate_ref[pl.ds(slot, 1), :] = jnp.full((1, LANE), g_ref[TOP_K * t + k], F32)
        return carry

    lax.fori_loop(0, n_tok, place, 0, unroll=8)

    def tiles(i, carry):
        first = i * tm
        owner = jnp.int32(0)
        for e in range(N_EXPERTS):
            owner = owner + jnp.where(first >= ends[e], 1, 0)
        exp_ref[i] = jnp.minimum(owner, N_EXPERTS - 1)
        on_ref[i] = jnp.where(first < total, 1, 0)
        return carry

    lax.fori_loop(0, n_tiles, tiles, 0)


def _slot_tables(ids, gw, counts, n_tok):
    tm = _MOE_SLOT_TILE
    n_slots = TOP_K * n_tok + N_EXPERTS * tm
    n_tiles = n_slots // tm
    smem = pl.BlockSpec(memory_space=pltpu.SMEM)
    vmem = pl.BlockSpec(memory_space=pltpu.VMEM)
    tile_exp, tile_on, tok_rows, gate_rows = pl.pallas_call(
        functools.partial(_tables_kernel, n_tok=n_tok, tm=tm),
        in_specs=[smem, smem, smem],
        out_specs=[smem, smem, vmem, vmem],
        out_shape=[jax.ShapeDtypeStruct((n_tiles,), jnp.int32), jax.ShapeDtypeStruct((n_tiles,), jnp.int32),
                   jax.ShapeDtypeStruct((n_slots, LANE), jnp.int32), jax.ShapeDtypeStruct((n_slots, LANE), F32)],
        scratch_shapes=[pltpu.SMEM((N_EXPERTS,), jnp.int32)],
        compiler_params=pltpu.CompilerParams(vmem_limit_bytes=VMEM_LIMIT),
        name="moe_tables",
    )(ids[:, :2 * TOP_K].reshape(-1), gw[:, :TOP_K].reshape(-1), counts)
    return tile_exp, tile_on, tok_rows[:, 0], gate_rows


def _moe_up_kernel(exp_ref, on_ref, tok_ref, h_ref, w1_ref, w3_ref, act_ref, xg0_scr, xg1_scr, *,
                   n_tok, n_tiles):
    t = pl.program_id(0)
    tm = xg0_scr.shape[0]

    def gather_row(tile, r, dst):
        src = jnp.minimum(tok_ref[tile * tm + r], n_tok - 1)
        dst[pl.ds(r, 1), :] = h_ref[pl.ds(src, 1), :]

    @pl.when(t == 0)
    def _():
        def body(r, carry):
            gather_row(0, r, xg0_scr)
            return carry

        lax.fori_loop(0, tm, body, 0, unroll=8)

    for parity, (cur, nxt) in enumerate(((xg0_scr, xg1_scr), (xg1_scr, xg0_scr))):
        @pl.when(jnp.logical_and(on_ref[t] == 1, t % 2 == parity))
        def _():
            h = cur[...].astype(BF16)
            nxt_tile = jnp.minimum(t + 1, n_tiles - 1)
            for r in range(tm):
                gather_row(nxt_tile, r, nxt)
            act_ref[...] = (jax.nn.silu(_dot(h, w1_ref[...])) * _dot(h, w3_ref[...])).astype(BF16)

    @pl.when(on_ref[t] == 0)
    def _():
        act_ref[...] = jnp.zeros_like(act_ref)


def _moe_up(layer, tables, h, w1, w3):
    tile_exp, tile_on, slot_tok, _ = tables
    n_tok = h.shape[0]
    tm = _MOE_SLOT_TILE
    n_tiles = tile_exp.shape[0]
    j = layer // 2
    wspec = pl.BlockSpec((None, None, D_MODEL, D_EXPERT), lambda t, ex, on, tk: (j, ex[t], 0, 0),
                         pipeline_mode=pl.Buffered(1))
    return pl.pallas_call(
        functools.partial(_moe_up_kernel, n_tok=n_tok, n_tiles=n_tiles),
        grid_spec=pltpu.PrefetchScalarGridSpec(
            num_scalar_prefetch=3,
            grid=(n_tiles,),
            in_specs=[pl.BlockSpec((n_tok, D_MODEL), lambda t, ex, on, tk: (0, 0),
                                   pipeline_mode=pl.Buffered(1)),
                      wspec, wspec],
            out_specs=pl.BlockSpec((tm, D_EXPERT), lambda t, ex, on, tk: (t, 0)),
            scratch_shapes=[pltpu.VMEM((tm, D_MODEL), F32), pltpu.VMEM((tm, D_MODEL), F32)],
        ),
        out_shape=jax.ShapeDtypeStruct((n_tiles * tm, D_EXPERT), BF16),
        compiler_params=_params("arbitrary"),
        name="moe_up",
    )(tile_exp, tile_on, slot_tok, h, w1, w3)


def _moe_down_kernel(exp_ref, on_ref, tok_ref, act_ref, gate_ref, w2_ref, x_ref, m_ref, g_ref, b_ref,
                     *rest, n_tiles, ln_rows, n_ctx_ln_tiles, split):
    o_refs, (acc_scr, y0_scr, y1_scr) = rest[:-3], rest[-3:]
    t = pl.program_id(0)
    tm = y0_scr.shape[0]

    @pl.when(t == 0)
    def _():
        acc_scr[...] = jnp.zeros_like(acc_scr)

    def project(dst):
        dst[...] = _dot(act_ref[...], w2_ref[...]) * gate_ref[:, 0:1]

    def scatter_group(tile, base, ys):
        toks = [tok_ref[tile * tm + base + u] for u in range(SUBLANE)]
        rows = [acc_scr[pl.ds(toks[u], 1), :] for u in range(SUBLANE)]
        for u in range(SUBLANE):
            acc_scr[pl.ds(toks[u], 1), :] = rows[u] + ys[u:u + 1, :]

    last = n_tiles - 1
    project_now = jnp.logical_and(t < n_tiles, on_ref[jnp.minimum(t, last)] == 1)
    add_prev = jnp.logical_and(jnp.logical_and(t >= 1, t <= n_tiles), on_ref[jnp.clip(t - 1, 0, last)] == 1)
    for parity, (cur, prv) in enumerate(((y0_scr, y1_scr), (y1_scr, y0_scr))):
        mine = t % 2 == parity

        @pl.when(jnp.logical_and(mine, jnp.logical_and(project_now, add_prev)))
        def _():
            project(cur)
            for g in range(tm // SUBLANE):
                scatter_group(t - 1, g * SUBLANE, prv[g * SUBLANE:(g + 1) * SUBLANE, :])

        @pl.when(jnp.logical_and(mine, jnp.logical_and(project_now, jnp.logical_not(add_prev))))
        def _():
            project(cur)

        @pl.when(jnp.logical_and(mine, jnp.logical_and(add_prev, jnp.logical_not(project_now))))
        def _():
            def body(g, carry):
                base = pl.multiple_of(g * SUBLANE, SUBLANE)
                scatter_group(t - 1, base, prv[pl.ds(base, SUBLANE), :])
                return carry

            lax.fori_loop(0, tm // SUBLANE, body, 0)

    @pl.when(t > n_tiles)
    def _():
        r0 = pl.multiple_of((t - n_tiles - 1) * ln_rows, ln_rows)
        f = acc_scr[pl.ds(r0, ln_rows), :]
        out = _layer_norm(ALPHA * x_ref[...] + m_ref[5:6, :] * f, g_ref[...], b_ref[...])
        if split:
            is_ctx = t - n_tiles - 1 < n_ctx_ln_tiles

            @pl.when(is_ctx)
            def _():
                o_refs[0][...] = out

            @pl.when(jnp.logical_not(is_ctx))
            def _():
                o_refs[1][...] = out
        else:
            o_refs[0][...] = out


def _moe_down(layer, tables, act, x, mods, w2, ln_g, ln_b, n_ctx_rows, rows_per_latent, split):
    tile_exp, tile_on, slot_tok, slot_gate = tables
    n_tok = x.shape[0]
    tm = _MOE_SLOT_TILE
    tl = _MOE_LN_TILE
    n_tiles = tile_exp.shape[0]
    j = layer // 2
    slot = lambda t: jnp.minimum(t, n_tiles - 1)
    ln = lambda t: jnp.maximum(t - n_tiles - 1, 0)
    n_ctx_tiles, per = n_ctx_rows // tl, rows_per_latent // tl

    def mod_index(t, *_):
        i = ln(t)
        return (layer, jnp.where(i < n_ctx_tiles, 0, 1 + (i - n_ctx_tiles) // per), 0, 0)

    fixed = lambda shape, lead: pl.BlockSpec((None,) + shape, lambda *_: (lead,) + (0,) * len(shape),
                                             pipeline_mode=pl.Buffered(1))
    if split:
        out_specs = [pl.BlockSpec((tl, D_MODEL), lambda t, *_: (jnp.minimum(ln(t), n_ctx_tiles - 1), 0)),
                     pl.BlockSpec((tl, D_MODEL), lambda t, *_: (jnp.maximum(ln(t) - n_ctx_tiles, 0), 0))]
        out_shape = [jax.ShapeDtypeStruct((n_ctx_rows, D_MODEL), F32),
                     jax.ShapeDtypeStruct((n_tok - n_ctx_rows, D_MODEL), F32)]
    else:
        out_specs = pl.BlockSpec((tl, D_MODEL), lambda t, *_: (ln(t), 0))
        out_shape = jax.ShapeDtypeStruct((n_tok, D_MODEL), F32)
    return pl.pallas_call(
        functools.partial(_moe_down_kernel, n_tiles=n_tiles, ln_rows=tl, n_ctx_ln_tiles=n_ctx_tiles,
                          split=split),
        grid_spec=pltpu.PrefetchScalarGridSpec(
            num_scalar_prefetch=3,
            grid=(n_tiles + 1 + n_tok // tl,),
            in_specs=[pl.BlockSpec((tm, D_EXPERT), lambda t, ex, *_: (slot(t), 0)),
                      pl.BlockSpec((tm, LANE), lambda t, ex, *_: (slot(t), 0)),
                      pl.BlockSpec((None, None, D_EXPERT, D_MODEL), lambda t, ex, *_: (j, ex[slot(t)], 0, 0)),
                      pl.BlockSpec((tl, D_MODEL), lambda t, *_: (ln(t), 0)),
                      pl.BlockSpec((None, None, 6, D_MODEL), mod_index),
                      fixed((1, D_MODEL), layer), fixed((1, D_MODEL), layer)],
            out_specs=out_specs,
            scratch_shapes=[pltpu.VMEM((n_tok + SUBLANE, D_MODEL), F32), pltpu.VMEM((tm, D_MODEL), F32),
                            pltpu.VMEM((tm, D_MODEL), F32)],
        ),
        out_shape=out_shape,
        compiler_params=_params("arbitrary"),
        name="moe_down",
    )(tile_exp, tile_on, slot_tok, act, slot_gate, w2, x, mods, ln_g, ln_b)


def _ffn_moe(layer, x, mods, w_router, b_router, w1, w3, w2, ln_g, ln_b, n_ctx_rows, rows_per_latent,
             split=False):
    h, ids, gw, counts = _moe_route(layer, x, mods, w_router, b_router, n_ctx_rows, rows_per_latent)
    tables = _slot_tables(ids, gw, counts, x.shape[0])
    act = _moe_up(layer, tables, h, w1, w3)
    return _moe_down(layer, tables, act, x, mods, w2, ln_g, ln_b, n_ctx_rows, rows_per_latent, split)


def _rot_cols(w):
    ws = w.reshape(w.shape[:-1] + (2, 2, ROPE_AXIS // 2))
    return jnp.stack([-ws[..., 1, :], ws[..., 0, :]], axis=-2).reshape(w.shape)


def _pad_rope_cols(w):
    return jnp.pad(w, ((0, 0),) * (w.ndim - 1) + ((ROPE_OFF, HEAD_PAD - ROPE_OFF - QK_ROPE),))


def _rope_tables(rows):
    r = jnp.repeat(jnp.arange(rows, dtype=F32), GRID_W)
    col = jnp.tile(jnp.arange(GRID_W, dtype=F32), rows)
    freqs = ROPE_THETA ** (-jnp.arange(0, ROPE_AXIS, 2, dtype=F32) / ROPE_AXIS)
    ar = r[:, None] * freqs
    ac = col[:, None] * freqs
    ang = jnp.concatenate([ar, ar, ac, ac], axis=-1)
    cos, sin = jnp.cos(ang), jnp.sin(ang)
    n = cos.shape[0]
    cos_p = jnp.concatenate([jnp.ones((n, ROPE_OFF), F32), cos,
                             jnp.zeros((n, HEAD_PAD - ROPE_OFF - QK_ROPE), F32)], axis=-1)
    return cos_p, _pad_rope_cols(sin)


def _prepared_weights(w_in, rg_wa, rg_ba, rg_wx, rg_bx, w_qb, w_kvb):
    bf = lambda a: a.astype(BF16)
    w_kr = w_in[:, :, _COL_KR:_COL_KR + QK_ROPE]
    w_s = jnp.concatenate([w_in[:, :, _COL_Q:_COL_KR], _pad_rope_cols(w_kr),
                           _pad_rope_cols(_rot_cols(w_kr))], axis=-1)
    w_x = w_in[:, :, :D_RNN]
    w_ygg = jnp.concatenate([w_in[:, :, D_RNN:2 * D_RNN], w_in[:, :, _COL_GATE:]], axis=-1)

    wq = w_qb.reshape(DEPTH, Q_LORA, N_HEADS, QK_NOPE + QK_ROPE)
    q_nope, q_rope = wq[..., :QK_NOPE], wq[..., QK_NOPE:]
    tail = jnp.zeros((DEPTH, Q_LORA, N_HEADS, HEAD_PAD - QK_NOPE - QK_ROPE), F32)
    hp = N_HEADS * HEAD_PAD
    w_q = jnp.concatenate([q_nope, q_rope, tail], axis=-1).reshape(DEPTH, Q_LORA, hp)
    w_q_rot = jnp.concatenate([jnp.zeros_like(q_nope), _rot_cols(q_rope), tail], axis=-1)
    w_q_rot = w_q_rot.reshape(DEPTH, Q_LORA, hp)
    wkv = w_kvb.reshape(DEPTH, KV_LORA, N_HEADS, QK_NOPE + V_HEAD)
    k_nope, v = wkv[..., :QK_NOPE], wkv[..., QK_NOPE:]
    w_k = jnp.concatenate([k_nope, jnp.zeros((DEPTH, KV_LORA, N_HEADS, HEAD_PAD - QK_NOPE), F32)], axis=-1)
    w_k = w_k.reshape(DEPTH, KV_LORA, hp)
    vz = jnp.zeros_like(v)
    odd = (jnp.arange(N_HEADS) % 2 == 1)[None, None, :, None]
    w_v = jnp.where(odd, jnp.concatenate([vz, v], axis=-1), jnp.concatenate([v, vz], axis=-1))
    w_v = w_v.reshape(DEPTH, KV_LORA, hp)

    nblk = RNN_GROUP // RNN_BW
    ng = D_RNN // RNN_GROUP
    eye = jnp.eye(nblk, dtype=F32)

    def block_diag(w):
        wg = w.reshape(DEPTH, ng, nblk, RNN_BW, RNN_BW)
        return jnp.einsum('lgnjk,nm->lgnjmk', wg, eye).reshape(DEPTH, ng, RNN_GROUP, RNN_GROUP)

    w_gate = jnp.concatenate([block_diag(rg_wa[:, 0]), block_diag(rg_wx[:, 0]),
                              block_diag(rg_wa[:, 1]), block_diag(rg_wx[:, 1])], axis=-1)
    grp = lambda b: b.reshape(DEPTH, ng, 1, RNN_GROUP)
    b_gate = jnp.concatenate([grp(rg_ba[:, 0]), grp(rg_bx[:, 0]), grp(rg_ba[:, 1]), grp(rg_bx[:, 1])],
                             axis=-1)
    return dict(w_x=bf(w_x), w_s=bf(w_s), w_ygg=bf(w_ygg), w_q=bf(w_q), w_q_rot=bf(w_q_rot), w_k=bf(w_k), w_v=bf(w_v),
                w_gate=bf(w_gate), b_gate=b_gate)


def kernel(x_prompt, x_sample, cache_ckv, cache_krope, state_rnn, c, c_ctx, w_ada, b_ada, w_in, conv_w, conv_b, rg_wa, rg_ba, rg_wx, rg_bx, rg_lambda, w_rnn_out, q_norm_g, w_qb, kv_norm_g, w_kvb, w_attn_out, w_out, ln1_g, ln1_b, ln2_g, ln2_b, w1_dense, w3_dense, w2_dense, w_router, b_router, w1_exp, w3_exp, w2_exp):
    nb, seq, _ = x_prompt.shape
    db, dseq, _ = x_sample.shape
    n_ctx = nb * seq
    n_lat = db * dseq
    assert n_ctx % _ROW_TILE == 0 and dseq % _ROW_TILE == 0 and db + 1 <= 8
    assert DEPTH % 2 == 0
    bf = lambda a: a.astype(BF16)
    vec = lambda a: a.reshape(a.shape[0], 1, a.shape[-1])

    cond8 = jnp.concatenate([c_ctx[None, :], c, jnp.zeros((8 - 1 - db, D_MODEL), F32)], axis=0)
    mods = _ada_mods(cond8, w_ada, b_ada)
    cos_p, sin_p = _rope_tables(dseq // GRID_W)
    past_kr = _pad_rope_cols(cache_krope)
    pw = _prepared_weights(w_in, rg_wa, rg_ba, rg_wx, rg_bx, w_qb, w_kvb)
    w1d, w3d, w2d = bf(w1_dense), bf(w3_dense), bf(w2_dense)
    w1e, w3e, w2e = bf(w1_exp), bf(w3_exp), bf(w2_exp)
    qg, kvg, cb = vec(q_norm_g), vec(kv_norm_g), vec(conv_b)
    l1g, l1b, l2g, l2b = vec(ln1_g), vec(ln1_b), vec(ln2_g), vec(ln2_b)
    br = vec(b_router)
    tiles = (n_ctx, dseq)

    x = (x_prompt.reshape(n_ctx, D_MODEL), x_sample.reshape(n_lat, D_MODEL))
    new_ckv, new_kr, new_h = [], [], []
    for l in range(DEPTH):
        xr, cq, ckv, kr, krr = _in_proj(l, x, mods, pw['w_x'], pw['w_s'], qg, kvg, *tiles)
        new_ckv.append(ckv[:n_ctx].reshape(nb, seq, KV_LORA))
        new_kr.append(kr[:n_ctx, ROPE_OFF:ROPE_OFF + QK_ROPE].reshape(nb, seq, QK_ROPE))

        rnn_args = (conv_w, cb, pw['w_gate'], pw['b_gate'], rg_lambda)
        hs_ctx, h_last = _rglru(l, xr, 0, nb, seq, *rnn_args, emit_last=True)
        (hs_lat,) = _rglru(l, xr, n_ctx, db, dseq, *rnn_args, h0=state_rnn)
        new_h.append(h_last)

        att_w = (pw['w_q'], pw['w_k'], pw['w_v'])
        at_ctx = _attention(l, cq, ckv, kr, 0, nb, seq, *att_w)
        at_lat = _attention(l, cq, ckv, kr, n_ctx, db, dseq, *att_w,
                            lat=(krr, cache_ckv, past_kr, cos_p, sin_p, pw['w_q_rot']))

        x = _mix(l, x, hs_ctx, hs_lat, at_ctx, at_lat, mods, pw['w_ygg'], w_rnn_out, w_attn_out, w_out,
                 l1g, l1b, *tiles)
        if l % 2 == 0:
            x = _ffn_dense(l, x, mods, w1d, w3d, w2d, l2g, l2b, *tiles)
        else:
            x = _ffn_moe(l, x, mods, w_router, br, w1e, w3e, w2e, l2g, l2b, *tiles, split=l == DEPTH - 1)

    return (x[0].reshape(nb, seq, D_MODEL), x[1].reshape(db, dseq, D_MODEL),
            jnp.stack(new_ckv, axis=1), jnp.stack(new_kr, axis=1), jnp.stack(new_h, axis=1))
```

```python
import functools
import math

import jax
import jax.numpy as jnp
from jax import lax
from jax.experimental import pallas as pl
from jax.experimental.pallas import tpu as pltpu

F32 = jnp.float32
BF16 = jnp.bfloat16

D_MODEL = 1024
DEPTH = 4
GRID_W = 64
D_RNN = D_MODEL
RNN_BLOCKS = 16
RNN_BW = D_RNN // RNN_BLOCKS
CONV_W = 4
RG_C = 8.0
N_HEADS = 16
QK_NOPE = 64
QK_ROPE = 32
V_HEAD = 64
Q_LORA = 256
KV_LORA = 128
ROPE_AXIS = QK_ROPE // 2
ROPE_THETA = 10000.0
D_FF = 2816
N_EXPERTS = 8
TOP_K = 2
D_EXPERT = 1408
ALPHA = (2 * DEPTH) ** 0.25
LN_EPS = 1e-5
RMS_EPS = 1e-6

LANE = 128
SUBLANE = 8
HEAD_PAD = LANE
ROPE_OFF = QK_NOPE
RNN_GROUP = 256
VMEM_LIMIT = 56 * 1024 * 1024

_ROW_TILE = 512
_MOE_SLOT_TILE = 512
_MOE_LN_TILE = 256
_ATT_Q_TILE = 512
_FF_CHUNK = 1408

_COL_Q = 2 * D_RNN
_COL_KR = _COL_Q + Q_LORA + KV_LORA
_COL_GATE = _COL_KR + QK_ROPE


def _dot(a, b):
    return jnp.dot(a, b, preferred_element_type=F32)


def _layer_norm(y, g, b):
    mu = jnp.mean(y, axis=-1, keepdims=True)
    d = y - mu
    var = jnp.mean(d * d, axis=-1, keepdims=True)
    return d * lax.rsqrt(var + LN_EPS) * g + b


def _rms_norm(y, g):
    return y * lax.rsqrt(jnp.mean(y * y, axis=-1, keepdims=True) + RMS_EPS) * g


def _params(*sem):
    return pltpu.CompilerParams(dimension_semantics=sem, vmem_limit_bytes=VMEM_LIMIT)


def _fixed_spec(shape, lead=(), tail=None):
    tail = (0,) * len(shape) if tail is None else tuple(tail)
    index = tuple(lead) + tail
    return pl.BlockSpec((None,) * len(lead) + tuple(shape), lambda *_: index,
                        pipeline_mode=pl.Buffered(1))


def _mod_spec(layer, tm, n_ctx_rows, rows_per_latent):
    n_ctx_tiles = n_ctx_rows // tm
    per = rows_per_latent // tm

    def index(i, *_):
        return (layer, jnp.where(i < n_ctx_tiles, 0, 1 + (i - n_ctx_tiles) // per), 0, 0)

    return pl.BlockSpec((None, None, 6, D_MODEL), index)


def _cast_once(pairs):
    @pl.when(pl.program_id(0) == 0)
    def _():
        for src, dst in pairs:
            dst[...] = src[...].astype(BF16)


def _stream_specs(x, tm, n_ctx_rows):
    if isinstance(x, tuple):
        nct = n_ctx_rows // tm
        return ([pl.BlockSpec((tm, D_MODEL), lambda i, *_: (jnp.minimum(i, nct - 1), 0)),
                 pl.BlockSpec((tm, D_MODEL), lambda i, *_: (jnp.maximum(i - nct, 0), 0))], list(x))
    return [pl.BlockSpec((tm, D_MODEL), lambda i, *_: (i, 0))], [x]


def _load_stream(refs, n_ctx_tiles):
    if len(refs) == 2:
        return jnp.where(pl.program_id(0) < n_ctx_tiles, refs[0][...], refs[1][...])
    return refs[0][...]


def _mods_kernel(c_ref, w_ref, b_ref, o_ref):
    s = jax.nn.silu(c_ref[...]).astype(BF16)
    o_ref[...] = _dot(s, w_ref[...].astype(BF16)) + b_ref[...]


def _ada_mods(cond8, w_ada, b_ada):
    out = pl.pallas_call(
        _mods_kernel,
        grid=(DEPTH, 6),
        in_specs=[
            pl.BlockSpec((8, D_MODEL), lambda l, j: (0, 0)),
            pl.BlockSpec((None, D_MODEL, D_MODEL), lambda l, j: (l, 0, j)),
            pl.BlockSpec((None, 1, D_MODEL), lambda l, j: (l, 0, j)),
        ],
        out_specs=pl.BlockSpec((None, 8, D_MODEL), lambda l, j: (l, 0, j)),
        out_shape=jax.ShapeDtypeStruct((DEPTH, 8, 6 * D_MODEL), F32),
        compiler_params=_params("arbitrary", "arbitrary"),
        name="ada_mods",
    )(cond8, w_ada, b_ada.reshape(DEPTH, 1, 6 * D_MODEL))
    return out.reshape(DEPTH, 8, 6, D_MODEL)


_S_COLS = Q_LORA + KV_LORA + 2 * HEAD_PAD


def _inproj_kernel(*refs, n_x, n_ctx_tiles):
    m_ref, wx_ref, ws_ref, qg_ref, kvg_ref, xr_ref, cq_ref, ckv_ref, kr_ref, krr_ref = refs[n_x:]
    x = _load_stream(refs[:n_x], n_ctx_tiles)
    h = (x * (1.0 + m_ref[1:2, :]) + m_ref[0:1, :]).astype(BF16)
    xr_ref[...] = _dot(h, wx_ref[...])
    s = _dot(h, ws_ref[...])
    cq_ref[...] = _rms_norm(s[:, :Q_LORA], qg_ref[...]).astype(BF16)
    ckv_ref[...] = _rms_norm(s[:, Q_LORA:Q_LORA + KV_LORA], kvg_ref[...])
    kr_ref[...] = s[:, Q_LORA + KV_LORA:Q_LORA + KV_LORA + HEAD_PAD]
    krr_ref[...] = s[:, Q_LORA + KV_LORA + HEAD_PAD:]


def _in_proj(layer, x, mods, w_x, w_s, q_g, kv_g, n_ctx_rows, rows_per_latent):
    tm = _ROW_TILE
    x_specs, xs = _stream_specs(x, tm, n_ctx_rows)
    n_tok = sum(a.shape[0] for a in xs)
    row = lambda w: pl.BlockSpec((tm, w), lambda i: (i, 0))
    return pl.pallas_call(
        functools.partial(_inproj_kernel, n_x=len(xs), n_ctx_tiles=n_ctx_rows // tm),
        grid=(n_tok // tm,),
        in_specs=x_specs + [
            _mod_spec(layer, tm, n_ctx_rows, rows_per_latent),
            _fixed_spec((D_MODEL, D_RNN), (layer,)),
            _fixed_spec((D_MODEL, _S_COLS), (layer,)),
            _fixed_spec((1, Q_LORA), (layer,)),
            _fixed_spec((1, KV_LORA), (layer,)),
        ],
        out_specs=[row(D_RNN), row(Q_LORA), row(KV_LORA), row(HEAD_PAD), row(HEAD_PAD)],
        out_shape=[
            jax.ShapeDtypeStruct((n_tok, D_RNN), F32),
            jax.ShapeDtypeStruct((n_tok, Q_LORA), BF16),
            jax.ShapeDtypeStruct((n_tok, KV_LORA), F32),
            jax.ShapeDtypeStruct((n_tok, HEAD_PAD), F32),
            jax.ShapeDtypeStruct((n_tok, HEAD_PAD), F32),
        ],
        compiler_params=_params("arbitrary"),
        name="in_proj",
    )(*xs, mods, w_x, w_s, q_g, kv_g)


def _segment_pitch(seg):
    assert seg % SUBLANE == 0
    return seg if (seg // SUBLANE) % 2 == 1 else seg + SUBLANE


def _rglru_kernel(*refs, seq, bt, has_h0, emit_last):
    it = iter(refs)
    x_ref, cw_ref, cb_ref, wg_ref, bg_ref, lam_ref = (next(it) for _ in range(6))
    h0_ref = next(it) if has_h0 else None
    hs_ref = next(it)
    hl_ref = next(it) if emit_last else None
    a_refs = (next(it), next(it))
    b_refs = (next(it), next(it))
    dg = RNN_GROUP
    nslab = dg // LANE
    seg = seq // SUBLANE
    pitch = _segment_pitch(seg)

    row = lax.broadcasted_iota(jnp.int32, (seq, dg), 0)
    decay_rate = [RG_C * jax.nn.softplus(-lam_ref[d:d + 1, :]) for d in range(2)]
    for b in range(bt):
        rows = slice(b * seq, (b + 1) * seq)
        x = x_ref[rows, :]
        xm1 = jnp.where(row >= 1, pltpu.roll(x, 1, 0), 0.0)
        xp1 = jnp.where(row < seq - 1, pltpu.roll(x, seq - 1, 0), 0.0)
        xp2 = jnp.where(row < seq - 2, pltpu.roll(x, seq - 2, 0), 0.0)
        xc = cb_ref[...] + xm1 * cw_ref[0:1, :]
        xc = xc + x * cw_ref[1:2, :]
        xc = xc + xp1 * cw_ref[2:3, :]
        xc = xc + xp2 * cw_ref[3:4, :]
        gates = _dot(xc.astype(BF16), wg_ref[...]) + bg_ref[...]
        for d in range(2):
            r = jax.nn.sigmoid(gates[:, (2 * d) * dg:(2 * d + 1) * dg])
            i = jax.nn.sigmoid(gates[:, (2 * d + 1) * dg:(2 * d + 2) * dg])
            neg_log_a = r * decay_rate[d]
            a = jnp.exp(-neg_log_a)
            y = jnp.tanh(neg_log_a) * (a * a + 1.0)
            bq = jnp.where(y > 0.0, y * lax.rsqrt(y), 0.0) * (i * xc)
            for s in range(SUBLANE):
                for c in range(nslab):
                    src = (slice(s * seg, (s + 1) * seg), slice(c * LANE, (c + 1) * LANE))
                    dst = slice(s * pitch, s * pitch + seg)
                    a_refs[d][b * nslab + c, dst, :] = a[src]
                    b_refs[d][b * nslab + c, dst, :] = bq[src]

    nch = bt * nslab
    zeros = tuple(jnp.zeros((SUBLANE, LANE), F32) for _ in range(nch))
    ones = tuple(jnp.ones((SUBLANE, LANE), F32) for _ in range(nch))

    def step(j, carry):
        out = []
        for d, t in ((0, j), (1, seg - 1 - j)):
            h_prev, p_prev = carry[2 * d], carry[2 * d + 1]
            h_new, p_new = [], []
            idx = pl.ds(t, SUBLANE, stride=pitch)
            for ch in range(nch):
                a = a_refs[d].at[ch][idx, :]
                h = a * h_prev[ch] + b_refs[d].at[ch][idx, :]
                p = a * p_prev[ch]
                b_refs[d].at[ch][idx, :] = h
                a_refs[d].at[ch][idx, :] = p
                h_new.append(h)
                p_new.append(p)
            out += [tuple(h_new), tuple(p_new)]
        return tuple(out)

    h_f, p_f, h_b, p_b = lax.fori_loop(0, seg, step, (zeros, ones, zeros, ones), unroll=2)

    for b in range(bt):
        for c in range(nslab):
            ch = b * nslab + c
            lanes = slice(c * LANE, (c + 1) * LANE)
            row_of = lambda v, s: v[s:s + 1, :]
            if has_h0:
                enter_f, enter_b = h0_ref[b, 0:1, lanes], h0_ref[b, 1:2, lanes]
            else:
                enter_f = enter_b = jnp.zeros((1, LANE), F32)
            ent_f, ent_b = [None] * SUBLANE, [None] * SUBLANE
            for s in range(SUBLANE):
                ent_f[s] = enter_f
                enter_f = row_of(h_f[ch], s) + row_of(p_f[ch], s) * enter_f
            for s in reversed(range(SUBLANE)):
                ent_b[s] = enter_b
                enter_b = row_of(h_b[ch], s) + row_of(p_b[ch], s) * enter_b
            if emit_last:
                hl_ref[b, 0:1, lanes] = enter_f
                hl_ref[b, 1:2, lanes] = enter_b
            for s in range(SUBLANE):
                src = slice(s * pitch, s * pitch + seg)
                fwd = b_refs[0][ch, src, :] + a_refs[0][ch, src, :] * ent_f[s]
                bwd = b_refs[1][ch, src, :] + a_refs[1][ch, src, :] * ent_b[s]
                hs_ref[b * seq + s * seg:b * seq + (s + 1) * seg, lanes] = fwd + bwd


def _rglru(layer, xr, row_start, nseq, seq, conv_w, conv_b, w_gate, b_gate, lam, h0=None, emit_last=False):
    bt = 4
    dg = RNN_GROUP
    ng = D_RNN // dg
    rows = bt * seq
    assert nseq % bt == 0 and row_start % rows == 0
    off = row_start // rows
    in_specs = [
        pl.BlockSpec((rows, dg), lambda bi, g: (off + bi, g)),
        pl.BlockSpec((None, CONV_W, dg), lambda bi, g: (layer, 0, g)),
        pl.BlockSpec((None, 1, dg), lambda bi, g: (layer, 0, g)),
        pl.BlockSpec((None, None, dg, 4 * dg), lambda bi, g: (layer, g, 0, 0)),
        pl.BlockSpec((None, None, 1, 4 * dg), lambda bi, g: (layer, g, 0, 0)),
        pl.BlockSpec((None, 2, dg), lambda bi, g: (layer, 0, g)),
    ]
    args = [xr, conv_w, conv_b, w_gate, b_gate, lam]
    if h0 is not None:
        in_specs.append(pl.BlockSpec((bt, None, 2, dg), lambda bi, g: (bi, layer, 0, g)))
        args.append(h0)
    out_specs = [pl.BlockSpec((rows, dg), lambda bi, g: (bi, g))]
    out_shape = [jax.ShapeDtypeStruct((nseq * seq, D_RNN), F32)]
    if emit_last:
        out_specs.append(pl.BlockSpec((bt, 2, dg), lambda bi, g: (bi, 0, g)))
        out_shape.append(jax.ShapeDtypeStruct((nseq, 2, D_RNN), F32))
    return pl.pallas_call(
        functools.partial(_rglru_kernel, seq=seq, bt=bt, has_h0=h0 is not None, emit_last=emit_last),
        grid=(nseq // bt, ng),
        in_specs=in_specs,
        out_specs=out_specs,
        out_shape=out_shape,
        scratch_shapes=[pltpu.VMEM((bt * (dg // LANE), SUBLANE * _segment_pitch(seq // SUBLANE), LANE), F32)
                        for _ in range(4)],
        compiler_params=_params("arbitrary", "arbitrary"),
        name=f"rglru_{seq}",
    )(*args)


def _attn_kernel(*refs, past, tq, rope):
    it = iter(refs)
    cq_ref, ckv_ref, kr_ref = (next(it) for _ in range(3))
    if rope:
        krr_ref, pckv_ref, pkr_ref, cos_ref, sin_ref = (next(it) for _ in range(5))
    wq_ref = next(it)
    wqr_ref = next(it) if rope else None
    wk_ref, wv_ref = next(it), next(it)
    o_ref = next(it)
    k_scr, v_scr = next(it), next(it)
    qi = pl.program_id(1)
    scale = 1.0 / math.sqrt(QK_NOPE + QK_ROPE)

    def fill(rows, ckv, k_rope):
        c = ckv.astype(BF16)
        k = _dot(c, wk_ref[...]) + jnp.tile(k_rope, (1, N_HEADS))
        k_scr[rows, :] = k.astype(BF16)
        v_scr[rows, :] = _dot(c, wv_ref[...]).astype(BF16)

    @pl.when(qi == 0)
    def _():
        seq = ckv_ref.shape[0]
        if rope:
            fill(pl.ds(0, past), pckv_ref[...], pkr_ref[...])
            fill(pl.ds(past, seq), ckv_ref[...],
                 kr_ref[...] * cos_ref[...] + krr_ref[...] * sin_ref[...])
        else:
            fill(pl.ds(0, seq), ckv_ref[...], kr_ref[...])

    cq = cq_ref[...]
    qa = _dot(cq, wq_ref[...])
    if rope:
        qb = _dot(cq, wqr_ref[...])
        q0 = pl.multiple_of(qi * tq, tq)
        cos_t = cos_ref[pl.ds(q0, tq), :]
        sin_t = sin_ref[pl.ds(q0, tq), :]
    for j in range(N_HEADS // 2):
        pair = None
        for h in (2 * j, 2 * j + 1):
            cols = slice(h * HEAD_PAD, (h + 1) * HEAD_PAD)
            qh = qa[:, cols]
            if rope:
                qh = qh * cos_t + qb[:, cols] * sin_t
            s = lax.dot_general(qh.astype(BF16), k_scr[:, cols], (((1,), (1,)), ((), ())),
                                preferred_element_type=F32)
            e = jnp.exp((s - jnp.max(s, axis=-1, keepdims=True)) * scale)
            denom = jnp.sum(e, axis=-1, keepdims=True)
            o = _dot(e.astype(BF16), v_scr[:, cols]) * (1.0 / denom)
            pair = o if pair is None else pair + o
        o_ref[:, j * LANE:(j + 1) * LANE] = pair.astype(BF16)


def _attention(layer, cq, ckv, kr, row_start, nseq, seq, w_q, w_k, w_v, lat=None):
    rope = lat is not None
    tq = min(_ATT_Q_TILE, seq)
    assert seq % tq == 0 and row_start % seq == 0
    past = lat[1].shape[2] if rope else 0
    total = past + seq
    hp = N_HEADS * HEAD_PAD
    nq = seq // tq
    s_off, q_off = row_start // seq, row_start // tq
    full = lambda w: pl.BlockSpec((seq, w), lambda b, q: (s_off + b, 0))
    q_rows = lambda w: pl.BlockSpec((tq, w), lambda b, q: (q_off + b * nq + q, 0))
    in_specs = [q_rows(Q_LORA), full(KV_LORA), full(HEAD_PAD)]
    args = [cq, ckv, kr]
    if rope:
        krr, pckv, pkr, cos, sin, w_q_rot = lat
        in_specs += [full(HEAD_PAD),
                     pl.BlockSpec((None, None, past, KV_LORA), lambda b, q: (b, layer, 0, 0)),
                     pl.BlockSpec((None, None, past, HEAD_PAD), lambda b, q: (b, layer, 0, 0)),
                     _fixed_spec((seq, HEAD_PAD)), _fixed_spec((seq, HEAD_PAD))]
        args += [krr, pckv, pkr, cos, sin]
    in_specs.append(_fixed_spec((Q_LORA, hp), (layer,)))
    args.append(w_q)
    if rope:
        in_specs.append(_fixed_spec((Q_LORA, hp), (layer,)))
        args.append(w_q_rot)
    in_specs += [_fixed_spec((KV_LORA, hp), (layer,)), _fixed_spec((KV_LORA, hp), (layer,))]
    args += [w_k, w_v]
    return pl.pallas_call(
        functools.partial(_attn_kernel, past=past, tq=tq, rope=rope),
        grid=(nseq, nq),
        in_specs=in_specs,
        out_specs=pl.BlockSpec((tq, N_HEADS * V_HEAD), lambda b, q: (b * nq + q, 0)),
        out_shape=jax.ShapeDtypeStruct((nseq * seq, N_HEADS * V_HEAD), BF16),
        scratch_shapes=[pltpu.VMEM((total, hp), BF16), pltpu.VMEM((total, hp), BF16)],
        compiler_params=_params("arbitrary", "arbitrary"),
        name=f"attn_{seq}",
    )(*args)


def _mix_kernel(*refs, n_x, n_ctx_tiles):
    (hsc_ref, hsl_ref, atc_ref, atl_ref, m_ref, wygg_ref, wro_ref, wao_ref, wo_ref,
     g_ref, b_ref, o_ref, wro_bf, wao_bf, wo_bf) = refs[n_x:]
    _cast_once([(wro_ref, wro_bf), (wao_ref, wao_bf), (wo_ref, wo_bf)])
    is_ctx = pl.program_id(0) < n_ctx_tiles
    hs = jnp.where(is_ctx, hsc_ref[...], hsl_ref[...])
    at = jnp.where(is_ctx, atc_ref[...], atl_ref[...])
    x = _load_stream(refs[:n_x], n_ctx_tiles)
    h = (x * (1.0 + m_ref[1:2, :]) + m_ref[0:1, :]).astype(BF16)
    ygg = _dot(h, wygg_ref[...])
    u = (hs * jax.nn.gelu(ygg[:, :D_RNN])).astype(BF16)
    rnn_out = _dot(u, wro_bf[...])
    att_out = _dot(at, wao_bf[...])
    merged = (jax.nn.sigmoid(ygg[:, D_RNN:D_RNN + D_MODEL]) * rnn_out
              + jax.nn.sigmoid(ygg[:, D_RNN + D_MODEL:]) * att_out)
    m = _dot(merged.astype(BF16), wo_bf[...])
    o_ref[...] = _layer_norm(ALPHA * x + m_ref[2:3, :] * m, g_ref[...], b_ref[...])


def _mix(layer, x, hs_ctx, hs_lat, at_ctx, at_lat, mods, w_ygg, w_ro, w_ao, w_o, ln_g, ln_b, n_ctx_rows,
         rows_per_latent):
    tm = _ROW_TILE
    nct = n_ctx_rows // tm
    x_specs, xs = _stream_specs(x, tm, n_ctx_rows)
    n_tok = sum(a.shape[0] for a in xs)
    row = pl.BlockSpec((tm, D_MODEL), lambda i: (i, 0))
    ctx_row = pl.BlockSpec((tm, D_MODEL), lambda i: (jnp.minimum(i, nct - 1), 0))
    lat_row = pl.BlockSpec((tm, D_MODEL), lambda i: (jnp.maximum(i - nct, 0), 0))
    sq = (D_MODEL, D_MODEL)
    return pl.pallas_call(
        functools.partial(_mix_kernel, n_x=len(xs), n_ctx_tiles=nct),
        grid=(n_tok // tm,),
        in_specs=x_specs + [ctx_row, lat_row, ctx_row, lat_row,
                  _mod_spec(layer, tm, n_ctx_rows, rows_per_latent),
                  _fixed_spec((D_MODEL, D_RNN + 2 * D_MODEL), (layer,)),
                  _fixed_spec(sq, (layer,)), _fixed_spec(sq, (layer,)), _fixed_spec(sq, (layer,)),
                  _fixed_spec((1, D_MODEL), (layer,)), _fixed_spec((1, D_MODEL), (layer,))],
        out_specs=row,
        out_shape=jax.ShapeDtypeStruct((n_tok, D_MODEL), F32),
        scratch_shapes=[pltpu.VMEM(sq, BF16) for _ in range(3)],
        compiler_params=_params("arbitrary"),
        name="mix_out",
    )(*xs, hs_ctx, hs_lat, at_ctx, at_lat, mods, w_ygg, w_ro, w_ao, w_o, ln_g, ln_b)


def _ffn_kernel(x_ref, m_ref, w1_ref, w3_ref, w2_ref, g_ref, b_ref, o_ref):
    x = x_ref[...]
    h = (x * (1.0 + m_ref[4:5, :]) + m_ref[3:4, :]).astype(BF16)
    f = None
    for c in range(D_FF // _FF_CHUNK):
        cols = slice(c * _FF_CHUNK, (c + 1) * _FF_CHUNK)
        act = (jax.nn.silu(_dot(h, w1_ref[:, cols])) * _dot(h, w3_ref[:, cols])).astype(BF16)
        y = _dot(act, w2_ref[cols, :])
        f = y if f is None else f + y
    o_ref[...] = _layer_norm(ALPHA * x + m_ref[5:6, :] * f, g_ref[...], b_ref[...])


def _ffn_dense(layer, x, mods, w1, w3, w2, ln_g, ln_b, n_ctx_rows, rows_per_latent):
    n_tok = x.shape[0]
    tm = _ROW_TILE
    j = layer // 2
    row = pl.BlockSpec((tm, D_MODEL), lambda i: (i, 0))
    return pl.pallas_call(
        _ffn_kernel,
        grid=(n_tok // tm,),
        in_specs=[row, _mod_spec(layer, tm, n_ctx_rows, rows_per_latent),
                  _fixed_spec((D_MODEL, D_FF), (j,)), _fixed_spec((D_MODEL, D_FF), (j,)),
                  _fixed_spec((D_FF, D_MODEL), (j,)),
                  _fixed_spec((1, D_MODEL), (layer,)), _fixed_spec((1, D_MODEL), (layer,))],
        out_specs=row,
        out_shape=jax.ShapeDtypeStruct((n_tok, D_MODEL), F32),
        compiler_params=_params("arbitrary"),
        name="ffn_dense",
    )(x, mods, w1, w3, w2, ln_g, ln_b)


def _split_bf16(a):
    hi = a.astype(BF16)
    return hi, (a - hi.astype(F32)).astype(BF16)


def _route_kernel(x_ref, m_ref, wr_ref, br_ref, h_ref, ids_ref, gw_ref, cnt_ref, run_scr, tri_scr):
    tm = x_ref.shape[0]

    @pl.when(pl.program_id(0) == 0)
    def _():
        run_scr[...] = jnp.zeros_like(run_scr)
        r = lax.broadcasted_iota(jnp.int32, (tm, tm), 0)
        c = lax.broadcasted_iota(jnp.int32, (tm, tm), 1)
        tri_scr[...] = jnp.where(c < r, 1.0, 0.0).astype(BF16)

    h = x_ref[...] * (1.0 + m_ref[4:5, :]) + m_ref[3:4, :]
    h_ref[...] = h
    h_hi, h_lo = _split_bf16(h)

    w_hi, w_lo = _split_bf16(wr_ref[...])
    logits = _dot(h_hi, w_hi) + (_dot(h_hi, w_lo) + _dot(h_lo, w_hi)) + br_ref[...]
    lane = lax.broadcasted_iota(jnp.int32, logits.shape, 1)
    big = jnp.int32(N_EXPERTS)
    v1 = jnp.max(logits, axis=-1, keepdims=True)
    i1 = jnp.min(jnp.where(logits == v1, lane, big), axis=-1, keepdims=True)
    rest = jnp.where(lane == i1, -jnp.inf, logits)
    v2 = jnp.max(rest, axis=-1, keepdims=True)
    i2 = jnp.min(jnp.where(rest == v2, lane, big), axis=-1, keepdims=True)
    e2 = jnp.exp(v2 - v1)
    g1 = 1.0 / (1.0 + e2)
    g2 = e2 / (1.0 + e2)

    m1, m2 = lane == i1, lane == i2
    member = jnp.where(m1, 1.0, 0.0) + jnp.where(m2, 1.0, 0.0)
    before = _dot(tri_scr[...], member.astype(BF16)) + run_scr[...]
    r1 = jnp.sum(jnp.where(m1, before, 0.0), axis=-1, keepdims=True).astype(jnp.int32)
    r2 = jnp.sum(jnp.where(m2, before, 0.0), axis=-1, keepdims=True).astype(jnp.int32)
    run_scr[...] += jnp.sum(member, axis=0, keepdims=True)

    ids_ref[...] = jnp.where(lane == 0, i1, jnp.where(lane == 1, i2, jnp.where(lane == 2, r1, r2)))
    gw_ref[...] = jnp.where(lane == 0, g1, g2)
    cnt_ref[...] = run_scr[...].astype(jnp.int32)


def _moe_route(layer, x, mods, w_router, b_router, n_ctx_rows, rows_per_latent):
    n_tok = x.shape[0]
    tm = _ROW_TILE
    j = layer // 2
    row = lambda w: pl.BlockSpec((tm, w), lambda i: (i, 0))
    return pl.pallas_call(
        _route_kernel,
        grid=(n_tok // tm,),
        in_specs=[row(D_MODEL), _mod_spec(layer, tm, n_ctx_rows, rows_per_latent),
                  _fixed_spec((D_MODEL, N_EXPERTS), (j,)), _fixed_spec((1, N_EXPERTS), (j,))],
        out_specs=[row(D_MODEL), row(N_EXPERTS), row(N_EXPERTS),
                   pl.BlockSpec((1, N_EXPERTS), lambda i: (0, 0))],
        out_shape=[jax.ShapeDtypeStruct((n_tok, D_MODEL), F32),
                   jax.ShapeDtypeStruct((n_tok, N_EXPERTS), jnp.int32),
                   jax.ShapeDtypeStruct((n_tok, N_EXPERTS), F32),
                   jax.ShapeDtypeStruct((1, N_EXPERTS), jnp.int32)],
        scratch_shapes=[pltpu.VMEM((1, N_EXPERTS), F32), pltpu.VMEM((tm, tm), BF16)],
        compiler_params=_params("arbitrary"),
        name="moe_route",
    )(x, mods, w_router, b_router)


def _tables_kernel(er_ref, g_ref, cnt_ref, exp_ref, on_ref, slot_ref, start_scr, *, n_tok, tm):
    n_tiles = exp_ref.shape[0]
    ends = []
    total = jnp.int32(0)
    for e in range(N_EXPERTS):
        start_scr[e] = total
        total = total + ((cnt_ref[0, e] + (tm - 1)) // tm) * tm
        ends.append(total)

    first_lane = lax.broadcasted_iota(jnp.int32, (1, LANE), 1) == 0
    slot_ref[...] = jnp.broadcast_to(jnp.where(first_lane, jnp.float32(n_tok), 0.0), slot_ref.shape)

    def place(t, carry):
        tok = t.astype(F32)
        for k in range(TOP_K):
            slot = start_scr[er_ref[k * n_tok + t]] + er_ref[(TOP_K + k) * n_tok + t]
            slot_ref[pl.ds(slot, 1), :] = jnp.where(first_lane, tok, g_ref[k * n_tok + t])
        return carry

    lax.fori_loop(0, n_tok, place, 0, unroll=8)

    def tiles(i, carry):
        first = i * tm
        owner = jnp.int32(0)
        for e in range(N_EXPERTS):
            owner = owner + jnp.where(first >= ends[e], 1, 0)
        exp_ref[i] = jnp.minimum(owner, N_EXPERTS - 1)
        on_ref[i] = jnp.where(first < total, 1, 0)
        return carry

    lax.fori_loop(0, n_tiles, tiles, 0)


def _slot_tables(ids, gw, counts, n_tok):
    tm = _MOE_SLOT_TILE
    n_slots = TOP_K * n_tok + N_EXPERTS * tm
    n_tiles = n_slots // tm
    smem = pl.BlockSpec(memory_space=pltpu.SMEM)
    vmem = pl.BlockSpec(memory_space=pltpu.VMEM)
    tile_exp, tile_on, slot_rows = pl.pallas_call(
        functools.partial(_tables_kernel, n_tok=n_tok, tm=tm),
        in_specs=[smem, smem, smem],
        out_specs=[smem, smem, vmem],
        out_shape=[jax.ShapeDtypeStruct((n_tiles,), jnp.int32), jax.ShapeDtypeStruct((n_tiles,), jnp.int32),
                   jax.ShapeDtypeStruct((n_slots, LANE), F32)],
        scratch_shapes=[pltpu.SMEM((N_EXPERTS,), jnp.int32)],
        compiler_params=pltpu.CompilerParams(vmem_limit_bytes=VMEM_LIMIT),
        name="moe_tables",
    )(ids[:, :2 * TOP_K].T.reshape(-1), gw[:, :TOP_K].T.reshape(-1), counts)
    return tile_exp, tile_on, slot_rows[:, 0].astype(jnp.int32), slot_rows


def _moe_up_kernel(exp_ref, on_ref, tok_ref, h_ref, w1_ref, w3_ref, act_ref, xg0_scr, xg1_scr, *,
                   n_tok, n_tiles):
    t = pl.program_id(0)
    tm = xg0_scr.shape[0]

    def gather_row(tile, r, dst):
        src = jnp.minimum(tok_ref[tile * tm + r], n_tok - 1)
        dst[pl.ds(r, 1), :] = h_ref[pl.ds(src, 1), :]

    @pl.when(t == 0)
    def _():
        def body(r, carry):
            gather_row(0, r, xg0_scr)
            return carry

        lax.fori_loop(0, tm, body, 0, unroll=8)

    for parity, (cur, nxt) in enumerate(((xg0_scr, xg1_scr), (xg1_scr, xg0_scr))):
        @pl.when(jnp.logical_and(on_ref[t] == 1, t % 2 == parity))
        def _():
            h = cur[...].astype(BF16)
            nxt_tile = jnp.minimum(t + 1, n_tiles - 1)
            for r in range(tm):
                gather_row(nxt_tile, r, nxt)
            act_ref[...] = (jax.nn.silu(_dot(h, w1_ref[...])) * _dot(h, w3_ref[...])).astype(BF16)

    @pl.when(on_ref[t] == 0)
    def _():
        act_ref[...] = jnp.zeros_like(act_ref)


def _moe_up(layer, tables, h, w1, w3):
    tile_exp, tile_on, slot_tok, _ = tables
    n_tok = h.shape[0]
    tm = _MOE_SLOT_TILE
    n_tiles = tile_exp.shape[0]
    j = layer // 2
    wspec = pl.BlockSpec((None, None, D_MODEL, D_EXPERT), lambda t, ex, on, tk: (j, ex[t], 0, 0),
                         pipeline_mode=pl.Buffered(1))
    return pl.pallas_call(
        functools.partial(_moe_up_kernel, n_tok=n_tok, n_tiles=n_tiles),
        grid_spec=pltpu.PrefetchScalarGridSpec(
            num_scalar_prefetch=3,
            grid=(n_tiles,),
            in_specs=[pl.BlockSpec((n_tok, D_MODEL), lambda t, ex, on, tk: (0, 0),
                                   pipeline_mode=pl.Buffered(1)),
                      wspec, wspec],
            out_specs=pl.BlockSpec((tm, D_EXPERT), lambda t, ex, on, tk: (t, 0)),
            scratch_shapes=[pltpu.VMEM((tm, D_MODEL), F32), pltpu.VMEM((tm, D_MODEL), F32)],
        ),
        out_shape=jax.ShapeDtypeStruct((n_tiles * tm, D_EXPERT), BF16),
        compiler_params=_params("arbitrary"),
        name="moe_up",
    )(tile_exp, tile_on, slot_tok, h, w1, w3)


def _moe_down_kernel(exp_ref, on_ref, tok_ref, act_ref, gate_ref, w2_ref, x_ref, m_ref, g_ref, b_ref,
                     *rest, n_tiles, ln_rows, n_ctx_ln_tiles, split):
    o_refs, (acc_scr, y0_scr, y1_scr) = rest[:-3], rest[-3:]
    t = pl.program_id(0)
    tm = y0_scr.shape[0]

    @pl.when(t == 0)
    def _():
        acc_scr[...] = jnp.zeros_like(acc_scr)

    def project(dst):
        dst[...] = _dot(act_ref[...], w2_ref[...]) * gate_ref[:, 1:2]

    def scatter_group(tile, base, ys):
        toks = [tok_ref[tile * tm + base + u] for u in range(SUBLANE)]
        rows = [acc_scr[pl.ds(toks[u], 1), :] for u in range(SUBLANE)]
        for u in range(SUBLANE):
            acc_scr[pl.ds(toks[u], 1), :] = rows[u] + ys[u:u + 1, :]

    last = n_tiles - 1
    project_now = jnp.logical_and(t < n_tiles, on_ref[jnp.minimum(t, last)] == 1)
    add_prev = jnp.logical_and(jnp.logical_and(t >= 1, t <= n_tiles), on_ref[jnp.clip(t - 1, 0, last)] == 1)
    for parity, (cur, prv) in enumerate(((y0_scr, y1_scr), (y1_scr, y0_scr))):
        mine = t % 2 == parity

        @pl.when(jnp.logical_and(mine, jnp.logical_and(project_now, add_prev)))
        def _():
            project(cur)
            for g in range(tm // SUBLANE):
                scatter_group(t - 1, g * SUBLANE, prv[g * SUBLANE:(g + 1) * SUBLANE, :])

        @pl.when(jnp.logical_and(mine, jnp.logical_and(project_now, jnp.logical_not(add_prev))))
        def _():
            project(cur)

        @pl.when(jnp.logical_and(mine, jnp.logical_and(add_prev, jnp.logical_not(project_now))))
        def _():
            def body(g, carry):
                base = pl.multiple_of(g * SUBLANE, SUBLANE)
                scatter_group(t - 1, base, prv[pl.ds(base, SUBLANE), :])
                return carry

            lax.fori_loop(0, tm // SUBLANE, body, 0)

    @pl.when(t > n_tiles)
    def _():
        r0 = pl.multiple_of((t - n_tiles - 1) * ln_rows, ln_rows)
        f = acc_scr[pl.ds(r0, ln_rows), :]
        out = _layer_norm(ALPHA * x_ref[...] + m_ref[5:6, :] * f, g_ref[...], b_ref[...])
        if split:
            is_ctx = t - n_tiles - 1 < n_ctx_ln_tiles

            @pl.when(is_ctx)
            def _():
                o_refs[0][...] = out

            @pl.when(jnp.logical_not(is_ctx))
            def _():
                o_refs[1][...] = out
        else:
            o_refs[0][...] = out


def _moe_down(layer, tables, act, x, mods, w2, ln_g, ln_b, n_ctx_rows, rows_per_latent, split):
    tile_exp, tile_on, slot_tok, slot_gate = tables
    n_tok = x.shape[0]
    tm = _MOE_SLOT_TILE
    tl = _MOE_LN_TILE
    n_tiles = tile_exp.shape[0]
    j = layer // 2
    slot = lambda t: jnp.minimum(t, n_tiles - 1)
    ln = lambda t: jnp.maximum(t - n_tiles - 1, 0)
    n_ctx_tiles, per = n_ctx_rows // tl, rows_per_latent // tl

    def mod_index(t, *_):
        i = ln(t)
        return (layer, jnp.where(i < n_ctx_tiles, 0, 1 + (i - n_ctx_tiles) // per), 0, 0)

    fixed = lambda shape, lead: pl.BlockSpec((None,) + shape, lambda *_: (lead,) + (0,) * len(shape),
                                             pipeline_mode=pl.Buffered(1))
    if split:
        out_specs = [pl.BlockSpec((tl, D_MODEL), lambda t, *_: (jnp.minimum(ln(t), n_ctx_tiles - 1), 0)),
                     pl.BlockSpec((tl, D_MODEL), lambda t, *_: (jnp.maximum(ln(t) - n_ctx_tiles, 0), 0))]
        out_shape = [jax.ShapeDtypeStruct((n_ctx_rows, D_MODEL), F32),
                     jax.ShapeDtypeStruct((n_tok - n_ctx_rows, D_MODEL), F32)]
    else:
        out_specs = pl.BlockSpec((tl, D_MODEL), lambda t, *_: (ln(t), 0))
        out_shape = jax.ShapeDtypeStruct((n_tok, D_MODEL), F32)
    return pl.pallas_call(
        functools.partial(_moe_down_kernel, n_tiles=n_tiles, ln_rows=tl, n_ctx_ln_tiles=n_ctx_tiles,
                          split=split),
        grid_spec=pltpu.PrefetchScalarGridSpec(
            num_scalar_prefetch=3,
            grid=(n_tiles + 1 + n_tok // tl,),
            in_specs=[pl.BlockSpec((tm, D_EXPERT), lambda t, ex, *_: (slot(t), 0)),
                      pl.BlockSpec((tm, LANE), lambda t, ex, *_: (slot(t), 0)),
                      pl.BlockSpec((None, None, D_EXPERT, D_MODEL), lambda t, ex, *_: (j, ex[slot(t)], 0, 0)),
                      pl.BlockSpec((tl, D_MODEL), lambda t, *_: (ln(t), 0)),
                      pl.BlockSpec((None, None, 6, D_MODEL), mod_index),
                      fixed((1, D_MODEL), layer), fixed((1, D_MODEL), layer)],
            out_specs=out_specs,
            scratch_shapes=[pltpu.VMEM((n_tok + SUBLANE, D_MODEL), F32), pltpu.VMEM((tm, D_MODEL), F32),
                            pltpu.VMEM((tm, D_MODEL), F32)],
        ),
        out_shape=out_shape,
        compiler_params=_params("arbitrary"),
        name="moe_down",
    )(tile_exp, tile_on, slot_tok, act, slot_gate, w2, x, mods, ln_g, ln_b)


def _ffn_moe(layer, x, mods, w_router, b_router, w1, w3, w2, ln_g, ln_b, n_ctx_rows, rows_per_latent,
             split=False):
    h, ids, gw, counts = _moe_route(layer, x, mods, w_router, b_router, n_ctx_rows, rows_per_latent)
    tables = _slot_tables(ids, gw, counts, x.shape[0])
    act = _moe_up(layer, tables, h, w1, w3)
    return _moe_down(layer, tables, act, x, mods, w2, ln_g, ln_b, n_ctx_rows, rows_per_latent, split)


def _rot_cols(w):
    ws = w.reshape(w.shape[:-1] + (2, 2, ROPE_AXIS // 2))
    return jnp.stack([-ws[..., 1, :], ws[..., 0, :]], axis=-2).reshape(w.shape)


def _pad_rope_cols(w):
    return jnp.pad(w, ((0, 0),) * (w.ndim - 1) + ((ROPE_OFF, HEAD_PAD - ROPE_OFF - QK_ROPE),))


def _rope_tables(rows):
    r = jnp.repeat(jnp.arange(rows, dtype=F32), GRID_W)
    col = jnp.tile(jnp.arange(GRID_W, dtype=F32), rows)
    freqs = ROPE_THETA ** (-jnp.arange(0, ROPE_AXIS, 2, dtype=F32) / ROPE_AXIS)
    ar = r[:, None] * freqs
    ac = col[:, None] * freqs
    ang = jnp.concatenate([ar, ar, ac, ac], axis=-1)
    cos, sin = jnp.cos(ang), jnp.sin(ang)
    n = cos.shape[0]
    cos_p = jnp.concatenate([jnp.ones((n, ROPE_OFF), F32), cos,
                             jnp.zeros((n, HEAD_PAD - ROPE_OFF - QK_ROPE), F32)], axis=-1)
    return cos_p, _pad_rope_cols(sin)


def _prepared_weights(w_in, rg_wa, rg_ba, rg_wx, rg_bx, w_qb, w_kvb):
    bf = lambda a: a.astype(BF16)
    w_kr = w_in[:, :, _COL_KR:_COL_KR + QK_ROPE]
    w_s = jnp.concatenate([w_in[:, :, _COL_Q:_COL_KR], _pad_rope_cols(w_kr),
                           _pad_rope_cols(_rot_cols(w_kr))], axis=-1)
    w_x = w_in[:, :, :D_RNN]
    w_ygg = jnp.concatenate([w_in[:, :, D_RNN:2 * D_RNN], w_in[:, :, _COL_GATE:]], axis=-1)

    wq = w_qb.reshape(DEPTH, Q_LORA, N_HEADS, QK_NOPE + QK_ROPE)
    q_nope, q_rope = wq[..., :QK_NOPE], wq[..., QK_NOPE:]
    tail = jnp.zeros((DEPTH, Q_LORA, N_HEADS, HEAD_PAD - QK_NOPE - QK_ROPE), F32)
    hp = N_HEADS * HEAD_PAD
    w_q = jnp.concatenate([q_nope, q_rope, tail], axis=-1).reshape(DEPTH, Q_LORA, hp)
    w_q_rot = jnp.concatenate([jnp.zeros_like(q_nope), _rot_cols(q_rope), tail], axis=-1)
    w_q_rot = w_q_rot.reshape(DEPTH, Q_LORA, hp)
    wkv = w_kvb.reshape(DEPTH, KV_LORA, N_HEADS, QK_NOPE + V_HEAD)
    k_nope, v = wkv[..., :QK_NOPE], wkv[..., QK_NOPE:]
    w_k = jnp.concatenate([k_nope, jnp.zeros((DEPTH, KV_LORA, N_HEADS, HEAD_PAD - QK_NOPE), F32)], axis=-1)
    w_k = w_k.reshape(DEPTH, KV_LORA, hp)
    vz = jnp.zeros_like(v)
    odd = (jnp.arange(N_HEADS) % 2 == 1)[None, None, :, None]
    w_v = jnp.where(odd, jnp.concatenate([vz, v], axis=-1), jnp.concatenate([v, vz], axis=-1))
    w_v = w_v.reshape(DEPTH, KV_LORA, hp)

    nblk = RNN_GROUP // RNN_BW
    ng = D_RNN // RNN_GROUP
    eye = jnp.eye(nblk, dtype=F32)

    def block_diag(w):
        wg = w.reshape(DEPTH, ng, nblk, RNN_BW, RNN_BW)
        return jnp.einsum('lgnjk,nm->lgnjmk', wg, eye).reshape(DEPTH, ng, RNN_GROUP, RNN_GROUP)

    w_gate = jnp.concatenate([block_diag(rg_wa[:, 0]), block_diag(rg_wx[:, 0]),
                              block_diag(rg_wa[:, 1]), block_diag(rg_wx[:, 1])], axis=-1)
    grp = lambda b: b.reshape(DEPTH, ng, 1, RNN_GROUP)
    b_gate = jnp.concatenate([grp(rg_ba[:, 0]), grp(rg_bx[:, 0]), grp(rg_ba[:, 1]), grp(rg_bx[:, 1])],
                             axis=-1)
    return dict(w_x=bf(w_x), w_s=bf(w_s), w_ygg=bf(w_ygg), w_q=bf(w_q), w_q_rot=bf(w_q_rot), w_k=bf(w_k), w_v=bf(w_v),
                w_gate=bf(w_gate), b_gate=b_gate)


def kernel(x_prompt, x_sample, cache_ckv, cache_krope, state_rnn, c, c_ctx, w_ada, b_ada, w_in, conv_w, conv_b, rg_wa, rg_ba, rg_wx, rg_bx, rg_lambda, w_rnn_out, q_norm_g, w_qb, kv_norm_g, w_kvb, w_attn_out, w_out, ln1_g, ln1_b, ln2_g, ln2_b, w1_dense, w3_dense, w2_dense, w_router, b_router, w1_exp, w3_exp, w2_exp):
    nb, seq, _ = x_prompt.shape
    db, dseq, _ = x_sample.shape
    n_ctx = nb * seq
    n_lat = db * dseq
    assert n_ctx % _ROW_TILE == 0 and dseq % _ROW_TILE == 0 and db + 1 <= 8
    assert DEPTH % 2 == 0
    bf = lambda a: a.astype(BF16)
    vec = lambda a: a.reshape(a.shape[0], 1, a.shape[-1])

    cond8 = jnp.concatenate([c_ctx[None, :], c, jnp.zeros((8 - 1 - db, D_MODEL), F32)], axis=0)
    mods = _ada_mods(cond8, w_ada, b_ada)
    cos_p, sin_p = _rope_tables(dseq // GRID_W)
    past_kr = _pad_rope_cols(cache_krope)
    pw = _prepared_weights(w_in, rg_wa, rg_ba, rg_wx, rg_bx, w_qb, w_kvb)
    w1d, w3d, w2d = bf(w1_dense), bf(w3_dense), bf(w2_dense)
    w1e, w3e, w2e = bf(w1_exp), bf(w3_exp), bf(w2_exp)
    qg, kvg, cb = vec(q_norm_g), vec(kv_norm_g), vec(conv_b)
    l1g, l1b, l2g, l2b = vec(ln1_g), vec(ln1_b), vec(ln2_g), vec(ln2_b)
    br = vec(b_router)
    tiles = (n_ctx, dseq)

    x = (x_prompt.reshape(n_ctx, D_MODEL), x_sample.reshape(n_lat, D_MODEL))
    new_ckv, new_kr, new_h = [], [], []
    for l in range(DEPTH):
        xr, cq, ckv, kr, krr = _in_proj(l, x, mods, pw['w_x'], pw['w_s'], qg, kvg, *tiles)
        new_ckv.append(ckv[:n_ctx].reshape(nb, seq, KV_LORA))
        new_kr.append(kr[:n_ctx, ROPE_OFF:ROPE_OFF + QK_ROPE].reshape(nb, seq, QK_ROPE))

        rnn_args = (conv_w, cb, pw['w_gate'], pw['b_gate'], rg_lambda)
        hs_ctx, h_last = _rglru(l, xr, 0, nb, seq, *rnn_args, emit_last=True)
        (hs_lat,) = _rglru(l, xr, n_ctx, db, dseq, *rnn_args, h0=state_rnn)
        new_h.append(h_last)

        att_w = (pw['w_q'], pw['w_k'], pw['w_v'])
        at_ctx = _attention(l, cq, ckv, kr, 0, nb, seq, *att_w)
        at_lat = _attention(l, cq, ckv, kr, n_ctx, db, dseq, *att_w,
                            lat=(krr, cache_ckv, past_kr, cos_p, sin_p, pw['w_q_rot']))

        x = _mix(l, x, hs_ctx, hs_lat, at_ctx, at_lat, mods, pw['w_ygg'], w_rnn_out, w_attn_out, w_out,
                 l1g, l1b, *tiles)
        if l % 2 == 0:
            x = _ffn_dense(l, x, mods, w1d, w3d, w2d, l2g, l2b, *tiles)
        else:
            x = _ffn_moe(l, x, mods, w_router, br, w1e, w3e, w2e, l2g, l2b, *tiles, split=l == DEPTH - 1)

    return (x[0].reshape(nb, seq, D_MODEL), x[1].reshape(db, dseq, D_MODEL),
            jnp.stack(new_ckv, axis=1), jnp.stack(new_kr, axis=1), jnp.stack(new_h, axis=1))
```

```python
import functools
import math

import jax
import jax.numpy as jnp
from jax import lax
from jax.experimental import pallas as pl
from jax.experimental.pallas import tpu as pltpu

F32 = jnp.float32
BF16 = jnp.bfloat16

D_MODEL = 1024
DEPTH = 4
GRID_W = 64
D_RNN = D_MODEL
RNN_BLOCKS = 16
RNN_BW = D_RNN // RNN_BLOCKS
CONV_W = 4
RG_C = 8.0
N_HEADS = 16
QK_NOPE = 64
QK_ROPE = 32
V_HEAD = 64
Q_LORA = 256
KV_LORA = 128
ROPE_AXIS = QK_ROPE // 2
ROPE_THETA = 10000.0
D_FF = 2816
N_EXPERTS = 8
TOP_K = 2
D_EXPERT = 1408
ALPHA = (2 * DEPTH) ** 0.25
LN_EPS = 1e-5
RMS_EPS = 1e-6

LANE = 128
SUBLANE = 8
HEAD_PAD = LANE
ROPE_OFF = QK_NOPE
RNN_GROUP = 256
_RNN_SEQ_TILE = 4
VMEM_LIMIT = 58 * 1024 * 1024

_ROW_TILE = 512
_ROUTE_TILE = 1024
_MOE_SLOT_TILE = 512
_MOE_LN_TILE = 256
_ATT_Q_TILE = 512
_FF_CHUNK = 1408

_COL_Q = 2 * D_RNN
_COL_KR = _COL_Q + Q_LORA + KV_LORA


def _dot(a, b):
    return jnp.dot(a, b, preferred_element_type=F32)


def _layer_norm(y, g, b):
    mu = jnp.mean(y, axis=-1, keepdims=True)
    d = y - mu
    var = jnp.mean(d * d, axis=-1, keepdims=True)
    return d * lax.rsqrt(var + LN_EPS) * g + b


def _rms_norm(y, g):
    return y * lax.rsqrt(jnp.mean(y * y, axis=-1, keepdims=True) + RMS_EPS) * g


def _params(*sem):
    return pltpu.CompilerParams(dimension_semantics=sem, vmem_limit_bytes=VMEM_LIMIT)


def _fixed_spec(shape, lead=(), tail=None):
    tail = (0,) * len(shape) if tail is None else tuple(tail)
    index = tuple(lead) + tail
    return pl.BlockSpec((None,) * len(lead) + tuple(shape), lambda *_: index,
                        pipeline_mode=pl.Buffered(1))


def _mod_spec(layer, tm, n_ctx_rows, rows_per_latent):
    n_ctx_tiles = n_ctx_rows // tm
    per = rows_per_latent // tm

    def index(i, *_):
        return (layer, jnp.where(i < n_ctx_tiles, 0, 1 + (i - n_ctx_tiles) // per), 0, 0)

    return pl.BlockSpec((None, None, 6, D_MODEL), index)


def _cast_once(pairs):
    @pl.when(pl.program_id(0) == 0)
    def _():
        for src, dst in pairs:
            dst[...] = src[...].astype(BF16)


def _stream_specs(x, tm, n_ctx_rows):
    if isinstance(x, tuple):
        nct = n_ctx_rows // tm
        return ([pl.BlockSpec((tm, D_MODEL), lambda i, *_: (jnp.minimum(i, nct - 1), 0)),
                 pl.BlockSpec((tm, D_MODEL), lambda i, *_: (jnp.maximum(i - nct, 0), 0))], list(x))
    return [pl.BlockSpec((tm, D_MODEL), lambda i, *_: (i, 0))], [x]


def _load_stream(refs, n_ctx_tiles):
    if len(refs) == 2:
        return jnp.where(pl.program_id(0) < n_ctx_tiles, refs[0][...], refs[1][...])
    return refs[0][...]


def _mods_kernel(c_ref, w_ref, b_ref, o_ref):
    s = jax.nn.silu(c_ref[...]).astype(BF16)
    o_ref[...] = _dot(s, w_ref[...].astype(BF16)) + b_ref[...]


def _ada_mods(cond8, w_ada, b_ada):
    out = pl.pallas_call(
        _mods_kernel,
        grid=(DEPTH, 6),
        in_specs=[
            pl.BlockSpec((8, D_MODEL), lambda l, j: (0, 0)),
            pl.BlockSpec((None, D_MODEL, D_MODEL), lambda l, j: (l, 0, j)),
            pl.BlockSpec((None, 1, D_MODEL), lambda l, j: (l, 0, j)),
        ],
        out_specs=pl.BlockSpec((None, 8, D_MODEL), lambda l, j: (l, 0, j)),
        out_shape=jax.ShapeDtypeStruct((DEPTH, 8, 6 * D_MODEL), F32),
        compiler_params=_params("arbitrary", "arbitrary"),
        name="ada_mods",
    )(cond8, w_ada, b_ada.reshape(DEPTH, 1, 6 * D_MODEL))
    return out.reshape(DEPTH, 8, 6, D_MODEL)


_S_COLS = Q_LORA + KV_LORA + 2 * HEAD_PAD


def _inproj_kernel(*refs, n_x, n_ctx_tiles):
    m_ref, wx_ref, ws_ref, qg_ref, kvg_ref, xr_ref, cq_ref, ckv_ref, kr_ref, krr_ref = refs[n_x:]
    x = _load_stream(refs[:n_x], n_ctx_tiles)
    h = (x * (1.0 + m_ref[1:2, :]) + m_ref[0:1, :]).astype(BF16)
    xr_ref[...] = _dot(h, wx_ref[...])
    s = _dot(h, ws_ref[...])
    cq_ref[...] = _rms_norm(s[:, :Q_LORA], qg_ref[...]).astype(BF16)
    ckv_ref[...] = _rms_norm(s[:, Q_LORA:Q_LORA + KV_LORA], kvg_ref[...])
    kr_ref[...] = s[:, Q_LORA + KV_LORA:Q_LORA + KV_LORA + HEAD_PAD]
    krr_ref[...] = s[:, Q_LORA + KV_LORA + HEAD_PAD:]


def _in_proj(layer, x, mods, w_x, w_s, q_g, kv_g, n_ctx_rows, rows_per_latent):
    tm = _ROW_TILE
    x_specs, xs = _stream_specs(x, tm, n_ctx_rows)
    n_tok = sum(a.shape[0] for a in xs)
    row = lambda w: pl.BlockSpec((tm, w), lambda i: (i, 0))
    return pl.pallas_call(
        functools.partial(_inproj_kernel, n_x=len(xs), n_ctx_tiles=n_ctx_rows // tm),
        grid=(n_tok // tm,),
        in_specs=x_specs + [
            _mod_spec(layer, tm, n_ctx_rows, rows_per_latent),
            _fixed_spec((D_MODEL, D_RNN), (layer,)),
            _fixed_spec((D_MODEL, _S_COLS), (layer,)),
            _fixed_spec((1, Q_LORA), (layer,)),
            _fixed_spec((1, KV_LORA), (layer,)),
        ],
        out_specs=[row(D_RNN), row(Q_LORA), row(KV_LORA), row(HEAD_PAD), row(HEAD_PAD)],
        out_shape=[
            jax.ShapeDtypeStruct((n_tok, D_RNN), F32),
            jax.ShapeDtypeStruct((n_tok, Q_LORA), BF16),
            jax.ShapeDtypeStruct((n_tok, KV_LORA), F32),
            jax.ShapeDtypeStruct((n_tok, HEAD_PAD), F32),
            jax.ShapeDtypeStruct((n_tok, HEAD_PAD), F32),
        ],
        compiler_params=_params("arbitrary"),
        name="in_proj",
    )(*xs, mods, w_x, w_s, q_g, kv_g)


def _segment_pitch(seg):
    assert seg % SUBLANE == 0
    return seg if (seg // SUBLANE) % 2 == 1 else seg + SUBLANE


def _rglru_kernel(*refs, seq, bt, has_h0, emit_last):
    it = iter(refs)
    x_ref, cw_ref, cb_ref, wg_ref, bg_ref, lam_ref = (next(it) for _ in range(6))
    h0_ref = next(it) if has_h0 else None
    hs_ref = next(it)
    hl_ref = next(it) if emit_last else None
    a_refs = (next(it), next(it))
    b_refs = (next(it), next(it))
    dg = RNN_GROUP
    nslab = dg // LANE
    seg = seq // SUBLANE
    pitch = _segment_pitch(seg)

    row = lax.broadcasted_iota(jnp.int32, (seq, dg), 0)
    decay_rate = [RG_C * jax.nn.softplus(-lam_ref[d:d + 1, :]) for d in range(2)]
    for b in range(bt):
        rows = slice(b * seq, (b + 1) * seq)
        x = x_ref[rows, :]
        xm1 = jnp.where(row >= 1, pltpu.roll(x, 1, 0), 0.0)
        xp1 = jnp.where(row < seq - 1, pltpu.roll(x, seq - 1, 0), 0.0)
        xp2 = jnp.where(row < seq - 2, pltpu.roll(x, seq - 2, 0), 0.0)
        xc = cb_ref[...] + xm1 * cw_ref[0:1, :]
        xc = xc + x * cw_ref[1:2, :]
        xc = xc + xp1 * cw_ref[2:3, :]
        xc = xc + xp2 * cw_ref[3:4, :]
        gates = _dot(xc.astype(BF16), wg_ref[...]) + bg_ref[...]
        for d in range(2):
            r = jax.nn.sigmoid(gates[:, (2 * d) * dg:(2 * d + 1) * dg])
            i = jax.nn.sigmoid(gates[:, (2 * d + 1) * dg:(2 * d + 2) * dg])
            neg_log_a = r * decay_rate[d]
            a = jnp.exp(-neg_log_a)
            y = jnp.tanh(neg_log_a) * (a * a + 1.0)
            bq = jnp.where(y > 0.0, y * lax.rsqrt(y), 0.0) * (i * xc)
            for s in range(SUBLANE):
                for c in range(nslab):
                    src = (slice(s * seg, (s + 1) * seg), slice(c * LANE, (c + 1) * LANE))
                    dst = slice(s * pitch, s * pitch + seg)
                    a_refs[d][b * nslab + c, dst, :] = a[src]
                    b_refs[d][b * nslab + c, dst, :] = bq[src]

    nch = bt * nslab
    zeros = tuple(jnp.zeros((SUBLANE, LANE), F32) for _ in range(nch))
    ones = tuple(jnp.ones((SUBLANE, LANE), F32) for _ in range(nch))

    def step(j, carry):
        out = []
        for d, t in ((0, j), (1, seg - 1 - j)):
            h_prev, p_prev = carry[2 * d], carry[2 * d + 1]
            h_new, p_new = [], []
            idx = pl.ds(t, SUBLANE, stride=pitch)
            for ch in range(nch):
                a = a_refs[d].at[ch][idx, :]
                h = a * h_prev[ch] + b_refs[d].at[ch][idx, :]
                p = a * p_prev[ch]
                b_refs[d].at[ch][idx, :] = h
                a_refs[d].at[ch][idx, :] = p
                h_new.append(h)
                p_new.append(p)
            out += [tuple(h_new), tuple(p_new)]
        return tuple(out)

    h_f, p_f, h_b, p_b = lax.fori_loop(0, seg, step, (zeros, ones, zeros, ones), unroll=2)

    for b in range(bt):
        for c in range(nslab):
            ch = b * nslab + c
            lanes = slice(c * LANE, (c + 1) * LANE)
            row_of = lambda v, s: v[s:s + 1, :]
            if has_h0:
                enter_f, enter_b = h0_ref[b, 0:1, lanes], h0_ref[b, 1:2, lanes]
            else:
                enter_f = enter_b = jnp.zeros((1, LANE), F32)
            ent_f, ent_b = [None] * SUBLANE, [None] * SUBLANE
            for s in range(SUBLANE):
                ent_f[s] = enter_f
                enter_f = row_of(h_f[ch], s) + row_of(p_f[ch], s) * enter_f
            for s in reversed(range(SUBLANE)):
                ent_b[s] = enter_b
                enter_b = row_of(h_b[ch], s) + row_of(p_b[ch], s) * enter_b
            if emit_last:
                hl_ref[b, 0:1, lanes] = enter_f
                hl_ref[b, 1:2, lanes] = enter_b
            for s in range(SUBLANE):
                src = slice(s * pitch, s * pitch + seg)
                fwd = b_refs[0][ch, src, :] + a_refs[0][ch, src, :] * ent_f[s]
                bwd = b_refs[1][ch, src, :] + a_refs[1][ch, src, :] * ent_b[s]
                hs_ref[b * seq + s * seg:b * seq + (s + 1) * seg, lanes] = fwd + bwd


def _rglru(layer, xr, row_start, nseq, seq, conv_w, conv_b, w_gate, b_gate, lam, h0=None, emit_last=False):
    bt = _RNN_SEQ_TILE
    dg = RNN_GROUP
    ng = D_RNN // dg
    rows = bt * seq
    assert nseq % bt == 0 and row_start % rows == 0
    off = row_start // rows
    in_specs = [
        pl.BlockSpec((rows, dg), lambda bi, g: (off + bi, g)),
        pl.BlockSpec((None, CONV_W, dg), lambda bi, g: (layer, 0, g)),
        pl.BlockSpec((None, 1, dg), lambda bi, g: (layer, 0, g)),
        pl.BlockSpec((None, None, dg, 4 * dg), lambda bi, g: (layer, g, 0, 0)),
        pl.BlockSpec((None, None, 1, 4 * dg), lambda bi, g: (layer, g, 0, 0)),
        pl.BlockSpec((None, 2, dg), lambda bi, g: (layer, 0, g)),
    ]
    args = [xr, conv_w, conv_b, w_gate, b_gate, lam]
    if h0 is not None:
        in_specs.append(pl.BlockSpec((bt, None, 2, dg), lambda bi, g: (bi, layer, 0, g)))
        args.append(h0)
    out_specs = [pl.BlockSpec((rows, dg), lambda bi, g: (bi, g))]
    out_shape = [jax.ShapeDtypeStruct((nseq * seq, D_RNN), F32)]
    if emit_last:
        out_specs.append(pl.BlockSpec((bt, 2, dg), lambda bi, g: (bi, 0, g)))
        out_shape.append(jax.ShapeDtypeStruct((nseq, 2, D_RNN), F32))
    return pl.pallas_call(
        functools.partial(_rglru_kernel, seq=seq, bt=bt, has_h0=h0 is not None, emit_last=emit_last),
        grid=(nseq // bt, ng),
        in_specs=in_specs,
        out_specs=out_specs,
        out_shape=out_shape,
        scratch_shapes=[pltpu.VMEM((bt * (dg // LANE), SUBLANE * _segment_pitch(seq // SUBLANE), LANE), F32)
                        for _ in range(4)],
        compiler_params=_params("arbitrary", "arbitrary"),
        name=f"rglru_{seq}",
    )(*args)


def _attn_kernel(*refs, past, tq, rope):
    it = iter(refs)
    cq_ref, ckv_ref, kr_ref = (next(it) for _ in range(3))
    if rope:
        krr_ref, pckv_ref, pkr_ref, cos_ref, sin_ref = (next(it) for _ in range(5))
    wq_ref = next(it)
    wqr_ref = next(it) if rope else None
    wk_ref, wv_ref = next(it), next(it)
    o_ref = next(it)
    k_scr, v_scr = next(it), next(it)
    qi = pl.program_id(1)
    scale = 1.0 / math.sqrt(QK_NOPE + QK_ROPE)

    def fill(rows, ckv, k_rope):
        c = ckv.astype(BF16)
        k = _dot(c, wk_ref[...]) + jnp.tile(k_rope, (1, N_HEADS))
        k_scr[rows, :] = k.astype(BF16)
        v_scr[rows, :] = _dot(c, wv_ref[...]).astype(BF16)

    @pl.when(qi == 0)
    def _():
        seq = ckv_ref.shape[0]
        if rope:
            fill(pl.ds(0, past), pckv_ref[...], pkr_ref[...])
            fill(pl.ds(past, seq), ckv_ref[...],
                 kr_ref[...] * cos_ref[...] + krr_ref[...] * sin_ref[...])
        else:
            fill(pl.ds(0, seq), ckv_ref[...], kr_ref[...])

    cq = cq_ref[...]
    qa = _dot(cq, wq_ref[...])
    if rope:
        qb = _dot(cq, wqr_ref[...])
        q0 = pl.multiple_of(qi * tq, tq)
        cos_t = cos_ref[pl.ds(q0, tq), :]
        sin_t = sin_ref[pl.ds(q0, tq), :]
    for j in range(N_HEADS // 2):
        pair = None
        for h in (2 * j, 2 * j + 1):
            cols = slice(h * HEAD_PAD, (h + 1) * HEAD_PAD)
            qh = qa[:, cols]
            if rope:
                qh = qh * cos_t + qb[:, cols] * sin_t
            s = lax.dot_general(qh.astype(BF16), k_scr[:, cols], (((1,), (1,)), ((), ())),
                                preferred_element_type=F32)
            e = jnp.exp((s - jnp.max(s, axis=-1, keepdims=True)) * scale)
            denom = jnp.sum(e, axis=-1, keepdims=True)
            o = _dot(e.astype(BF16), v_scr[:, cols]) * (1.0 / denom)
            pair = o if pair is None else pair + o
        o_ref[:, j * LANE:(j + 1) * LANE] = pair.astype(BF16)


def _attention(layer, cq, ckv, kr, row_start, nseq, seq, w_q, w_k, w_v, lat=None):
    rope = lat is not None
    tq = min(_ATT_Q_TILE, seq)
    assert seq % tq == 0 and row_start % seq == 0
    past = lat[1].shape[2] if rope else 0
    total = past + seq
    hp = N_HEADS * HEAD_PAD
    nq = seq // tq
    s_off, q_off = row_start // seq, row_start // tq
    full = lambda w: pl.BlockSpec((seq, w), lambda b, q: (s_off + b, 0))
    q_rows = lambda w: pl.BlockSpec((tq, w), lambda b, q: (q_off + b * nq + q, 0))
    in_specs = [q_rows(Q_LORA), full(KV_LORA), full(HEAD_PAD)]
    args = [cq, ckv, kr]
    if rope:
        krr, pckv, pkr, cos, sin, w_q_rot = lat
        in_specs += [full(HEAD_PAD),
                     pl.BlockSpec((None, None, past, KV_LORA), lambda b, q: (b, layer, 0, 0)),
                     pl.BlockSpec((None, None, past, HEAD_PAD), lambda b, q: (b, layer, 0, 0)),
                     _fixed_spec((seq, HEAD_PAD)), _fixed_spec((seq, HEAD_PAD))]
        args += [krr, pckv, pkr, cos, sin]
    in_specs.append(_fixed_spec((Q_LORA, hp), (layer,)))
    args.append(w_q)
    if rope:
        in_specs.append(_fixed_spec((Q_LORA, hp), (layer,)))
        args.append(w_q_rot)
    in_specs += [_fixed_spec((KV_LORA, hp), (layer,)), _fixed_spec((KV_LORA, hp), (layer,))]
    args += [w_k, w_v]
    return pl.pallas_call(
        functools.partial(_attn_kernel, past=past, tq=tq, rope=rope),
        grid=(nseq, nq),
        in_specs=in_specs,
        out_specs=pl.BlockSpec((tq, N_HEADS * V_HEAD), lambda b, q: (b * nq + q, 0)),
        out_shape=jax.ShapeDtypeStruct((nseq * seq, N_HEADS * V_HEAD), BF16),
        scratch_shapes=[pltpu.VMEM((total, hp), BF16), pltpu.VMEM((total, hp), BF16)],
        compiler_params=_params("arbitrary", "arbitrary"),
        name=f"attn_{seq}",
    )(*args)


def _mix_kernel(*refs, n_x, n_ctx_tiles):
    (hsc_ref, hsl_ref, atc_ref, atl_ref, m_ref, wy_ref, wt_ref, wro_ref, wao_ref, wo_ref,
     g_ref, b_ref, o_ref, wro_bf, wao_bf, wo_bf) = refs[n_x:]
    _cast_once([(wro_ref, wro_bf), (wao_ref, wao_bf), (wo_ref, wo_bf)])
    is_ctx = pl.program_id(0) < n_ctx_tiles
    hs = jnp.where(is_ctx, hsc_ref[...], hsl_ref[...])
    at = jnp.where(is_ctx, atc_ref[...], atl_ref[...])
    x = _load_stream(refs[:n_x], n_ctx_tiles)
    h = (x * (1.0 + m_ref[1:2, :]) + m_ref[0:1, :]).astype(BF16)
    u = (hs * jax.nn.gelu(_dot(h, wy_ref[...]))).astype(BF16)
    rnn_out = _dot(u, wro_bf[...])
    att_out = _dot(at, wao_bf[...])
    tail = _dot(h, wt_ref[...])
    merged = (jax.nn.sigmoid(tail[:, QK_ROPE:QK_ROPE + D_MODEL]) * rnn_out
              + jax.nn.sigmoid(tail[:, QK_ROPE + D_MODEL:]) * att_out)
    m = _dot(merged.astype(BF16), wo_bf[...])
    o_ref[...] = _layer_norm(ALPHA * x + m_ref[2:3, :] * m, g_ref[...], b_ref[...])


def _mix(layer, x, hs_ctx, hs_lat, at_ctx, at_lat, mods, w_y, w_tail, w_ro, w_ao, w_o, ln_g, ln_b, n_ctx_rows,
         rows_per_latent):
    tm = _ROW_TILE
    nct = n_ctx_rows // tm
    x_specs, xs = _stream_specs(x, tm, n_ctx_rows)
    n_tok = sum(a.shape[0] for a in xs)
    row = pl.BlockSpec((tm, D_MODEL), lambda i: (i, 0))
    ctx_row = pl.BlockSpec((tm, D_MODEL), lambda i: (jnp.minimum(i, nct - 1), 0))
    lat_row = pl.BlockSpec((tm, D_MODEL), lambda i: (jnp.maximum(i - nct, 0), 0))
    sq = (D_MODEL, D_MODEL)
    return pl.pallas_call(
        functools.partial(_mix_kernel, n_x=len(xs), n_ctx_tiles=nct),
        grid=(n_tok // tm,),
        in_specs=x_specs + [ctx_row, lat_row, ctx_row, lat_row,
                  _mod_spec(layer, tm, n_ctx_rows, rows_per_latent),
                  _fixed_spec((D_MODEL, D_RNN), (layer,)),
                  _fixed_spec((D_MODEL, QK_ROPE + 2 * D_MODEL), (layer,)),
                  _fixed_spec(sq, (layer,)), _fixed_spec(sq, (layer,)), _fixed_spec(sq, (layer,)),
                  _fixed_spec((1, D_MODEL), (layer,)), _fixed_spec((1, D_MODEL), (layer,))],
        out_specs=row,
        out_shape=jax.ShapeDtypeStruct((n_tok, D_MODEL), F32),
        scratch_shapes=[pltpu.VMEM(sq, BF16) for _ in range(3)],
        compiler_params=_params("arbitrary"),
        name="mix_out",
    )(*xs, hs_ctx, hs_lat, at_ctx, at_lat, mods, w_y, w_tail, w_ro, w_ao, w_o, ln_g, ln_b)


def _ffn_kernel(x_ref, m_ref, w1_ref, w3_ref, w2_ref, g_ref, b_ref, o_ref):
    x = x_ref[...]
    h = (x * (1.0 + m_ref[4:5, :]) + m_ref[3:4, :]).astype(BF16)
    f = None
    for c in range(D_FF // _FF_CHUNK):
        cols = slice(c * _FF_CHUNK, (c + 1) * _FF_CHUNK)
        act = (jax.nn.silu(_dot(h, w1_ref[:, cols])) * _dot(h, w3_ref[:, cols])).astype(BF16)
        y = _dot(act, w2_ref[cols, :])
        f = y if f is None else f + y
    o_ref[...] = _layer_norm(ALPHA * x + m_ref[5:6, :] * f, g_ref[...], b_ref[...])


def _ffn_dense(layer, x, mods, w1, w3, w2, ln_g, ln_b, n_ctx_rows, rows_per_latent):
    n_tok = x.shape[0]
    tm = _ROW_TILE
    j = layer // 2
    row = pl.BlockSpec((tm, D_MODEL), lambda i: (i, 0))
    return pl.pallas_call(
        _ffn_kernel,
        grid=(n_tok // tm,),
        in_specs=[row, _mod_spec(layer, tm, n_ctx_rows, rows_per_latent),
                  _fixed_spec((D_MODEL, D_FF), (j,)), _fixed_spec((D_MODEL, D_FF), (j,)),
                  _fixed_spec((D_FF, D_MODEL), (j,)),
                  _fixed_spec((1, D_MODEL), (layer,)), _fixed_spec((1, D_MODEL), (layer,))],
        out_specs=row,
        out_shape=jax.ShapeDtypeStruct((n_tok, D_MODEL), F32),
        compiler_params=_params("arbitrary"),
        name="ffn_dense",
    )(x, mods, w1, w3, w2, ln_g, ln_b)


def _split_bf16(a):
    hi = a.astype(BF16)
    return hi, (a - hi.astype(F32)).astype(BF16)


def _route_kernel(x_ref, m_ref, wr_ref, br_ref, h_ref, ids_ref, gw_ref, cnt_ref, run_scr, tri_scr):
    tm = x_ref.shape[0]

    @pl.when(pl.program_id(0) == 0)
    def _():
        run_scr[...] = jnp.zeros_like(run_scr)
        r = lax.broadcasted_iota(jnp.int32, (tm, tm), 0)
        c = lax.broadcasted_iota(jnp.int32, (tm, tm), 1)
        tri_scr[...] = jnp.where(c < r, 1.0, 0.0).astype(BF16)

    h = x_ref[...] * (1.0 + m_ref[4:5, :]) + m_ref[3:4, :]
    h_ref[...] = h
    h_hi, h_lo = _split_bf16(h)

    w_hi, w_lo = _split_bf16(wr_ref[...])
    logits = _dot(h_hi, w_hi) + (_dot(h_hi, w_lo) + _dot(h_lo, w_hi)) + br_ref[...]
    lane = lax.broadcasted_iota(jnp.int32, logits.shape, 1)
    big = jnp.int32(N_EXPERTS)
    v1 = jnp.max(logits, axis=-1, keepdims=True)
    i1 = jnp.min(jnp.where(logits == v1, lane, big), axis=-1, keepdims=True)
    rest = jnp.where(lane == i1, -jnp.inf, logits)
    v2 = jnp.max(rest, axis=-1, keepdims=True)
    i2 = jnp.min(jnp.where(rest == v2, lane, big), axis=-1, keepdims=True)
    e2 = jnp.exp(v2 - v1)
    g1 = 1.0 / (1.0 + e2)
    g2 = e2 / (1.0 + e2)

    m1, m2 = lane == i1, lane == i2
    member = jnp.where(m1, 1.0, 0.0) + jnp.where(m2, 1.0, 0.0)
    before = _dot(tri_scr[...], member.astype(BF16)) + run_scr[...]
    r1 = jnp.sum(jnp.where(m1, before, 0.0), axis=-1, keepdims=True).astype(jnp.int32)
    r2 = jnp.sum(jnp.where(m2, before, 0.0), axis=-1, keepdims=True).astype(jnp.int32)
    run_scr[...] += jnp.sum(member, axis=0, keepdims=True)

    ids_ref[...] = jnp.where(lane == 0, i1, jnp.where(lane == 1, i2, jnp.where(lane == 2, r1, r2)))
    gw_ref[...] = jnp.where(lane == 0, g1, g2)
    cnt_ref[...] = run_scr[...].astype(jnp.int32)


def _moe_route(layer, x, mods, w_router, b_router, n_ctx_rows, rows_per_latent):
    n_tok = x.shape[0]
    tm = _ROUTE_TILE
    j = layer // 2
    row = lambda w: pl.BlockSpec((tm, w), lambda i: (i, 0))
    return pl.pallas_call(
        _route_kernel,
        grid=(n_tok // tm,),
        in_specs=[row(D_MODEL), _mod_spec(layer, tm, n_ctx_rows, rows_per_latent),
                  _fixed_spec((D_MODEL, N_EXPERTS), (j,)), _fixed_spec((1, N_EXPERTS), (j,))],
        out_specs=[row(D_MODEL), row(N_EXPERTS), row(N_EXPERTS),
                   pl.BlockSpec((1, N_EXPERTS), lambda i: (0, 0))],
        out_shape=[jax.ShapeDtypeStruct((n_tok, D_MODEL), F32),
                   jax.ShapeDtypeStruct((n_tok, N_EXPERTS), jnp.int32),
                   jax.ShapeDtypeStruct((n_tok, N_EXPERTS), F32),
                   jax.ShapeDtypeStruct((1, N_EXPERTS), jnp.int32)],
        scratch_shapes=[pltpu.VMEM((1, N_EXPERTS), F32), pltpu.VMEM((tm, tm), BF16)],
        compiler_params=_params("arbitrary"),
        name="moe_route",
    )(x, mods, w_router, b_router)


def _tables_kernel(er_ref, g_ref, cnt_ref, exp_ref, on_ref, slot_ref, start_scr, *, n_tok, tm):
    n_tiles = exp_ref.shape[0]
    ends = []
    total = jnp.int32(0)
    for e in range(N_EXPERTS):
        start_scr[e] = total
        total = total + ((cnt_ref[0, e] + (tm - 1)) // tm) * tm
        ends.append(total)

    first_lane = lax.broadcasted_iota(jnp.int32, (1, LANE), 1) == 0
    slot_ref[...] = jnp.broadcast_to(jnp.where(first_lane, jnp.float32(n_tok), 0.0), slot_ref.shape)

    def place(t, carry):
        tok = jnp.asarray(t, jnp.int32).astype(F32)
        for k in range(TOP_K):
            slot = start_scr[er_ref[k * n_tok + t]] + er_ref[(TOP_K + k) * n_tok + t]
            slot_ref[pl.ds(slot, 1), :] = jnp.where(first_lane, tok, g_ref[k * n_tok + t])
        return carry

    lax.fori_loop(0, n_tok, place, 0, unroll=8)

    def tiles(i, carry):
        first = i * tm
        owner = jnp.int32(0)
        for e in range(N_EXPERTS):
            owner = owner + jnp.where(first >= ends[e], 1, 0)
        exp_ref[i] = jnp.minimum(owner, N_EXPERTS - 1)
        on_ref[i] = jnp.where(first < total, 1, 0)
        return carry

    lax.fori_loop(0, n_tiles, tiles, 0)


def _slot_tables(ids, gw, counts, n_tok):
    tm = _MOE_SLOT_TILE
    n_slots = TOP_K * n_tok + N_EXPERTS * tm
    n_tiles = n_slots // tm
    smem = pl.BlockSpec(memory_space=pltpu.SMEM)
    vmem = pl.BlockSpec(memory_space=pltpu.VMEM)
    tile_exp, tile_on, slot_rows = pl.pallas_call(
        functools.partial(_tables_kernel, n_tok=n_tok, tm=tm),
        in_specs=[smem, smem, smem],
        out_specs=[smem, smem, vmem],
        out_shape=[jax.ShapeDtypeStruct((n_tiles,), jnp.int32), jax.ShapeDtypeStruct((n_tiles,), jnp.int32),
                   jax.ShapeDtypeStruct((n_slots, LANE), F32)],
        scratch_shapes=[pltpu.SMEM((N_EXPERTS,), jnp.int32)],
        compiler_params=pltpu.CompilerParams(vmem_limit_bytes=VMEM_LIMIT),
        name="moe_tables",
    )(ids[:, :2 * TOP_K].T.reshape(-1), gw[:, :TOP_K].T.reshape(-1), counts)
    return tile_exp, tile_on, slot_rows[:, 0].astype(jnp.int32), slot_rows


def _moe_up_kernel(exp_ref, on_ref, tok_ref, h_ref, w1_ref, w3_ref, act_ref, xg0_scr, xg1_scr, *,
                   n_tok, n_tiles):
    t = pl.program_id(0)
    tm = xg0_scr.shape[0]

    def gather_row(tile, r, dst):
        src = jnp.minimum(tok_ref[tile * tm + r], n_tok - 1)
        dst[pl.ds(r, 1), :] = h_ref[pl.ds(src, 1), :]

    @pl.when(t == 0)
    def _():
        def body(r, carry):
            gather_row(0, r, xg0_scr)
            return carry

        lax.fori_loop(0, tm, body, 0, unroll=8)

    for parity, (cur, nxt) in enumerate(((xg0_scr, xg1_scr), (xg1_scr, xg0_scr))):
        @pl.when(jnp.logical_and(on_ref[t] == 1, t % 2 == parity))
        def _():
            h = cur[...].astype(BF16)
            nxt_tile = jnp.minimum(t + 1, n_tiles - 1)
            for r in range(tm):
                gather_row(nxt_tile, r, nxt)
            act_ref[...] = (jax.nn.silu(_dot(h, w1_ref[...])) * _dot(h, w3_ref[...])).astype(BF16)

    @pl.when(on_ref[t] == 0)
    def _():
        act_ref[...] = jnp.zeros_like(act_ref)


def _moe_up(layer, tables, h, w1, w3):
    tile_exp, tile_on, slot_tok, _ = tables
    n_tok = h.shape[0]
    tm = _MOE_SLOT_TILE
    n_tiles = tile_exp.shape[0]
    j = layer // 2
    wspec = pl.BlockSpec((None, None, D_MODEL, D_EXPERT), lambda t, ex, on, tk: (j, ex[t], 0, 0),
                         pipeline_mode=pl.Buffered(1))
    return pl.pallas_call(
        functools.partial(_moe_up_kernel, n_tok=n_tok, n_tiles=n_tiles),
        grid_spec=pltpu.PrefetchScalarGridSpec(
            num_scalar_prefetch=3,
            grid=(n_tiles,),
            in_specs=[pl.BlockSpec((n_tok, D_MODEL), lambda t, ex, on, tk: (0, 0),
                                   pipeline_mode=pl.Buffered(1)),
                      wspec, wspec],
            out_specs=pl.BlockSpec((tm, D_EXPERT), lambda t, ex, on, tk: (t, 0)),
            scratch_shapes=[pltpu.VMEM((tm, D_MODEL), F32), pltpu.VMEM((tm, D_MODEL), F32)],
        ),
        out_shape=jax.ShapeDtypeStruct((n_tiles * tm, D_EXPERT), BF16),
        compiler_params=_params("arbitrary"),
        name="moe_up",
    )(tile_exp, tile_on, slot_tok, h, w1, w3)


def _moe_down_kernel(exp_ref, on_ref, tok_ref, act_ref, gate_ref, w2_ref, x_ref, m_ref, g_ref, b_ref,
                     *rest, n_tiles, ln_rows, n_ctx_ln_tiles, split):
    o_refs, (acc_scr, y0_scr, y1_scr) = rest[:-3], rest[-3:]
    t = pl.program_id(0)
    tm = y0_scr.shape[0]

    @pl.when(t == 0)
    def _():
        acc_scr[...] = jnp.zeros_like(acc_scr)

    def project(dst):
        dst[...] = _dot(act_ref[...], w2_ref[...]) * gate_ref[:, 1:2]

    def scatter_group(tile, base, ys):
        toks = [tok_ref[tile * tm + base + u] for u in range(SUBLANE)]
        rows = [acc_scr[pl.ds(toks[u], 1), :] for u in range(SUBLANE)]
        for u in range(SUBLANE):
            acc_scr[pl.ds(toks[u], 1), :] = rows[u] + ys[u:u + 1, :]

    last = n_tiles - 1
    project_now = jnp.logical_and(t < n_tiles, on_ref[jnp.minimum(t, last)] == 1)
    add_prev = jnp.logical_and(jnp.logical_and(t >= 1, t <= n_tiles), on_ref[jnp.clip(t - 1, 0, last)] == 1)
    for parity, (cur, prv) in enumerate(((y0_scr, y1_scr), (y1_scr, y0_scr))):
        mine = t % 2 == parity

        @pl.when(jnp.logical_and(mine, jnp.logical_and(project_now, add_prev)))
        def _():
            project(cur)
            for g in range(tm // SUBLANE):
                scatter_group(t - 1, g * SUBLANE, prv[g * SUBLANE:(g + 1) * SUBLANE, :])

        @pl.when(jnp.logical_and(mine, jnp.logical_and(project_now, jnp.logical_not(add_prev))))
        def _():
            project(cur)

        @pl.when(jnp.logical_and(mine, jnp.logical_and(add_prev, jnp.logical_not(project_now))))
        def _():
            def body(g, carry):
                base = pl.multiple_of(g * SUBLANE, SUBLANE)
                scatter_group(t - 1, base, prv[pl.ds(base, SUBLANE), :])
                return carry

            lax.fori_loop(0, tm // SUBLANE, body, 0)

    @pl.when(t > n_tiles)
    def _():
        r0 = pl.multiple_of((t - n_tiles - 1) * ln_rows, ln_rows)
        f = acc_scr[pl.ds(r0, ln_rows), :]
        out = _layer_norm(ALPHA * x_ref[...] + m_ref[5:6, :] * f, g_ref[...], b_ref[...])
        if split:
            is_ctx = t - n_tiles - 1 < n_ctx_ln_tiles

            @pl.when(is_ctx)
            def _():
                o_refs[0][...] = out

            @pl.when(jnp.logical_not(is_ctx))
            def _():
                o_refs[1][...] = out
        else:
            o_refs[0][...] = out


def _moe_down(layer, tables, act, x, mods, w2, ln_g, ln_b, n_ctx_rows, rows_per_latent, split):
    tile_exp, tile_on, slot_tok, slot_gate = tables
    n_tok = x.shape[0]
    tm = _MOE_SLOT_TILE
    tl = _MOE_LN_TILE
    n_tiles = tile_exp.shape[0]
    j = layer // 2
    slot = lambda t: jnp.minimum(t, n_tiles - 1)
    ln = lambda t: jnp.maximum(t - n_tiles - 1, 0)
    n_ctx_tiles, per = n_ctx_rows // tl, rows_per_latent // tl

    def mod_index(t, *_):
        i = ln(t)
        return (layer, jnp.where(i < n_ctx_tiles, 0, 1 + (i - n_ctx_tiles) // per), 0, 0)

    fixed = lambda shape, lead: pl.BlockSpec((None,) + shape, lambda *_: (lead,) + (0,) * len(shape),
                                             pipeline_mode=pl.Buffered(1))
    if split:
        out_specs = [pl.BlockSpec((tl, D_MODEL), lambda t, *_: (jnp.minimum(ln(t), n_ctx_tiles - 1), 0)),
                     pl.BlockSpec((tl, D_MODEL), lambda t, *_: (jnp.maximum(ln(t) - n_ctx_tiles, 0), 0))]
        out_shape = [jax.ShapeDtypeStruct((n_ctx_rows, D_MODEL), F32),
                     jax.ShapeDtypeStruct((n_tok - n_ctx_rows, D_MODEL), F32)]
    else:
        out_specs = pl.BlockSpec((tl, D_MODEL), lambda t, *_: (ln(t), 0))
        out_shape = jax.ShapeDtypeStruct((n_tok, D_MODEL), F32)
    return pl.pallas_call(
        functools.partial(_moe_down_kernel, n_tiles=n_tiles, ln_rows=tl, n_ctx_ln_tiles=n_ctx_tiles,
                          split=split),
        grid_spec=pltpu.PrefetchScalarGridSpec(
            num_scalar_prefetch=3,
            grid=(n_tiles + 1 + n_tok // tl,),
            in_specs=[pl.BlockSpec((tm, D_EXPERT), lambda t, ex, *_: (slot(t), 0)),
                      pl.BlockSpec((tm, LANE), lambda t, ex, *_: (slot(t), 0)),
                      pl.BlockSpec((None, None, D_EXPERT, D_MODEL), lambda t, ex, *_: (j, ex[slot(t)], 0, 0)),
                      pl.BlockSpec((tl, D_MODEL), lambda t, *_: (ln(t), 0)),
                      pl.BlockSpec((None, None, 6, D_MODEL), mod_index),
                      fixed((1, D_MODEL), layer), fixed((1, D_MODEL), layer)],
            out_specs=out_specs,
            scratch_shapes=[pltpu.VMEM((n_tok + SUBLANE, D_MODEL), F32), pltpu.VMEM((tm, D_MODEL), F32),
                            pltpu.VMEM((tm, D_MODEL), F32)],
        ),
        out_shape=out_shape,
        compiler_params=_params("arbitrary"),
        name="moe_down",
    )(tile_exp, tile_on, slot_tok, act, slot_gate, w2, x, mods, ln_g, ln_b)


def _ffn_moe(layer, x, mods, w_router, b_router, w1, w3, w2, ln_g, ln_b, n_ctx_rows, rows_per_latent,
             split=False):
    h, ids, gw, counts = _moe_route(layer, x, mods, w_router, b_router, n_ctx_rows, rows_per_latent)
    tables = _slot_tables(ids, gw, counts, x.shape[0])
    act = _moe_up(layer, tables, h, w1, w3)
    return _moe_down(layer, tables, act, x, mods, w2, ln_g, ln_b, n_ctx_rows, rows_per_latent, split)


def _rot_cols(w):
    ws = w.reshape(w.shape[:-1] + (2, 2, ROPE_AXIS // 2))
    return jnp.stack([-ws[..., 1, :], ws[..., 0, :]], axis=-2).reshape(w.shape)


def _pad_rope_cols(w):
    return jnp.pad(w, ((0, 0),) * (w.ndim - 1) + ((ROPE_OFF, HEAD_PAD - ROPE_OFF - QK_ROPE),))


def _rope_tables(rows):
    r = jnp.repeat(jnp.arange(rows, dtype=F32), GRID_W)
    col = jnp.tile(jnp.arange(GRID_W, dtype=F32), rows)
    freqs = ROPE_THETA ** (-jnp.arange(0, ROPE_AXIS, 2, dtype=F32) / ROPE_AXIS)
    ar = r[:, None] * freqs
    ac = col[:, None] * freqs
    ang = jnp.concatenate([ar, ar, ac, ac], axis=-1)
    cos, sin = jnp.cos(ang), jnp.sin(ang)
    n = cos.shape[0]
    cos_p = jnp.concatenate([jnp.ones((n, ROPE_OFF), F32), cos,
                             jnp.zeros((n, HEAD_PAD - ROPE_OFF - QK_ROPE), F32)], axis=-1)
    return cos_p, _pad_rope_cols(sin)


def _prepared_weights(w_in, rg_wa, rg_ba, rg_wx, rg_bx, w_qb, w_kvb):
    bf = lambda a: a.astype(BF16)
    w_kr = w_in[:, :, _COL_KR:_COL_KR + QK_ROPE]
    w_s = jnp.concatenate([w_in[:, :, _COL_Q:_COL_KR], _pad_rope_cols(w_kr),
                           _pad_rope_cols(_rot_cols(w_kr))], axis=-1)
    w_x = w_in[:, :, :D_RNN]
    w_y = w_in[:, :, D_RNN:2 * D_RNN]
    w_tail = w_in[:, :, _COL_KR:]

    hp = N_HEADS * HEAD_PAD

    def head_pad(w, lo):
        w = jnp.pad(w, ((0, 0),) * 3 + ((lo, HEAD_PAD - lo - w.shape[-1]),))
        return w.reshape(DEPTH, w.shape[1], hp)

    wq = w_qb.reshape(DEPTH, Q_LORA, N_HEADS, QK_NOPE + QK_ROPE)
    w_q = head_pad(wq, 0)
    w_q_rot = head_pad(_rot_cols(wq[..., QK_NOPE:]), ROPE_OFF)
    wkv = w_kvb.reshape(DEPTH, KV_LORA, N_HEADS, QK_NOPE + V_HEAD)
    w_k = head_pad(wkv[..., :QK_NOPE], 0)
    v = wkv[..., QK_NOPE:].reshape(DEPTH, KV_LORA, N_HEADS // 2, 2, V_HEAD)
    w_v = jnp.einsum('lcjpd,pq->lcjpqd', v, jnp.eye(2, dtype=F32)).reshape(DEPTH, KV_LORA, hp)

    nblk = RNN_GROUP // RNN_BW
    ng = D_RNN // RNN_GROUP
    kinds = lambda a, x: jnp.stack([a[:, 0], x[:, 0], a[:, 1], x[:, 1]], axis=1)
    wg = kinds(rg_wa, rg_wx).reshape(DEPTH, 4, ng, nblk, RNN_BW, RNN_BW)
    w_gate = jnp.einsum('ltgnjk,nm->lgnjtmk', wg, jnp.eye(nblk, dtype=F32))
    w_gate = w_gate.reshape(DEPTH, ng, RNN_GROUP, 4 * RNN_GROUP)
    b_gate = kinds(rg_ba, rg_bx).reshape(DEPTH, 4, ng, RNN_GROUP).transpose(0, 2, 1, 3)
    b_gate = b_gate.reshape(DEPTH, ng, 1, 4 * RNN_GROUP)
    return dict(w_x=bf(w_x), w_s=bf(w_s), w_y=bf(w_y), w_tail=bf(w_tail), w_q=bf(w_q), w_q_rot=bf(w_q_rot), w_k=bf(w_k), w_v=bf(w_v),
                w_gate=bf(w_gate), b_gate=b_gate)


def kernel(x_prompt, x_sample, cache_ckv, cache_krope, state_rnn, c, c_ctx, w_ada, b_ada, w_in, conv_w, conv_b, rg_wa, rg_ba, rg_wx, rg_bx, rg_lambda, w_rnn_out, q_norm_g, w_qb, kv_norm_g, w_kvb, w_attn_out, w_out, ln1_g, ln1_b, ln2_g, ln2_b, w1_dense, w3_dense, w2_dense, w_router, b_router, w1_exp, w3_exp, w2_exp):
    nb, seq, _ = x_prompt.shape
    db, dseq, _ = x_sample.shape
    n_ctx = nb * seq
    n_lat = db * dseq
    assert n_ctx % _ROUTE_TILE == 0 and dseq % _ROUTE_TILE == 0 and _ROUTE_TILE % _ROW_TILE == 0 and db + 1 <= 8
    assert DEPTH % 2 == 0
    bf = lambda a: a.astype(BF16)
    vec = lambda a: a.reshape(a.shape[0], 1, a.shape[-1])

    cond8 = jnp.concatenate([c_ctx[None, :], c, jnp.zeros((8 - 1 - db, D_MODEL), F32)], axis=0)
    mods = _ada_mods(cond8, w_ada, b_ada)
    cos_p, sin_p = _rope_tables(dseq // GRID_W)
    past_kr = _pad_rope_cols(cache_krope)
    pw = _prepared_weights(w_in, rg_wa, rg_ba, rg_wx, rg_bx, w_qb, w_kvb)
    w1d, w3d, w2d = bf(w1_dense), bf(w3_dense), bf(w2_dense)
    w1e, w3e, w2e = bf(w1_exp), bf(w3_exp), bf(w2_exp)
    qg, kvg, cb = vec(q_norm_g), vec(kv_norm_g), vec(conv_b)
    l1g, l1b, l2g, l2b = vec(ln1_g), vec(ln1_b), vec(ln2_g), vec(ln2_b)
    br = vec(b_router)
    tiles = (n_ctx, dseq)

    x = (x_prompt.reshape(n_ctx, D_MODEL), x_sample.reshape(n_lat, D_MODEL))
    new_ckv, new_kr, new_h = [], [], []
    for l in range(DEPTH):
        xr, cq, ckv, kr, krr = _in_proj(l, x, mods, pw['w_x'], pw['w_s'], qg, kvg, *tiles)
        new_ckv.append(ckv[:n_ctx].reshape(nb, seq, KV_LORA))
        new_kr.append(kr[:n_ctx, ROPE_OFF:ROPE_OFF + QK_ROPE].reshape(nb, seq, QK_ROPE))

        rnn_args = (conv_w, cb, pw['w_gate'], pw['b_gate'], rg_lambda)
        hs_ctx, h_last = _rglru(l, xr, 0, nb, seq, *rnn_args, emit_last=True)
        (hs_lat,) = _rglru(l, xr, n_ctx, db, dseq, *rnn_args, h0=state_rnn)
        new_h.append(h_last)

        att_w = (pw['w_q'], pw['w_k'], pw['w_v'])
        at_ctx = _attention(l, cq, ckv, kr, 0, nb, seq, *att_w)
        at_lat = _attention(l, cq, ckv, kr, n_ctx, db, dseq, *att_w,
                            lat=(krr, cache_ckv, past_kr, cos_p, sin_p, pw['w_q_rot']))

        x = _mix(l, x, hs_ctx, hs_lat, at_ctx, at_lat, mods, pw['w_y'], pw['w_tail'], w_rnn_out, w_attn_out, w_out,
                 l1g, l1b, *tiles)
        if l % 2 == 0:
            x = _ffn_dense(l, x, mods, w1d, w3d, w2d, l2g, l2b, *tiles)
        else:
            x = _ffn_moe(l, x, mods, w_router, br, w1e, w3e, w2e, l2g, l2b, *tiles, split=l == DEPTH - 1)

    return (x[0].reshape(nb, seq, D_MODEL), x[1].reshape(db, dseq, D_MODEL),
            jnp.stack(new_ckv, axis=1), jnp.stack(new_kr, axis=1), jnp.stack(new_h, axis=1))
```

```python
import functools
import math

import jax
import jax.numpy as jnp
from jax import lax
from jax.experimental import pallas as pl
from jax.experimental.pallas import tpu as pltpu

F32 = jnp.float32
BF16 = jnp.bfloat16

D_MODEL = 1024
DEPTH = 4
GRID_W = 64
D_RNN = D_MODEL
RNN_BLOCKS = 16
RNN_BW = D_RNN // RNN_BLOCKS
CONV_W = 4
RG_C = 8.0
N_HEADS = 16
QK_NOPE = 64
QK_ROPE = 32
V_HEAD = 64
Q_LORA = 256
KV_LORA = 128
ROPE_AXIS = QK_ROPE // 2
ROPE_THETA = 10000.0
D_FF = 2816
N_EXPERTS = 8
TOP_K = 2
D_EXPERT = 1408
ALPHA = (2 * DEPTH) ** 0.25
LN_EPS = 1e-5
RMS_EPS = 1e-6

LANE = 128
SUBLANE = 8
HEAD_PAD = LANE
ROPE_OFF = QK_NOPE
RNN_GROUP = 256
_RNN_SEQ_TILE = 4
VMEM_LIMIT = 58 * 1024 * 1024

_ROW_TILE = 512
_ROUTE_TILE = 1024
_MOE_SLOT_TILE = 512
_MOE_LN_TILE = 256
_ATT_Q_TILE = 512
_FF_CHUNK = 1408

_COL_Q = 2 * D_RNN
_COL_KR = _COL_Q + Q_LORA + KV_LORA
_COL_GATE = _COL_KR + QK_ROPE


def _dot(a, b):
    return jnp.dot(a, b, preferred_element_type=F32)


def _layer_norm(y, g, b):
    mu = jnp.mean(y, axis=-1, keepdims=True)
    d = y - mu
    var = jnp.mean(d * d, axis=-1, keepdims=True)
    return d * lax.rsqrt(var + LN_EPS) * g + b


def _rms_norm(y, g):
    return y * lax.rsqrt(jnp.mean(y * y, axis=-1, keepdims=True) + RMS_EPS) * g


def _params(*sem):
    return pltpu.CompilerParams(dimension_semantics=sem, vmem_limit_bytes=VMEM_LIMIT)


def _fixed_spec(shape, lead=(), tail=None):
    tail = (0,) * len(shape) if tail is None else tuple(tail)
    index = tuple(lead) + tail
    return pl.BlockSpec((None,) * len(lead) + tuple(shape), lambda *_: index,
                        pipeline_mode=pl.Buffered(1))


def _mod_spec(layer, tm, n_ctx_rows, rows_per_latent):
    n_ctx_tiles = n_ctx_rows // tm
    per = rows_per_latent // tm

    def index(i, *_):
        return (layer, jnp.where(i < n_ctx_tiles, 0, 1 + (i - n_ctx_tiles) // per), 0, 0)

    return pl.BlockSpec((None, None, 6, D_MODEL), index)


def _cast_once(pairs):
    @pl.when(pl.program_id(0) == 0)
    def _():
        for src, dst in pairs:
            dst[...] = src[...].astype(BF16)


def _stream_specs(x, tm, n_ctx_rows):
    if isinstance(x, tuple):
        nct = n_ctx_rows // tm
        return ([pl.BlockSpec((tm, D_MODEL), lambda i, *_: (jnp.minimum(i, nct - 1), 0)),
                 pl.BlockSpec((tm, D_MODEL), lambda i, *_: (jnp.maximum(i - nct, 0), 0))], list(x))
    return [pl.BlockSpec((tm, D_MODEL), lambda i, *_: (i, 0))], [x]


def _load_stream(refs, n_ctx_tiles):
    if len(refs) == 2:
        return jnp.where(pl.program_id(0) < n_ctx_tiles, refs[0][...], refs[1][...])
    return refs[0][...]


def _mods_kernel(c_ref, w_ref, b_ref, o_ref):
    s = jax.nn.silu(c_ref[...]).astype(BF16)
    o_ref[...] = _dot(s, w_ref[...].astype(BF16)) + b_ref[...]


def _ada_mods(cond8, w_ada, b_ada):
    out = pl.pallas_call(
        _mods_kernel,
        grid=(DEPTH, 6),
        in_specs=[
            pl.BlockSpec((8, D_MODEL), lambda l, j: (0, 0)),
            pl.BlockSpec((None, D_MODEL, D_MODEL), lambda l, j: (l, 0, j)),
            pl.BlockSpec((None, 1, D_MODEL), lambda l, j: (l, 0, j)),
        ],
        out_specs=pl.BlockSpec((None, 8, D_MODEL), lambda l, j: (l, 0, j)),
        out_shape=jax.ShapeDtypeStruct((DEPTH, 8, 6 * D_MODEL), F32),
        compiler_params=_params("arbitrary", "arbitrary"),
        name="ada_mods",
    )(cond8, w_ada, b_ada.reshape(DEPTH, 1, 6 * D_MODEL))
    return out.reshape(DEPTH, 8, 6, D_MODEL)


_S_COLS = Q_LORA + KV_LORA + 2 * HEAD_PAD


def _inproj_kernel(*refs, n_x, n_ctx_tiles):
    m_ref, wx_ref, ws_ref, qg_ref, kvg_ref, xr_ref, cq_ref, ckv_ref, kr_ref, krr_ref = refs[n_x:]
    x = _load_stream(refs[:n_x], n_ctx_tiles)
    h = (x * (1.0 + m_ref[1:2, :]) + m_ref[0:1, :]).astype(BF16)
    xr_ref[...] = _dot(h, wx_ref[...])
    s = _dot(h, ws_ref[...])
    cq_ref[...] = _rms_norm(s[:, :Q_LORA], qg_ref[...]).astype(BF16)
    ckv_ref[...] = _rms_norm(s[:, Q_LORA:Q_LORA + KV_LORA], kvg_ref[...])
    kr_ref[...] = s[:, Q_LORA + KV_LORA:Q_LORA + KV_LORA + HEAD_PAD]
    krr_ref[...] = s[:, Q_LORA + KV_LORA + HEAD_PAD:]


def _in_proj(layer, x, mods, w_x, w_s, q_g, kv_g, n_ctx_rows, rows_per_latent):
    tm = _ROW_TILE
    x_specs, xs = _stream_specs(x, tm, n_ctx_rows)
    n_tok = sum(a.shape[0] for a in xs)
    row = lambda w: pl.BlockSpec((tm, w), lambda i: (i, 0))
    return pl.pallas_call(
        functools.partial(_inproj_kernel, n_x=len(xs), n_ctx_tiles=n_ctx_rows // tm),
        grid=(n_tok // tm,),
        in_specs=x_specs + [
            _mod_spec(layer, tm, n_ctx_rows, rows_per_latent),
            _fixed_spec((D_MODEL, D_RNN), (layer,)),
            _fixed_spec((D_MODEL, _S_COLS), (layer,)),
            _fixed_spec((1, Q_LORA), (layer,)),
            _fixed_spec((1, KV_LORA), (layer,)),
        ],
        out_specs=[row(D_RNN), row(Q_LORA), row(KV_LORA), row(HEAD_PAD), row(HEAD_PAD)],
        out_shape=[
            jax.ShapeDtypeStruct((n_tok, D_RNN), F32),
            jax.ShapeDtypeStruct((n_tok, Q_LORA), BF16),
            jax.ShapeDtypeStruct((n_tok, KV_LORA), F32),
            jax.ShapeDtypeStruct((n_tok, HEAD_PAD), F32),
            jax.ShapeDtypeStruct((n_tok, HEAD_PAD), F32),
        ],
        compiler_params=_params("arbitrary"),
        name="in_proj",
    )(*xs, mods, w_x, w_s, q_g, kv_g)


def _segment_pitch(seg):
    assert seg % SUBLANE == 0
    return seg if (seg // SUBLANE) % 2 == 1 else seg + SUBLANE


def _rglru_kernel(*refs, seq, bt, has_h0, emit_last):
    it = iter(refs)
    x_ref, cw_ref, cb_ref, wg_ref, bg_ref, lam_ref = (next(it) for _ in range(6))
    h0_ref = next(it) if has_h0 else None
    hs_ref = next(it)
    hl_ref = next(it) if emit_last else None
    a_refs = (next(it), next(it))
    b_refs = (next(it), next(it))
    dg = RNN_GROUP
    nslab = dg // LANE
    seg = seq // SUBLANE
    pitch = _segment_pitch(seg)

    row = lax.broadcasted_iota(jnp.int32, (seq, dg), 0)
    decay_rate = [RG_C * jax.nn.softplus(-lam_ref[d:d + 1, :]) for d in range(2)]
    for b in range(bt):
        rows = slice(b * seq, (b + 1) * seq)
        x = x_ref[rows, :]
        xm1 = jnp.where(row >= 1, pltpu.roll(x, 1, 0), 0.0)
        xp1 = jnp.where(row < seq - 1, pltpu.roll(x, seq - 1, 0), 0.0)
        xp2 = jnp.where(row < seq - 2, pltpu.roll(x, seq - 2, 0), 0.0)
        xc = cb_ref[...] + xm1 * cw_ref[0:1, :]
        xc = xc + x * cw_ref[1:2, :]
        xc = xc + xp1 * cw_ref[2:3, :]
        xc = xc + xp2 * cw_ref[3:4, :]
        gates = _dot(xc.astype(BF16), wg_ref[...]) + bg_ref[...]
        for d in range(2):
            r = jax.nn.sigmoid(gates[:, (2 * d) * dg:(2 * d + 1) * dg])
            i = jax.nn.sigmoid(gates[:, (2 * d + 1) * dg:(2 * d + 2) * dg])
            neg_log_a = r * decay_rate[d]
            a = jnp.exp(-neg_log_a)
            y = jnp.tanh(neg_log_a) * (a * a + 1.0)
            bq = jnp.where(y > 0.0, y * lax.rsqrt(y), 0.0) * (i * xc)
            for s in range(SUBLANE):
                for c in range(nslab):
                    src = (slice(s * seg, (s + 1) * seg), slice(c * LANE, (c + 1) * LANE))
                    dst = slice(s * pitch, s * pitch + seg)
                    a_refs[d][b * nslab + c, dst, :] = a[src]
                    b_refs[d][b * nslab + c, dst, :] = bq[src]

    nch = bt * nslab
    zeros = tuple(jnp.zeros((SUBLANE, LANE), F32) for _ in range(nch))
    ones = tuple(jnp.ones((SUBLANE, LANE), F32) for _ in range(nch))

    def step(j, carry):
        out = []
        for d, t in ((0, j), (1, seg - 1 - j)):
            h_prev, p_prev = carry[2 * d], carry[2 * d + 1]
            h_new, p_new = [], []
            idx = pl.ds(t, SUBLANE, stride=pitch)
            for ch in range(nch):
                a = a_refs[d].at[ch][idx, :]
                h = a * h_prev[ch] + b_refs[d].at[ch][idx, :]
                p = a * p_prev[ch]
                b_refs[d].at[ch][idx, :] = h
                a_refs[d].at[ch][idx, :] = p
                h_new.append(h)
                p_new.append(p)
            out += [tuple(h_new), tuple(p_new)]
        return tuple(out)

    h_f, p_f, h_b, p_b = lax.fori_loop(0, seg, step, (zeros, ones, zeros, ones), unroll=2)

    for b in range(bt):
        for c in range(nslab):
            ch = b * nslab + c
            lanes = slice(c * LANE, (c + 1) * LANE)
            row_of = lambda v, s: v[s:s + 1, :]
            if has_h0:
                enter_f, enter_b = h0_ref[b, 0:1, lanes], h0_ref[b, 1:2, lanes]
            else:
                enter_f = enter_b = jnp.zeros((1, LANE), F32)
            ent_f, ent_b = [None] * SUBLANE, [None] * SUBLANE
            for s in range(SUBLANE):
                ent_f[s] = enter_f
                enter_f = row_of(h_f[ch], s) + row_of(p_f[ch], s) * enter_f
            for s in reversed(range(SUBLANE)):
                ent_b[s] = enter_b
                enter_b = row_of(h_b[ch], s) + row_of(p_b[ch], s) * enter_b
            if emit_last:
                hl_ref[b, 0:1, lanes] = enter_f
                hl_ref[b, 1:2, lanes] = enter_b
            for s in range(SUBLANE):
                src = slice(s * pitch, s * pitch + seg)
                fwd = b_refs[0][ch, src, :] + a_refs[0][ch, src, :] * ent_f[s]
                bwd = b_refs[1][ch, src, :] + a_refs[1][ch, src, :] * ent_b[s]
                hs_ref[b * seq + s * seg:b * seq + (s + 1) * seg, lanes] = fwd + bwd


def _rglru(layer, xr, row_start, nseq, seq, conv_w, conv_b, w_gate, b_gate, lam, h0=None, emit_last=False):
    bt = _RNN_SEQ_TILE
    dg = RNN_GROUP
    ng = D_RNN // dg
    rows = bt * seq
    assert nseq % bt == 0 and row_start % rows == 0
    off = row_start // rows
    in_specs = [
        pl.BlockSpec((rows, dg), lambda bi, g: (off + bi, g)),
        pl.BlockSpec((None, CONV_W, dg), lambda bi, g: (layer, 0, g)),
        pl.BlockSpec((None, 1, dg), lambda bi, g: (layer, 0, g)),
        pl.BlockSpec((None, None, dg, 4 * dg), lambda bi, g: (layer, g, 0, 0)),
        pl.BlockSpec((None, None, 1, 4 * dg), lambda bi, g: (layer, g, 0, 0)),
        pl.BlockSpec((None, 2, dg), lambda bi, g: (layer, 0, g)),
    ]
    args = [xr, conv_w, conv_b, w_gate, b_gate, lam]
    if h0 is not None:
        in_specs.append(pl.BlockSpec((bt, None, 2, dg), lambda bi, g: (bi, layer, 0, g)))
        args.append(h0)
    out_specs = [pl.BlockSpec((rows, dg), lambda bi, g: (bi, g))]
    out_shape = [jax.ShapeDtypeStruct((nseq * seq, D_RNN), F32)]
    if emit_last:
        out_specs.append(pl.BlockSpec((bt, 2, dg), lambda bi, g: (bi, 0, g)))
        out_shape.append(jax.ShapeDtypeStruct((nseq, 2, D_RNN), F32))
    return pl.pallas_call(
        functools.partial(_rglru_kernel, seq=seq, bt=bt, has_h0=h0 is not None, emit_last=emit_last),
        grid=(nseq // bt, ng),
        in_specs=in_specs,
        out_specs=out_specs,
        out_shape=out_shape,
        scratch_shapes=[pltpu.VMEM((bt * (dg // LANE), SUBLANE * _segment_pitch(seq // SUBLANE), LANE), F32)
                        for _ in range(4)],
        compiler_params=_params("arbitrary", "arbitrary"),
        name=f"rglru_{seq}",
    )(*args)


def _attn_kernel(*refs, past, tq, rope):
    it = iter(refs)
    cq_ref, ckv_ref, kr_ref = (next(it) for _ in range(3))
    if rope:
        krr_ref, pckv_ref, pkr_ref, cos_ref, sin_ref = (next(it) for _ in range(5))
    wq_ref = next(it)
    wqr_ref = next(it) if rope else None
    wk_ref, wv_ref = next(it), next(it)
    o_ref = next(it)
    k_scr, v_scr = next(it), next(it)
    qi = pl.program_id(1)
    scale = 1.0 / math.sqrt(QK_NOPE + QK_ROPE)

    def fill(rows, ckv, k_rope):
        c = ckv.astype(BF16)
        k = _dot(c, wk_ref[...]) + jnp.tile(k_rope, (1, N_HEADS))
        k_scr[rows, :] = k.astype(BF16)
        v_scr[rows, :] = _dot(c, wv_ref[...]).astype(BF16)

    @pl.when(qi == 0)
    def _():
        seq = ckv_ref.shape[0]
        if rope:
            fill(pl.ds(0, past), pckv_ref[...], pkr_ref[...])
            fill(pl.ds(past, seq), ckv_ref[...],
                 kr_ref[...] * cos_ref[...] + krr_ref[...] * sin_ref[...])
        else:
            fill(pl.ds(0, seq), ckv_ref[...], kr_ref[...])

    cq = cq_ref[...]
    qa = _dot(cq, wq_ref[...])
    if rope:
        qb = _dot(cq, wqr_ref[...])
        q0 = pl.multiple_of(qi * tq, tq)
        cos_t = cos_ref[pl.ds(q0, tq), :]
        sin_t = sin_ref[pl.ds(q0, tq), :]
    for j in range(N_HEADS // 2):
        pair = None
        for h in (2 * j, 2 * j + 1):
            cols = slice(h * HEAD_PAD, (h + 1) * HEAD_PAD)
            qh = qa[:, cols]
            if rope:
                qh = qh * cos_t + qb[:, cols] * sin_t
            s = lax.dot_general(qh.astype(BF16), k_scr[:, cols], (((1,), (1,)), ((), ())),
                                preferred_element_type=F32)
            e = jnp.exp((s - jnp.max(s, axis=-1, keepdims=True)) * scale)
            denom = jnp.sum(e, axis=-1, keepdims=True)
            o = _dot(e.astype(BF16), v_scr[:, cols]) * (1.0 / denom)
            pair = o if pair is None else pair + o
        o_ref[:, j * LANE:(j + 1) * LANE] = pair.astype(BF16)


def _attention(layer, cq, ckv, kr, row_start, nseq, seq, w_q, w_k, w_v, lat=None):
    rope = lat is not None
    tq = min(_ATT_Q_TILE, seq)
    assert seq % tq == 0 and row_start % seq == 0
    past = lat[1].shape[2] if rope else 0
    total = past + seq
    hp = N_HEADS * HEAD_PAD
    nq = seq // tq
    s_off, q_off = row_start // seq, row_start // tq
    full = lambda w: pl.BlockSpec((seq, w), lambda b, q: (s_off + b, 0))
    q_rows = lambda w: pl.BlockSpec((tq, w), lambda b, q: (q_off + b * nq + q, 0))
    in_specs = [q_rows(Q_LORA), full(KV_LORA), full(HEAD_PAD)]
    args = [cq, ckv, kr]
    if rope:
        krr, pckv, pkr, cos, sin, w_q_rot = lat
        in_specs += [full(HEAD_PAD),
                     pl.BlockSpec((None, None, past, KV_LORA), lambda b, q: (b, layer, 0, 0)),
                     pl.BlockSpec((None, None, past, HEAD_PAD), lambda b, q: (b, layer, 0, 0)),
                     _fixed_spec((seq, HEAD_PAD)), _fixed_spec((seq, HEAD_PAD))]
        args += [krr, pckv, pkr, cos, sin]
    in_specs.append(_fixed_spec((Q_LORA, hp), (layer,)))
    args.append(w_q)
    if rope:
        in_specs.append(_fixed_spec((Q_LORA, hp), (layer,)))
        args.append(w_q_rot)
    in_specs += [_fixed_spec((KV_LORA, hp), (layer,)), _fixed_spec((KV_LORA, hp), (layer,))]
    args += [w_k, w_v]
    return pl.pallas_call(
        functools.partial(_attn_kernel, past=past, tq=tq, rope=rope),
        grid=(nseq, nq),
        in_specs=in_specs,
        out_specs=pl.BlockSpec((tq, N_HEADS * V_HEAD), lambda b, q: (b * nq + q, 0)),
        out_shape=jax.ShapeDtypeStruct((nseq * seq, N_HEADS * V_HEAD), BF16),
        scratch_shapes=[pltpu.VMEM((total, hp), BF16), pltpu.VMEM((total, hp), BF16)],
        compiler_params=_params("arbitrary", "arbitrary"),
        name=f"attn_{seq}",
    )(*args)


def _mix_kernel(*refs, n_x, n_ctx_tiles):
    (hsc_ref, hsl_ref, atc_ref, atl_ref, m_ref, wygg_ref, wro_ref, wao_ref, wo_ref,
     g_ref, b_ref, o_ref, wro_bf, wao_bf, wo_bf) = refs[n_x:]
    _cast_once([(wro_ref, wro_bf), (wao_ref, wao_bf), (wo_ref, wo_bf)])
    is_ctx = pl.program_id(0) < n_ctx_tiles
    hs = jnp.where(is_ctx, hsc_ref[...], hsl_ref[...])
    at = jnp.where(is_ctx, atc_ref[...], atl_ref[...])
    x = _load_stream(refs[:n_x], n_ctx_tiles)
    h = (x * (1.0 + m_ref[1:2, :]) + m_ref[0:1, :]).astype(BF16)
    ygg = _dot(h, wygg_ref[...])
    u = (hs * jax.nn.gelu(ygg[:, :D_RNN])).astype(BF16)
    rnn_out = _dot(u, wro_bf[...])
    att_out = _dot(at, wao_bf[...])
    merged = (jax.nn.sigmoid(ygg[:, D_RNN:D_RNN + D_MODEL]) * rnn_out
              + jax.nn.sigmoid(ygg[:, D_RNN + D_MODEL:]) * att_out)
    m = _dot(merged.astype(BF16), wo_bf[...])
    o_ref[...] = _layer_norm(ALPHA * x + m_ref[2:3, :] * m, g_ref[...], b_ref[...])


def _mix(layer, x, hs_ctx, hs_lat, at_ctx, at_lat, mods, w_ygg, w_ro, w_ao, w_o, ln_g, ln_b, n_ctx_rows,
         rows_per_latent):
    tm = _ROW_TILE
    nct = n_ctx_rows // tm
    x_specs, xs = _stream_specs(x, tm, n_ctx_rows)
    n_tok = sum(a.shape[0] for a in xs)
    row = pl.BlockSpec((tm, D_MODEL), lambda i: (i, 0))
    ctx_row = pl.BlockSpec((tm, D_MODEL), lambda i: (jnp.minimum(i, nct - 1), 0))
    lat_row = pl.BlockSpec((tm, D_MODEL), lambda i: (jnp.maximum(i - nct, 0), 0))
    sq = (D_MODEL, D_MODEL)
    return pl.pallas_call(
        functools.partial(_mix_kernel, n_x=len(xs), n_ctx_tiles=nct),
        grid=(n_tok // tm,),
        in_specs=x_specs + [ctx_row, lat_row, ctx_row, lat_row,
                  _mod_spec(layer, tm, n_ctx_rows, rows_per_latent),
                  _fixed_spec((D_MODEL, D_RNN + 2 * D_MODEL), (layer,)),
                  _fixed_spec(sq, (layer,)), _fixed_spec(sq, (layer,)), _fixed_spec(sq, (layer,)),
                  _fixed_spec((1, D_MODEL), (layer,)), _fixed_spec((1, D_MODEL), (layer,))],
        out_specs=row,
        out_shape=jax.ShapeDtypeStruct((n_tok, D_MODEL), F32),
        scratch_shapes=[pltpu.VMEM(sq, BF16) for _ in range(3)],
        compiler_params=_params("arbitrary"),
        name="mix_out",
    )(*xs, hs_ctx, hs_lat, at_ctx, at_lat, mods, w_ygg, w_ro, w_ao, w_o, ln_g, ln_b)


def _ffn_kernel(x_ref, m_ref, w1_ref, w3_ref, w2_ref, g_ref, b_ref, o_ref):
    x = x_ref[...]
    h = (x * (1.0 + m_ref[4:5, :]) + m_ref[3:4, :]).astype(BF16)
    f = None
    for c in range(D_FF // _FF_CHUNK):
        cols = slice(c * _FF_CHUNK, (c + 1) * _FF_CHUNK)
        act = (jax.nn.silu(_dot(h, w1_ref[:, cols])) * _dot(h, w3_ref[:, cols])).astype(BF16)
        y = _dot(act, w2_ref[cols, :])
        f = y if f is None else f + y
    o_ref[...] = _layer_norm(ALPHA * x + m_ref[5:6, :] * f, g_ref[...], b_ref[...])


def _ffn_dense(layer, x, mods, w1, w3, w2, ln_g, ln_b, n_ctx_rows, rows_per_latent):
    n_tok = x.shape[0]
    tm = _ROW_TILE
    j = layer // 2
    row = pl.BlockSpec((tm, D_MODEL), lambda i: (i, 0))
    return pl.pallas_call(
        _ffn_kernel,
        grid=(n_tok // tm,),
        in_specs=[row, _mod_spec(layer, tm, n_ctx_rows, rows_per_latent),
                  _fixed_spec((D_MODEL, D_FF), (j,)), _fixed_spec((D_MODEL, D_FF), (j,)),
                  _fixed_spec((D_FF, D_MODEL), (j,)),
                  _fixed_spec((1, D_MODEL), (layer,)), _fixed_spec((1, D_MODEL), (layer,))],
        out_specs=row,
        out_shape=jax.ShapeDtypeStruct((n_tok, D_MODEL), F32),
        compiler_params=_params("arbitrary"),
        name="ffn_dense",
    )(x, mods, w1, w3, w2, ln_g, ln_b)


def _split_bf16(a):
    hi = a.astype(BF16)
    return hi, (a - hi.astype(F32)).astype(BF16)


def _route_kernel(x_ref, m_ref, wr_ref, br_ref, h_ref, ids_ref, gw_ref, cnt_ref, run_scr, tri_scr):
    tm = x_ref.shape[0]

    @pl.when(pl.program_id(0) == 0)
    def _():
        run_scr[...] = jnp.zeros_like(run_scr)
        r = lax.broadcasted_iota(jnp.int32, (tm, tm), 0)
        c = lax.broadcasted_iota(jnp.int32, (tm, tm), 1)
        tri_scr[...] = jnp.where(c < r, 1.0, 0.0).astype(BF16)

    h = x_ref[...] * (1.0 + m_ref[4:5, :]) + m_ref[3:4, :]
    h_ref[...] = h
    h_hi, h_lo = _split_bf16(h)

    w_hi, w_lo = _split_bf16(wr_ref[...])
    logits = _dot(h_hi, w_hi) + (_dot(h_hi, w_lo) + _dot(h_lo, w_hi)) + br_ref[...]
    lane = lax.broadcasted_iota(jnp.int32, logits.shape, 1)
    big = jnp.int32(N_EXPERTS)
    v1 = jnp.max(logits, axis=-1, keepdims=True)
    i1 = jnp.min(jnp.where(logits == v1, lane, big), axis=-1, keepdims=True)
    rest = jnp.where(lane == i1, -jnp.inf, logits)
    v2 = jnp.max(rest, axis=-1, keepdims=True)
    i2 = jnp.min(jnp.where(rest == v2, lane, big), axis=-1, keepdims=True)
    e2 = jnp.exp(v2 - v1)
    g1 = 1.0 / (1.0 + e2)
    g2 = e2 / (1.0 + e2)

    m1, m2 = lane == i1, lane == i2
    member = jnp.where(m1, 1.0, 0.0) + jnp.where(m2, 1.0, 0.0)
    before = _dot(tri_scr[...], member.astype(BF16)) + run_scr[...]
    r1 = jnp.sum(jnp.where(m1, before, 0.0), axis=-1, keepdims=True).astype(jnp.int32)
    r2 = jnp.sum(jnp.where(m2, before, 0.0), axis=-1, keepdims=True).astype(jnp.int32)
    run_scr[...] += jnp.sum(member, axis=0, keepdims=True)

    ids_ref[...] = jnp.where(lane == 0, i1, jnp.where(lane == 1, i2, jnp.where(lane == 2, r1, r2)))
    gw_ref[...] = jnp.where(lane == 0, g1, g2)
    cnt_ref[...] = run_scr[...].astype(jnp.int32)


def _moe_route(layer, x, mods, w_router, b_router, n_ctx_rows, rows_per_latent):
    n_tok = x.shape[0]
    tm = _ROUTE_TILE
    j = layer // 2
    row = lambda w: pl.BlockSpec((tm, w), lambda i: (i, 0))
    return pl.pallas_call(
        _route_kernel,
        grid=(n_tok // tm,),
        in_specs=[row(D_MODEL), _mod_spec(layer, tm, n_ctx_rows, rows_per_latent),
                  _fixed_spec((D_MODEL, N_EXPERTS), (j,)), _fixed_spec((1, N_EXPERTS), (j,))],
        out_specs=[row(D_MODEL), row(N_EXPERTS), row(N_EXPERTS),
                   pl.BlockSpec((1, N_EXPERTS), lambda i: (0, 0))],
        out_shape=[jax.ShapeDtypeStruct((n_tok, D_MODEL), F32),
                   jax.ShapeDtypeStruct((n_tok, N_EXPERTS), jnp.int32),
                   jax.ShapeDtypeStruct((n_tok, N_EXPERTS), F32),
                   jax.ShapeDtypeStruct((1, N_EXPERTS), jnp.int32)],
        scratch_shapes=[pltpu.VMEM((1, N_EXPERTS), F32), pltpu.VMEM((tm, tm), BF16)],
        compiler_params=_params("arbitrary"),
        name="moe_route",
    )(x, mods, w_router, b_router)


def _tables_kernel(er_ref, g_ref, cnt_ref, exp_ref, on_ref, slot_ref, start_scr, *, n_tok, tm):
    n_tiles = exp_ref.shape[0]
    ends = []
    total = jnp.int32(0)
    for e in range(N_EXPERTS):
        start_scr[e] = total
        total = total + ((cnt_ref[0, e] + (tm - 1)) // tm) * tm
        ends.append(total)

    first_lane = lax.broadcasted_iota(jnp.int32, (1, LANE), 1) == 0
    slot_ref[...] = jnp.broadcast_to(jnp.where(first_lane, jnp.float32(n_tok), 0.0), slot_ref.shape)

    def place(t, carry):
        tok = jnp.asarray(t, jnp.int32).astype(F32)
        for k in range(TOP_K):
            slot = start_scr[er_ref[k * n_tok + t]] + er_ref[(TOP_K + k) * n_tok + t]
            slot_ref[pl.ds(slot, 1), :] = jnp.where(first_lane, tok, g_ref[k * n_tok + t])
        return carry

    lax.fori_loop(0, n_tok, place, 0, unroll=8)

    def tiles(i, carry):
        first = i * tm
        owner = jnp.int32(0)
        for e in range(N_EXPERTS):
            owner = owner + jnp.where(first >= ends[e], 1, 0)
        exp_ref[i] = jnp.minimum(owner, N_EXPERTS - 1)
        on_ref[i] = jnp.where(first < total, 1, 0)
        return carry

    lax.fori_loop(0, n_tiles, tiles, 0)


def _slot_tables(ids, gw, counts, n_tok):
    tm = _MOE_SLOT_TILE
    n_slots = TOP_K * n_tok + N_EXPERTS * tm
    n_tiles = n_slots // tm
    smem = pl.BlockSpec(memory_space=pltpu.SMEM)
    vmem = pl.BlockSpec(memory_space=pltpu.VMEM)
    tile_exp, tile_on, slot_rows = pl.pallas_call(
        functools.partial(_tables_kernel, n_tok=n_tok, tm=tm),
        in_specs=[smem, smem, smem],
        out_specs=[smem, smem, vmem],
        out_shape=[jax.ShapeDtypeStruct((n_tiles,), jnp.int32), jax.ShapeDtypeStruct((n_tiles,), jnp.int32),
                   jax.ShapeDtypeStruct((n_slots, LANE), F32)],
        scratch_shapes=[pltpu.SMEM((N_EXPERTS,), jnp.int32)],
        compiler_params=pltpu.CompilerParams(vmem_limit_bytes=VMEM_LIMIT),
        name="moe_tables",
    )(ids[:, :2 * TOP_K].T.reshape(-1), gw[:, :TOP_K].T.reshape(-1), counts)
    return tile_exp, tile_on, slot_rows[:, 0].astype(jnp.int32), slot_rows


def _moe_up_kernel(exp_ref, on_ref, tok_ref, h_ref, w1_ref, w3_ref, act_ref, xg0_scr, xg1_scr, *,
                   n_tok, n_tiles):
    t = pl.program_id(0)
    tm = xg0_scr.shape[0]

    def gather_row(tile, r, dst):
        src = jnp.minimum(tok_ref[tile * tm + r], n_tok - 1)
        dst[pl.ds(r, 1), :] = h_ref[pl.ds(src, 1), :]

    @pl.when(t == 0)
    def _():
        def body(r, carry):
            gather_row(0, r, xg0_scr)
            return carry

        lax.fori_loop(0, tm, body, 0, unroll=8)

    for parity, (cur, nxt) in enumerate(((xg0_scr, xg1_scr), (xg1_scr, xg0_scr))):
        @pl.when(jnp.logical_and(on_ref[t] == 1, t % 2 == parity))
        def _():
            h = cur[...].astype(BF16)
            nxt_tile = jnp.minimum(t + 1, n_tiles - 1)
            for r in range(tm):
                gather_row(nxt_tile, r, nxt)
            act_ref[...] = (jax.nn.silu(_dot(h, w1_ref[...])) * _dot(h, w3_ref[...])).astype(BF16)

    @pl.when(on_ref[t] == 0)
    def _():
        act_ref[...] = jnp.zeros_like(act_ref)


def _moe_up(layer, tables, h, w1, w3):
    tile_exp, tile_on, slot_tok, _ = tables
    n_tok = h.shape[0]
    tm = _MOE_SLOT_TILE
    n_tiles = tile_exp.shape[0]
    j = layer // 2
    wspec = pl.BlockSpec((None, None, D_MODEL, D_EXPERT), lambda t, ex, on, tk: (j, ex[t], 0, 0))
    return pl.pallas_call(
        functools.partial(_moe_up_kernel, n_tok=n_tok, n_tiles=n_tiles),
        grid_spec=pltpu.PrefetchScalarGridSpec(
            num_scalar_prefetch=3,
            grid=(n_tiles,),
            in_specs=[pl.BlockSpec((n_tok, D_MODEL), lambda t, ex, on, tk: (0, 0),
                                   pipeline_mode=pl.Buffered(1)),
                      wspec, wspec],
            out_specs=pl.BlockSpec((tm, D_EXPERT), lambda t, ex, on, tk: (t, 0)),
            scratch_shapes=[pltpu.VMEM((tm, D_MODEL), F32), pltpu.VMEM((tm, D_MODEL), F32)],
        ),
        out_shape=jax.ShapeDtypeStruct((n_tiles * tm, D_EXPERT), BF16),
        compiler_params=_params("arbitrary"),
        name="moe_up",
    )(tile_exp, tile_on, slot_tok, h, w1, w3)


def _moe_down_kernel(exp_ref, on_ref, tok_ref, act_ref, gate_ref, w2_ref, x_ref, m_ref, g_ref, b_ref,
                     *rest, n_tiles, ln_rows, n_ctx_ln_tiles, split):
    o_refs, (acc_scr, y0_scr, y1_scr) = rest[:-3], rest[-3:]
    t = pl.program_id(0)
    tm = y0_scr.shape[0]

    @pl.when(t == 0)
    def _():
        acc_scr[...] = jnp.zeros_like(acc_scr)

    def project(dst):
        dst[...] = _dot(act_ref[...], w2_ref[...]) * gate_ref[:, 1:2]

    def scatter_group(tile, base, ys):
        toks = [tok_ref[tile * tm + base + u] for u in range(SUBLANE)]
        rows = [acc_scr[pl.ds(toks[u], 1), :] for u in range(SUBLANE)]
        for u in range(SUBLANE):
            acc_scr[pl.ds(toks[u], 1), :] = rows[u] + ys[u:u + 1, :]

    last = n_tiles - 1
    project_now = jnp.logical_and(t < n_tiles, on_ref[jnp.minimum(t, last)] == 1)
    add_prev = jnp.logical_and(jnp.logical_and(t >= 1, t <= n_tiles), on_ref[jnp.clip(t - 1, 0, last)] == 1)
    for parity, (cur, prv) in enumerate(((y0_scr, y1_scr), (y1_scr, y0_scr))):
        mine = t % 2 == parity

        @pl.when(jnp.logical_and(mine, jnp.logical_and(project_now, add_prev)))
        def _():
            project(cur)
            for g in range(tm // SUBLANE):
                scatter_group(t - 1, g * SUBLANE, prv[g * SUBLANE:(g + 1) * SUBLANE, :])

        @pl.when(jnp.logical_and(mine, jnp.logical_and(project_now, jnp.logical_not(add_prev))))
        def _():
            project(cur)

        @pl.when(jnp.logical_and(mine, jnp.logical_and(add_prev, jnp.logical_not(project_now))))
        def _():
            def body(g, carry):
                base = pl.multiple_of(g * SUBLANE, SUBLANE)
                scatter_group(t - 1, base, prv[pl.ds(base, SUBLANE), :])
                return carry

            lax.fori_loop(0, tm // SUBLANE, body, 0)

    @pl.when(t > n_tiles)
    def _():
        r0 = pl.multiple_of((t - n_tiles - 1) * ln_rows, ln_rows)
        f = acc_scr[pl.ds(r0, ln_rows), :]
        out = _layer_norm(ALPHA * x_ref[...] + m_ref[5:6, :] * f, g_ref[...], b_ref[...])
        if split:
            is_ctx = t - n_tiles - 1 < n_ctx_ln_tiles

            @pl.when(is_ctx)
            def _():
                o_refs[0][...] = out

            @pl.when(jnp.logical_not(is_ctx))
            def _():
                o_refs[1][...] = out
        else:
            o_refs[0][...] = out


def _moe_down(layer, tables, act, x, mods, w2, ln_g, ln_b, n_ctx_rows, rows_per_latent, split):
    tile_exp, tile_on, slot_tok, slot_gate = tables
    n_tok = x.shape[0]
    tm = _MOE_SLOT_TILE
    tl = _MOE_LN_TILE
    n_tiles = tile_exp.shape[0]
    j = layer // 2
    slot = lambda t: jnp.minimum(t, n_tiles - 1)
    ln = lambda t: jnp.maximum(t - n_tiles - 1, 0)
    n_ctx_tiles, per = n_ctx_rows // tl, rows_per_latent // tl

    def mod_index(t, *_):
        i = ln(t)
        return (layer, jnp.where(i < n_ctx_tiles, 0, 1 + (i - n_ctx_tiles) // per), 0, 0)

    fixed = lambda shape, lead: pl.BlockSpec((None,) + shape, lambda *_: (lead,) + (0,) * len(shape),
                                             pipeline_mode=pl.Buffered(1))
    if split:
        out_specs = [pl.BlockSpec((tl, D_MODEL), lambda t, *_: (jnp.minimum(ln(t), n_ctx_tiles - 1), 0)),
                     pl.BlockSpec((tl, D_MODEL), lambda t, *_: (jnp.maximum(ln(t) - n_ctx_tiles, 0), 0))]
        out_shape = [jax.ShapeDtypeStruct((n_ctx_rows, D_MODEL), F32),
                     jax.ShapeDtypeStruct((n_tok - n_ctx_rows, D_MODEL), F32)]
    else:
        out_specs = pl.BlockSpec((tl, D_MODEL), lambda t, *_: (ln(t), 0))
        out_shape = jax.ShapeDtypeStruct((n_tok, D_MODEL), F32)
    return pl.pallas_call(
        functools.partial(_moe_down_kernel, n_tiles=n_tiles, ln_rows=tl, n_ctx_ln_tiles=n_ctx_tiles,
                          split=split),
        grid_spec=pltpu.PrefetchScalarGridSpec(
            num_scalar_prefetch=3,
            grid=(n_tiles + 1 + n_tok // tl,),
            in_specs=[pl.BlockSpec((tm, D_EXPERT), lambda t, ex, *_: (slot(t), 0)),
                      pl.BlockSpec((tm, LANE), lambda t, ex, *_: (slot(t), 0)),
                      pl.BlockSpec((None, None, D_EXPERT, D_MODEL), lambda t, ex, *_: (j, ex[slot(t)], 0, 0)),
                      pl.BlockSpec((tl, D_MODEL), lambda t, *_: (ln(t), 0)),
                      pl.BlockSpec((None, None, 6, D_MODEL), mod_index),
                      fixed((1, D_MODEL), layer), fixed((1, D_MODEL), layer)],
            out_specs=out_specs,
            scratch_shapes=[pltpu.VMEM((n_tok + SUBLANE, D_MODEL), F32), pltpu.VMEM((tm, D_MODEL), F32),
                            pltpu.VMEM((tm, D_MODEL), F32)],
        ),
        out_shape=out_shape,
        compiler_params=_params("arbitrary"),
        name="moe_down",
    )(tile_exp, tile_on, slot_tok, act, slot_gate, w2, x, mods, ln_g, ln_b)


def _ffn_moe(layer, x, mods, w_router, b_router, w1, w3, w2, ln_g, ln_b, n_ctx_rows, rows_per_latent,
             split=False):
    h, ids, gw, counts = _moe_route(layer, x, mods, w_router, b_router, n_ctx_rows, rows_per_latent)
    tables = _slot_tables(ids, gw, counts, x.shape[0])
    act = _moe_up(layer, tables, h, w1, w3)
    return _moe_down(layer, tables, act, x, mods, w2, ln_g, ln_b, n_ctx_rows, rows_per_latent, split)


def _rot_cols(w):
    ws = w.reshape(w.shape[:-1] + (2, 2, ROPE_AXIS // 2))
    return jnp.stack([-ws[..., 1, :], ws[..., 0, :]], axis=-2).reshape(w.shape)


def _pad_rope_cols(w):
    return jnp.pad(w, ((0, 0),) * (w.ndim - 1) + ((ROPE_OFF, HEAD_PAD - ROPE_OFF - QK_ROPE),))


def _rope_tables(rows):
    r = jnp.repeat(jnp.arange(rows, dtype=F32), GRID_W)
    col = jnp.tile(jnp.arange(GRID_W, dtype=F32), rows)
    freqs = ROPE_THETA ** (-jnp.arange(0, ROPE_AXIS, 2, dtype=F32) / ROPE_AXIS)
    ar = r[:, None] * freqs
    ac = col[:, None] * freqs
    ang = jnp.concatenate([ar, ar, ac, ac], axis=-1)
    cos, sin = jnp.cos(ang), jnp.sin(ang)
    n = cos.shape[0]
    cos_p = jnp.concatenate([jnp.ones((n, ROPE_OFF), F32), cos,
                             jnp.zeros((n, HEAD_PAD - ROPE_OFF - QK_ROPE), F32)], axis=-1)
    return cos_p, _pad_rope_cols(sin)


def _prepared_weights(w_in, rg_wa, rg_ba, rg_wx, rg_bx, w_qb, w_kvb):
    bf = lambda a: a.astype(BF16)
    w_kr = w_in[:, :, _COL_KR:_COL_KR + QK_ROPE]
    w_s = jnp.concatenate([w_in[:, :, _COL_Q:_COL_KR], _pad_rope_cols(w_kr),
                           _pad_rope_cols(_rot_cols(w_kr))], axis=-1)
    w_x = w_in[:, :, :D_RNN]
    w_ygg = jnp.concatenate([w_in[:, :, D_RNN:2 * D_RNN], w_in[:, :, _COL_GATE:]], axis=-1)

    hp = N_HEADS * HEAD_PAD

    def head_pad(w, lo):
        w = jnp.pad(w, ((0, 0),) * 3 + ((lo, HEAD_PAD - lo - w.shape[-1]),))
        return w.reshape(DEPTH, w.shape[1], hp)

    wq = w_qb.reshape(DEPTH, Q_LORA, N_HEADS, QK_NOPE + QK_ROPE)
    w_q = head_pad(wq, 0)
    w_q_rot = head_pad(_rot_cols(wq[..., QK_NOPE:]), ROPE_OFF)
    wkv = w_kvb.reshape(DEPTH, KV_LORA, N_HEADS, QK_NOPE + V_HEAD)
    w_k = head_pad(wkv[..., :QK_NOPE], 0)
    v = wkv[..., QK_NOPE:].reshape(DEPTH, KV_LORA, N_HEADS // 2, 2, V_HEAD)
    w_v = jnp.einsum('lcjpd,pq->lcjpqd', v, jnp.eye(2, dtype=F32)).reshape(DEPTH, KV_LORA, hp)

    nblk = RNN_GROUP // RNN_BW
    ng = D_RNN // RNN_GROUP
    kinds = lambda a, x: jnp.stack([a[:, 0], x[:, 0], a[:, 1], x[:, 1]], axis=1)
    wg = kinds(rg_wa, rg_wx).reshape(DEPTH, 4, ng, nblk, RNN_BW, RNN_BW)
    w_gate = jnp.einsum('ltgnjk,nm->lgnjtmk', wg, jnp.eye(nblk, dtype=F32))
    w_gate = w_gate.reshape(DEPTH, ng, RNN_GROUP, 4 * RNN_GROUP)
    b_gate = kinds(rg_ba, rg_bx).reshape(DEPTH, 4, ng, RNN_GROUP).transpose(0, 2, 1, 3)
    b_gate = b_gate.reshape(DEPTH, ng, 1, 4 * RNN_GROUP)
    return dict(w_x=bf(w_x), w_s=bf(w_s), w_ygg=bf(w_ygg), w_q=bf(w_q), w_q_rot=bf(w_q_rot), w_k=bf(w_k), w_v=bf(w_v),
                w_gate=bf(w_gate), b_gate=b_gate)


def kernel(x_prompt, x_sample, cache_ckv, cache_krope, state_rnn, c, c_ctx, w_ada, b_ada, w_in, conv_w, conv_b, rg_wa, rg_ba, rg_wx, rg_bx, rg_lambda, w_rnn_out, q_norm_g, w_qb, kv_norm_g, w_kvb, w_attn_out, w_out, ln1_g, ln1_b, ln2_g, ln2_b, w1_dense, w3_dense, w2_dense, w_router, b_router, w1_exp, w3_exp, w2_exp):
    nb, seq, _ = x_prompt.shape
    db, dseq, _ = x_sample.shape
    n_ctx = nb * seq
    n_lat = db * dseq
    assert n_ctx % _ROUTE_TILE == 0 and dseq % _ROUTE_TILE == 0 and _ROUTE_TILE % _ROW_TILE == 0 and db + 1 <= 8
    assert DEPTH % 2 == 0
    bf = lambda a: a.astype(BF16)
    vec = lambda a: a.reshape(a.shape[0], 1, a.shape[-1])

    cond8 = jnp.concatenate([c_ctx[None, :], c, jnp.zeros((8 - 1 - db, D_MODEL), F32)], axis=0)
    mods = _ada_mods(cond8, w_ada, b_ada)
    cos_p, sin_p = _rope_tables(dseq // GRID_W)
    past_kr = _pad_rope_cols(cache_krope)
    pw = _prepared_weights(w_in, rg_wa, rg_ba, rg_wx, rg_bx, w_qb, w_kvb)
    w1d, w3d, w2d = bf(w1_dense), bf(w3_dense), bf(w2_dense)
    w1e, w3e, w2e = bf(w1_exp), bf(w3_exp), bf(w2_exp)
    qg, kvg, cb = vec(q_norm_g), vec(kv_norm_g), vec(conv_b)
    l1g, l1b, l2g, l2b = vec(ln1_g), vec(ln1_b), vec(ln2_g), vec(ln2_b)
    br = vec(b_router)
    tiles = (n_ctx, dseq)

    x = (x_prompt.reshape(n_ctx, D_MODEL), x_sample.reshape(n_lat, D_MODEL))
    new_ckv, new_kr, new_h = [], [], []
    for l in range(DEPTH):
        xr, cq, ckv, kr, krr = _in_proj(l, x, mods, pw['w_x'], pw['w_s'], qg, kvg, *tiles)
        new_ckv.append(ckv[:n_ctx].reshape(nb, seq, KV_LORA))
        new_kr.append(kr[:n_ctx, ROPE_OFF:ROPE_OFF + QK_ROPE].reshape(nb, seq, QK_ROPE))

        rnn_args = (conv_w, cb, pw['w_gate'], pw['b_gate'], rg_lambda)
        hs_ctx, h_last = _rglru(l, xr, 0, nb, seq, *rnn_args, emit_last=True)
        (hs_lat,) = _rglru(l, xr, n_ctx, db, dseq, *rnn_args, h0=state_rnn)
        new_h.append(h_last)

        att_w = (pw['w_q'], pw['w_k'], pw['w_v'])
        at_ctx = _attention(l, cq, ckv, kr, 0, nb, seq, *att_w)
        at_lat = _attention(l, cq, ckv, kr, n_ctx, db, dseq, *att_w,
                            lat=(krr, cache_ckv, past_kr, cos_p, sin_p, pw['w_q_rot']))

        x = _mix(l, x, hs_ctx, hs_lat, at_ctx, at_lat, mods, pw['w_ygg'], w_rnn_out, w_attn_out, w_out,
                 l1g, l1b, *tiles)
        if l % 2 == 0:
            x = _ffn_dense(l, x, mods, w1d, w3d, w2d, l2g, l2b, *tiles)
        else:
            x = _ffn_moe(l, x, mods, w_router, br, w1e, w3e, w2e, l2g, l2b, *tiles, split=l == DEPTH - 1)

    return (x[0].reshape(nb, seq, D_MODEL), x[1].reshape(db, dseq, D_MODEL),
            jnp.stack(new_ckv, axis=1), jnp.stack(new_kr, axis=1), jnp.stack(new_h, axis=1))
```

```python
import functools
import math

import jax
import jax.numpy as jnp
from jax import lax
from jax.experimental import pallas as pl
from jax.experimental.pallas import tpu as pltpu

F32 = jnp.float32
BF16 = jnp.bfloat16

D_MODEL = 1024
DEPTH = 4
GRID_W = 64
D_RNN = D_MODEL
RNN_BLOCKS = 16
RNN_BW = D_RNN // RNN_BLOCKS
CONV_W = 4
RG_C = 8.0
N_HEADS = 16
QK_NOPE = 64
QK_ROPE = 32
V_HEAD = 64
Q_LORA = 256
KV_LORA = 128
ROPE_AXIS = QK_ROPE // 2
ROPE_THETA = 10000.0
D_FF = 2816
N_EXPERTS = 8
TOP_K = 2
D_EXPERT = 1408
ALPHA = (2 * DEPTH) ** 0.25
LN_EPS = 1e-5
RMS_EPS = 1e-6

LANE = 128
SUBLANE = 8
HEAD_PAD = LANE
ROPE_OFF = QK_NOPE
RNN_GROUP = 256
_RNN_SEQ_TILE = 4
VMEM_LIMIT = 58 * 1024 * 1024

_ROW_TILE = 512
_ROUTE_TILE = 1024
_MOE_SLOT_TILE = 512
_MOE_LN_TILE = 256
_ATT_Q_TILE = 512
_FF_CHUNK = 1408

_COL_Q = 2 * D_RNN
_COL_KR = _COL_Q + Q_LORA + KV_LORA
_COL_GATE = _COL_KR + QK_ROPE


def _dot(a, b):
    return jnp.dot(a, b, preferred_element_type=F32)


def _layer_norm(y, g, b):
    mu = jnp.mean(y, axis=-1, keepdims=True)
    d = y - mu
    var = jnp.mean(d * d, axis=-1, keepdims=True)
    return d * lax.rsqrt(var + LN_EPS) * g + b


def _rms_norm(y, g):
    return y * lax.rsqrt(jnp.mean(y * y, axis=-1, keepdims=True) + RMS_EPS) * g


def _params(*sem):
    return pltpu.CompilerParams(dimension_semantics=sem, vmem_limit_bytes=VMEM_LIMIT)


def _fixed_spec(shape, lead=(), tail=None):
    tail = (0,) * len(shape) if tail is None else tuple(tail)
    index = tuple(lead) + tail
    return pl.BlockSpec((None,) * len(lead) + tuple(shape), lambda *_: index,
                        pipeline_mode=pl.Buffered(1))


def _mod_spec(layer, tm, n_ctx_rows, rows_per_latent):
    n_ctx_tiles = n_ctx_rows // tm
    per = rows_per_latent // tm

    def index(i, *_):
        return (layer, jnp.where(i < n_ctx_tiles, 0, 1 + (i - n_ctx_tiles) // per), 0, 0)

    return pl.BlockSpec((None, None, 6, D_MODEL), index)


def _cast_once(pairs):
    @pl.when(pl.program_id(0) == 0)
    def _():
        for src, dst in pairs:
            dst[...] = src[...].astype(BF16)


def _stream_specs(x, tm, n_ctx_rows):
    if isinstance(x, tuple):
        nct = n_ctx_rows // tm
        return ([pl.BlockSpec((tm, D_MODEL), lambda i, *_: (jnp.minimum(i, nct - 1), 0)),
                 pl.BlockSpec((tm, D_MODEL), lambda i, *_: (jnp.maximum(i - nct, 0), 0))], list(x))
    return [pl.BlockSpec((tm, D_MODEL), lambda i, *_: (i, 0))], [x]


def _load_stream(refs, n_ctx_tiles):
    if len(refs) == 2:
        return jnp.where(pl.program_id(0) < n_ctx_tiles, refs[0][...], refs[1][...])
    return refs[0][...]


def _mods_kernel(c_ref, w_ref, b_ref, o_ref):
    s = jax.nn.silu(c_ref[...]).astype(BF16)
    o_ref[...] = _dot(s, w_ref[...].astype(BF16)) + b_ref[...]


def _ada_mods(cond8, w_ada, b_ada):
    out = pl.pallas_call(
        _mods_kernel,
        grid=(DEPTH, 6),
        in_specs=[
            pl.BlockSpec((8, D_MODEL), lambda l, j: (0, 0)),
            pl.BlockSpec((None, D_MODEL, D_MODEL), lambda l, j: (l, 0, j)),
            pl.BlockSpec((None, 1, D_MODEL), lambda l, j: (l, 0, j)),
        ],
        out_specs=pl.BlockSpec((None, 8, D_MODEL), lambda l, j: (l, 0, j)),
        out_shape=jax.ShapeDtypeStruct((DEPTH, 8, 6 * D_MODEL), F32),
        compiler_params=_params("arbitrary", "arbitrary"),
        name="ada_mods",
    )(cond8, w_ada, b_ada.reshape(DEPTH, 1, 6 * D_MODEL))
    return out.reshape(DEPTH, 8, 6, D_MODEL)


_S_COLS = Q_LORA + KV_LORA + 2 * HEAD_PAD


def _inproj_kernel(*refs, n_x, n_ctx_tiles):
    m_ref, wx_ref, ws_ref, qg_ref, kvg_ref, xr_ref, cq_ref, ckv_ref, kr_ref, krr_ref = refs[n_x:]
    x = _load_stream(refs[:n_x], n_ctx_tiles)
    h = (x * (1.0 + m_ref[1:2, :]) + m_ref[0:1, :]).astype(BF16)
    xr_ref[...] = _dot(h, wx_ref[...])
    s = _dot(h, ws_ref[...])
    cq_ref[...] = _rms_norm(s[:, :Q_LORA], qg_ref[...]).astype(BF16)
    ckv_ref[...] = _rms_norm(s[:, Q_LORA:Q_LORA + KV_LORA], kvg_ref[...])
    kr_ref[...] = s[:, Q_LORA + KV_LORA:Q_LORA + KV_LORA + HEAD_PAD]
    krr_ref[...] = s[:, Q_LORA + KV_LORA + HEAD_PAD:]


def _in_proj(layer, x, mods, w_x, w_s, q_g, kv_g, n_ctx_rows, rows_per_latent):
    tm = _ROUTE_TILE
    x_specs, xs = _stream_specs(x, tm, n_ctx_rows)
    n_tok = sum(a.shape[0] for a in xs)
    row = lambda w: pl.BlockSpec((tm, w), lambda i: (i, 0))
    return pl.pallas_call(
        functools.partial(_inproj_kernel, n_x=len(xs), n_ctx_tiles=n_ctx_rows // tm),
        grid=(n_tok // tm,),
        in_specs=x_specs + [
            _mod_spec(layer, tm, n_ctx_rows, rows_per_latent),
            _fixed_spec((D_MODEL, D_RNN), (layer,)),
            _fixed_spec((D_MODEL, _S_COLS), (layer,)),
            _fixed_spec((1, Q_LORA), (layer,)),
            _fixed_spec((1, KV_LORA), (layer,)),
        ],
        out_specs=[row(D_RNN), row(Q_LORA), row(KV_LORA), row(HEAD_PAD), row(HEAD_PAD)],
        out_shape=[
            jax.ShapeDtypeStruct((n_tok, D_RNN), F32),
            jax.ShapeDtypeStruct((n_tok, Q_LORA), BF16),
            jax.ShapeDtypeStruct((n_tok, KV_LORA), F32),
            jax.ShapeDtypeStruct((n_tok, HEAD_PAD), F32),
            jax.ShapeDtypeStruct((n_tok, HEAD_PAD), F32),
        ],
        compiler_params=_params("arbitrary"),
        name="in_proj",
    )(*xs, mods, w_x, w_s, q_g, kv_g)


def _segment_pitch(seg):
    assert seg % SUBLANE == 0
    return seg if (seg // SUBLANE) % 2 == 1 else seg + SUBLANE


def _rglru_kernel(*refs, seq, bt, has_h0, emit_last):
    it = iter(refs)
    x_ref, cw_ref, cb_ref, wg_ref, bg_ref, lam_ref = (next(it) for _ in range(6))
    h0_ref = next(it) if has_h0 else None
    hs_ref = next(it)
    hl_ref = next(it) if emit_last else None
    a_refs = (next(it), next(it))
    b_refs = (next(it), next(it))
    dg = RNN_GROUP
    nslab = dg // LANE
    seg = seq // SUBLANE
    pitch = _segment_pitch(seg)

    row = lax.broadcasted_iota(jnp.int32, (seq, dg), 0)
    decay_rate = [RG_C * jax.nn.softplus(-lam_ref[d:d + 1, :]) for d in range(2)]
    for b in range(bt):
        rows = slice(b * seq, (b + 1) * seq)
        x = x_ref[rows, :]
        xm1 = jnp.where(row >= 1, pltpu.roll(x, 1, 0), 0.0)
        xp1 = jnp.where(row < seq - 1, pltpu.roll(x, seq - 1, 0), 0.0)
        xp2 = jnp.where(row < seq - 2, pltpu.roll(x, seq - 2, 0), 0.0)
        xc = cb_ref[...] + xm1 * cw_ref[0:1, :]
        xc = xc + x * cw_ref[1:2, :]
        xc = xc + xp1 * cw_ref[2:3, :]
        xc = xc + xp2 * cw_ref[3:4, :]
        gates = _dot(xc.astype(BF16), wg_ref[...]) + bg_ref[...]
        for d in range(2):
            r = jax.nn.sigmoid(gates[:, (2 * d) * dg:(2 * d + 1) * dg])
            i = jax.nn.sigmoid(gates[:, (2 * d + 1) * dg:(2 * d + 2) * dg])
            neg_log_a = r * decay_rate[d]
            a = jnp.exp(-neg_log_a)
            y = jnp.tanh(neg_log_a) * (a * a + 1.0)
            bq = jnp.where(y > 0.0, y * lax.rsqrt(y), 0.0) * (i * xc)
            for s in range(SUBLANE):
                for c in range(nslab):
                    src = (slice(s * seg, (s + 1) * seg), slice(c * LANE, (c + 1) * LANE))
                    dst = slice(s * pitch, s * pitch + seg)
                    a_refs[d][b * nslab + c, dst, :] = a[src]
                    b_refs[d][b * nslab + c, dst, :] = bq[src]

    nch = bt * nslab
    zeros = tuple(jnp.zeros((SUBLANE, LANE), F32) for _ in range(nch))
    ones = tuple(jnp.ones((SUBLANE, LANE), F32) for _ in range(nch))

    def step(j, carry):
        out = []
        for d, t in ((0, j), (1, seg - 1 - j)):
            h_prev, p_prev = carry[2 * d], carry[2 * d + 1]
            h_new, p_new = [], []
            idx = pl.ds(t, SUBLANE, stride=pitch)
            for ch in range(nch):
                a = a_refs[d].at[ch][idx, :]
                h = a * h_prev[ch] + b_refs[d].at[ch][idx, :]
                p = a * p_prev[ch]
                b_refs[d].at[ch][idx, :] = h
                a_refs[d].at[ch][idx, :] = p
                h_new.append(h)
                p_new.append(p)
            out += [tuple(h_new), tuple(p_new)]
        return tuple(out)

    h_f, p_f, h_b, p_b = lax.fori_loop(0, seg, step, (zeros, ones, zeros, ones), unroll=2)

    for b in range(bt):
        for c in range(nslab):
            ch = b * nslab + c
            lanes = slice(c * LANE, (c + 1) * LANE)
            row_of = lambda v, s: v[s:s + 1, :]
            if has_h0:
                enter_f, enter_b = h0_ref[b, 0:1, lanes], h0_ref[b, 1:2, lanes]
            else:
                enter_f = enter_b = jnp.zeros((1, LANE), F32)
            ent_f, ent_b = [None] * SUBLANE, [None] * SUBLANE
            for s in range(SUBLANE):
                ent_f[s] = enter_f
                enter_f = row_of(h_f[ch], s) + row_of(p_f[ch], s) * enter_f
            for s in reversed(range(SUBLANE)):
                ent_b[s] = enter_b
                enter_b = row_of(h_b[ch], s) + row_of(p_b[ch], s) * enter_b
            if emit_last:
                hl_ref[b, 0:1, lanes] = enter_f
                hl_ref[b, 1:2, lanes] = enter_b
            for s in range(SUBLANE):
                src = slice(s * pitch, s * pitch + seg)
                fwd = b_refs[0][ch, src, :] + a_refs[0][ch, src, :] * ent_f[s]
                bwd = b_refs[1][ch, src, :] + a_refs[1][ch, src, :] * ent_b[s]
                hs_ref[b * seq + s * seg:b * seq + (s + 1) * seg, lanes] = fwd + bwd


def _rglru(layer, xr, row_start, nseq, seq, conv_w, conv_b, w_gate, b_gate, lam, h0=None, emit_last=False):
    bt = _RNN_SEQ_TILE
    dg = RNN_GROUP
    ng = D_RNN // dg
    rows = bt * seq
    assert nseq % bt == 0 and row_start % rows == 0
    off = row_start // rows
    in_specs = [
        pl.BlockSpec((rows, dg), lambda bi, g: (off + bi, g)),
        pl.BlockSpec((None, CONV_W, dg), lambda bi, g: (layer, 0, g)),
        pl.BlockSpec((None, 1, dg), lambda bi, g: (layer, 0, g)),
        pl.BlockSpec((None, None, dg, 4 * dg), lambda bi, g: (layer, g, 0, 0)),
        pl.BlockSpec((None, None, 1, 4 * dg), lambda bi, g: (layer, g, 0, 0)),
        pl.BlockSpec((None, 2, dg), lambda bi, g: (layer, 0, g)),
    ]
    args = [xr, conv_w, conv_b, w_gate, b_gate, lam]
    if h0 is not None:
        in_specs.append(pl.BlockSpec((bt, None, 2, dg), lambda bi, g: (bi, layer, 0, g)))
        args.append(h0)
    out_specs = [pl.BlockSpec((rows, dg), lambda bi, g: (bi, g))]
    out_shape = [jax.ShapeDtypeStruct((nseq * seq, D_RNN), F32)]
    if emit_last:
        out_specs.append(pl.BlockSpec((bt, 2, dg), lambda bi, g: (bi, 0, g)))
        out_shape.append(jax.ShapeDtypeStruct((nseq, 2, D_RNN), F32))
    return pl.pallas_call(
        functools.partial(_rglru_kernel, seq=seq, bt=bt, has_h0=h0 is not None, emit_last=emit_last),
        grid=(nseq // bt, ng),
        in_specs=in_specs,
        out_specs=out_specs,
        out_shape=out_shape,
        scratch_shapes=[pltpu.VMEM((bt * (dg // LANE), SUBLANE * _segment_pitch(seq // SUBLANE), LANE), F32)
                        for _ in range(4)],
        compiler_params=_params("arbitrary", "arbitrary"),
        name=f"rglru_{seq}",
    )(*args)


def _attn_kernel(*refs, past, tq, rope):
    it = iter(refs)
    cq_ref, ckv_ref, kr_ref = (next(it) for _ in range(3))
    if rope:
        krr_ref, pckv_ref, pkr_ref, cos_ref, sin_ref = (next(it) for _ in range(5))
    wq_ref = next(it)
    wqr_ref = next(it) if rope else None
    wk_ref, wv_ref = next(it), next(it)
    o_ref = next(it)
    k_scr, v_scr = next(it), next(it)
    qi = pl.program_id(1)
    scale = 1.0 / math.sqrt(QK_NOPE + QK_ROPE)

    def fill(rows, ckv, k_rope):
        c = ckv.astype(BF16)
        k = _dot(c, wk_ref[...]) + jnp.tile(k_rope, (1, N_HEADS))
        k_scr[rows, :] = k.astype(BF16)
        v_scr[rows, :] = _dot(c, wv_ref[...]).astype(BF16)

    @pl.when(qi == 0)
    def _():
        seq = ckv_ref.shape[0]
        if rope:
            fill(pl.ds(0, past), pckv_ref[...], pkr_ref[...])
            fill(pl.ds(past, seq), ckv_ref[...],
                 kr_ref[...] * cos_ref[...] + krr_ref[...] * sin_ref[...])
        else:
            fill(pl.ds(0, seq), ckv_ref[...], kr_ref[...])

    cq = cq_ref[...]
    qa = _dot(cq, wq_ref[...])
    if rope:
        qb = _dot(cq, wqr_ref[...])
        q0 = pl.multiple_of(qi * tq, tq)
        cos_t = cos_ref[pl.ds(q0, tq), :]
        sin_t = sin_ref[pl.ds(q0, tq), :]
    for j in range(N_HEADS // 2):
        pair = None
        for h in (2 * j, 2 * j + 1):
            cols = slice(h * HEAD_PAD, (h + 1) * HEAD_PAD)
            qh = qa[:, cols]
            if rope:
                qh = qh * cos_t + qb[:, cols] * sin_t
            s = lax.dot_general(qh.astype(BF16), k_scr[:, cols], (((1,), (1,)), ((), ())),
                                preferred_element_type=F32)
            e = jnp.exp((s - jnp.max(s, axis=-1, keepdims=True)) * scale)
            denom = jnp.sum(e, axis=-1, keepdims=True)
            o = _dot(e.astype(BF16), v_scr[:, cols]) * (1.0 / denom)
            pair = o if pair is None else pair + o
        o_ref[:, j * LANE:(j + 1) * LANE] = pair.astype(BF16)


def _attention(layer, cq, ckv, kr, row_start, nseq, seq, w_q, w_k, w_v, lat=None):
    rope = lat is not None
    tq = min(_ATT_Q_TILE, seq)
    assert seq % tq == 0 and row_start % seq == 0
    past = lat[1].shape[2] if rope else 0
    total = past + seq
    hp = N_HEADS * HEAD_PAD
    nq = seq // tq
    s_off, q_off = row_start // seq, row_start // tq
    full = lambda w: pl.BlockSpec((seq, w), lambda b, q: (s_off + b, 0))
    q_rows = lambda w: pl.BlockSpec((tq, w), lambda b, q: (q_off + b * nq + q, 0))
    in_specs = [q_rows(Q_LORA), full(KV_LORA), full(HEAD_PAD)]
    args = [cq, ckv, kr]
    if rope:
        krr, pckv, pkr, cos, sin, w_q_rot = lat
        in_specs += [full(HEAD_PAD),
                     pl.BlockSpec((None, None, past, KV_LORA), lambda b, q: (b, layer, 0, 0)),
                     pl.BlockSpec((None, None, past, HEAD_PAD), lambda b, q: (b, layer, 0, 0)),
                     _fixed_spec((seq, HEAD_PAD)), _fixed_spec((seq, HEAD_PAD))]
        args += [krr, pckv, pkr, cos, sin]
    in_specs.append(_fixed_spec((Q_LORA, hp), (layer,)))
    args.append(w_q)
    if rope:
        in_specs.append(_fixed_spec((Q_LORA, hp), (layer,)))
        args.append(w_q_rot)
    in_specs += [_fixed_spec((KV_LORA, hp), (layer,)), _fixed_spec((KV_LORA, hp), (layer,))]
    args += [w_k, w_v]
    return pl.pallas_call(
        functools.partial(_attn_kernel, past=past, tq=tq, rope=rope),
        grid=(nseq, nq),
        in_specs=in_specs,
        out_specs=pl.BlockSpec((tq, N_HEADS * V_HEAD), lambda b, q: (b * nq + q, 0)),
        out_shape=jax.ShapeDtypeStruct((nseq * seq, N_HEADS * V_HEAD), BF16),
        scratch_shapes=[pltpu.VMEM((total, hp), BF16), pltpu.VMEM((total, hp), BF16)],
        compiler_params=_params("arbitrary", "arbitrary"),
        name=f"attn_{seq}",
    )(*args)


def _mix_kernel(*refs, n_x, n_ctx_tiles):
    (hsc_ref, hsl_ref, atc_ref, atl_ref, m_ref, wygg_ref, wro_ref, wao_ref, wo_ref,
     g_ref, b_ref, o_ref, wro_bf, wao_bf, wo_bf) = refs[n_x:]
    _cast_once([(wro_ref, wro_bf), (wao_ref, wao_bf), (wo_ref, wo_bf)])
    is_ctx = pl.program_id(0) < n_ctx_tiles
    hs = jnp.where(is_ctx, hsc_ref[...], hsl_ref[...])
    at = jnp.where(is_ctx, atc_ref[...], atl_ref[...])
    x = _load_stream(refs[:n_x], n_ctx_tiles)
    h = (x * (1.0 + m_ref[1:2, :]) + m_ref[0:1, :]).astype(BF16)
    ygg = _dot(h, wygg_ref[...])
    u = (hs * jax.nn.gelu(ygg[:, :D_RNN])).astype(BF16)
    rnn_out = _dot(u, wro_bf[...])
    att_out = _dot(at, wao_bf[...])
    merged = (jax.nn.sigmoid(ygg[:, D_RNN:D_RNN + D_MODEL]) * rnn_out
              + jax.nn.sigmoid(ygg[:, D_RNN + D_MODEL:]) * att_out)
    m = _dot(merged.astype(BF16), wo_bf[...])
    o_ref[...] = _layer_norm(ALPHA * x + m_ref[2:3, :] * m, g_ref[...], b_ref[...])


def _mix(layer, x, hs_ctx, hs_lat, at_ctx, at_lat, mods, w_ygg, w_ro, w_ao, w_o, ln_g, ln_b, n_ctx_rows,
         rows_per_latent):
    tm = _ROW_TILE
    nct = n_ctx_rows // tm
    x_specs, xs = _stream_specs(x, tm, n_ctx_rows)
    n_tok = sum(a.shape[0] for a in xs)
    row = pl.BlockSpec((tm, D_MODEL), lambda i: (i, 0))
    ctx_row = pl.BlockSpec((tm, D_MODEL), lambda i: (jnp.minimum(i, nct - 1), 0))
    lat_row = pl.BlockSpec((tm, D_MODEL), lambda i: (jnp.maximum(i - nct, 0), 0))
    sq = (D_MODEL, D_MODEL)
    return pl.pallas_call(
        functools.partial(_mix_kernel, n_x=len(xs), n_ctx_tiles=nct),
        grid=(n_tok // tm,),
        in_specs=x_specs + [ctx_row, lat_row, ctx_row, lat_row,
                  _mod_spec(layer, tm, n_ctx_rows, rows_per_latent),
                  _fixed_spec((D_MODEL, D_RNN + 2 * D_MODEL), (layer,)),
                  _fixed_spec(sq, (layer,)), _fixed_spec(sq, (layer,)), _fixed_spec(sq, (layer,)),
                  _fixed_spec((1, D_MODEL), (layer,)), _fixed_spec((1, D_MODEL), (layer,))],
        out_specs=row,
        out_shape=jax.ShapeDtypeStruct((n_tok, D_MODEL), F32),
        scratch_shapes=[pltpu.VMEM(sq, BF16) for _ in range(3)],
        compiler_params=_params("arbitrary"),
        name="mix_out",
    )(*xs, hs_ctx, hs_lat, at_ctx, at_lat, mods, w_ygg, w_ro, w_ao, w_o, ln_g, ln_b)


def _ffn_kernel(x_ref, m_ref, w1_ref, w3_ref, w2_ref, g_ref, b_ref, o_ref):
    x = x_ref[...]
    h = (x * (1.0 + m_ref[4:5, :]) + m_ref[3:4, :]).astype(BF16)
    f = None
    for c in range(D_FF // _FF_CHUNK):
        cols = slice(c * _FF_CHUNK, (c + 1) * _FF_CHUNK)
        act = (jax.nn.silu(_dot(h, w1_ref[:, cols])) * _dot(h, w3_ref[:, cols])).astype(BF16)
        y = _dot(act, w2_ref[cols, :])
        f = y if f is None else f + y
    o_ref[...] = _layer_norm(ALPHA * x + m_ref[5:6, :] * f, g_ref[...], b_ref[...])


def _ffn_dense(layer, x, mods, w1, w3, w2, ln_g, ln_b, n_ctx_rows, rows_per_latent):
    n_tok = x.shape[0]
    tm = _ROW_TILE
    j = layer // 2
    row = pl.BlockSpec((tm, D_MODEL), lambda i: (i, 0))
    return pl.pallas_call(
        _ffn_kernel,
        grid=(n_tok // tm,),
        in_specs=[row, _mod_spec(layer, tm, n_ctx_rows, rows_per_latent),
                  _fixed_spec((D_MODEL, D_FF), (j,)), _fixed_spec((D_MODEL, D_FF), (j,)),
                  _fixed_spec((D_FF, D_MODEL), (j,)),
                  _fixed_spec((1, D_MODEL), (layer,)), _fixed_spec((1, D_MODEL), (layer,))],
        out_specs=row,
        out_shape=jax.ShapeDtypeStruct((n_tok, D_MODEL), F32),
        compiler_params=_params("arbitrary"),
        name="ffn_dense",
    )(x, mods, w1, w3, w2, ln_g, ln_b)


def _split_bf16(a):
    hi = a.astype(BF16)
    return hi, (a - hi.astype(F32)).astype(BF16)


def _route_kernel(x_ref, m_ref, wr_ref, br_ref, h_ref, ids_ref, gw_ref, cnt_ref, run_scr, tri_scr):
    tm = x_ref.shape[0]

    @pl.when(pl.program_id(0) == 0)
    def _():
        run_scr[...] = jnp.zeros_like(run_scr)
        r = lax.broadcasted_iota(jnp.int32, (tm, tm), 0)
        c = lax.broadcasted_iota(jnp.int32, (tm, tm), 1)
        tri_scr[...] = jnp.where(c < r, 1.0, 0.0).astype(BF16)

    h = x_ref[...] * (1.0 + m_ref[4:5, :]) + m_ref[3:4, :]
    h_ref[...] = h
    h_hi, h_lo = _split_bf16(h)

    w_hi, w_lo = _split_bf16(wr_ref[...])
    logits = _dot(h_hi, w_hi) + (_dot(h_hi, w_lo) + _dot(h_lo, w_hi)) + br_ref[...]
    lane = lax.broadcasted_iota(jnp.int32, logits.shape, 1)
    big = jnp.int32(N_EXPERTS)
    v1 = jnp.max(logits, axis=-1, keepdims=True)
    i1 = jnp.min(jnp.where(logits == v1, lane, big), axis=-1, keepdims=True)
    rest = jnp.where(lane == i1, -jnp.inf, logits)
    v2 = jnp.max(rest, axis=-1, keepdims=True)
    i2 = jnp.min(jnp.where(rest == v2, lane, big), axis=-1, keepdims=True)
    e2 = jnp.exp(v2 - v1)
    g1 = 1.0 / (1.0 + e2)
    g2 = e2 / (1.0 + e2)

    m1, m2 = lane == i1, lane == i2
    member = jnp.where(m1, 1.0, 0.0) + jnp.where(m2, 1.0, 0.0)
    before = _dot(tri_scr[...], member.astype(BF16)) + run_scr[...]
    r1 = jnp.sum(jnp.where(m1, before, 0.0), axis=-1, keepdims=True).astype(jnp.int32)
    r2 = jnp.sum(jnp.where(m2, before, 0.0), axis=-1, keepdims=True).astype(jnp.int32)
    run_scr[...] += jnp.sum(member, axis=0, keepdims=True)

    ids_ref[...] = jnp.where(lane == 0, i1, jnp.where(lane == 1, i2, jnp.where(lane == 2, r1, r2)))
    gw_ref[...] = jnp.where(lane == 0, g1, g2)
    cnt_ref[...] = run_scr[...].astype(jnp.int32)


def _moe_route(layer, x, mods, w_router, b_router, n_ctx_rows, rows_per_latent):
    n_tok = x.shape[0]
    tm = _ROUTE_TILE
    j = layer // 2
    row = lambda w: pl.BlockSpec((tm, w), lambda i: (i, 0))
    return pl.pallas_call(
        _route_kernel,
        grid=(n_tok // tm,),
        in_specs=[row(D_MODEL), _mod_spec(layer, tm, n_ctx_rows, rows_per_latent),
                  _fixed_spec((D_MODEL, N_EXPERTS), (j,)), _fixed_spec((1, N_EXPERTS), (j,))],
        out_specs=[row(D_MODEL), row(N_EXPERTS), row(N_EXPERTS),
                   pl.BlockSpec((1, N_EXPERTS), lambda i: (0, 0))],
        out_shape=[jax.ShapeDtypeStruct((n_tok, D_MODEL), F32),
                   jax.ShapeDtypeStruct((n_tok, N_EXPERTS), jnp.int32),
                   jax.ShapeDtypeStruct((n_tok, N_EXPERTS), F32),
                   jax.ShapeDtypeStruct((1, N_EXPERTS), jnp.int32)],
        scratch_shapes=[pltpu.VMEM((1, N_EXPERTS), F32), pltpu.VMEM((tm, tm), BF16)],
        compiler_params=_params("arbitrary"),
        name="moe_route",
    )(x, mods, w_router, b_router)


def _tables_kernel(er_ref, g_ref, cnt_ref, exp_ref, on_ref, slot_ref, start_scr, *, n_tok, tm):
    n_tiles = exp_ref.shape[0]
    ends = []
    total = jnp.int32(0)
    for e in range(N_EXPERTS):
        start_scr[e] = total
        total = total + ((cnt_ref[0, e] + (tm - 1)) // tm) * tm
        ends.append(total)

    first_lane = lax.broadcasted_iota(jnp.int32, (1, LANE), 1) == 0
    slot_ref[...] = jnp.broadcast_to(jnp.where(first_lane, jnp.float32(n_tok), 0.0), slot_ref.shape)

    def place(t, carry):
        tok = jnp.asarray(t, jnp.int32).astype(F32)
        for k in range(TOP_K):
            slot = start_scr[er_ref[k * n_tok + t]] + er_ref[(TOP_K + k) * n_tok + t]
            slot_ref[pl.ds(slot, 1), :] = jnp.where(first_lane, tok, g_ref[k * n_tok + t])
        return carry

    lax.fori_loop(0, n_tok, place, 0, unroll=8)

    def tiles(i, carry):
        first = i * tm
        owner = jnp.int32(0)
        for e in range(N_EXPERTS):
            owner = owner + jnp.where(first >= ends[e], 1, 0)
        exp_ref[i] = jnp.minimum(owner, N_EXPERTS - 1)
        on_ref[i] = jnp.where(first < total, 1, 0)
        return carry

    lax.fori_loop(0, n_tiles, tiles, 0)


def _slot_tables(ids, gw, counts, n_tok):
    tm = _MOE_SLOT_TILE
    n_slots = TOP_K * n_tok + N_EXPERTS * tm
    n_tiles = n_slots // tm
    smem = pl.BlockSpec(memory_space=pltpu.SMEM)
    vmem = pl.BlockSpec(memory_space=pltpu.VMEM)
    tile_exp, tile_on, slot_rows = pl.pallas_call(
        functools.partial(_tables_kernel, n_tok=n_tok, tm=tm),
        in_specs=[smem, smem, smem],
        out_specs=[smem, smem, vmem],
        out_shape=[jax.ShapeDtypeStruct((n_tiles,), jnp.int32), jax.ShapeDtypeStruct((n_tiles,), jnp.int32),
                   jax.ShapeDtypeStruct((n_slots, LANE), F32)],
        scratch_shapes=[pltpu.SMEM((N_EXPERTS,), jnp.int32)],
        compiler_params=pltpu.CompilerParams(vmem_limit_bytes=VMEM_LIMIT),
        name="moe_tables",
    )(ids[:, :2 * TOP_K].T.reshape(-1), gw[:, :TOP_K].T.reshape(-1), counts)
    return tile_exp, tile_on, slot_rows[:, 0].astype(jnp.int32), slot_rows


def _moe_up_kernel(exp_ref, on_ref, tok_ref, h_ref, w1_ref, w3_ref, act_ref, xg0_scr, xg1_scr, *,
                   n_tok, n_tiles):
    t = pl.program_id(0)
    tm = xg0_scr.shape[0]

    def gather_row(tile, r, dst):
        src = jnp.minimum(tok_ref[tile * tm + r], n_tok - 1)
        dst[pl.ds(r, 1), :] = h_ref[pl.ds(src, 1), :]

    @pl.when(t == 0)
    def _():
        def body(r, carry):
            gather_row(0, r, xg0_scr)
            return carry

        lax.fori_loop(0, tm, body, 0, unroll=8)

    for parity, (cur, nxt) in enumerate(((xg0_scr, xg1_scr), (xg1_scr, xg0_scr))):
        @pl.when(jnp.logical_and(on_ref[t] == 1, t % 2 == parity))
        def _():
            h = cur[...].astype(BF16)
            nxt_tile = jnp.minimum(t + 1, n_tiles - 1)
            for r in range(tm):
                gather_row(nxt_tile, r, nxt)
            act_ref[...] = (jax.nn.silu(_dot(h, w1_ref[...])) * _dot(h, w3_ref[...])).astype(BF16)

    @pl.when(on_ref[t] == 0)
    def _():
        act_ref[...] = jnp.zeros_like(act_ref)


def _moe_up(layer, tables, h, w1, w3):
    tile_exp, tile_on, slot_tok, _ = tables
    n_tok = h.shape[0]
    tm = _MOE_SLOT_TILE
    n_tiles = tile_exp.shape[0]
    j = layer // 2
    wspec = pl.BlockSpec((None, None, D_MODEL, D_EXPERT), lambda t, ex, on, tk: (j, ex[t], 0, 0))
    return pl.pallas_call(
        functools.partial(_moe_up_kernel, n_tok=n_tok, n_tiles=n_tiles),
        grid_spec=pltpu.PrefetchScalarGridSpec(
            num_scalar_prefetch=3,
            grid=(n_tiles,),
            in_specs=[pl.BlockSpec((n_tok, D_MODEL), lambda t, ex, on, tk: (0, 0),
                                   pipeline_mode=pl.Buffered(1)),
                      wspec, wspec],
            out_specs=pl.BlockSpec((tm, D_EXPERT), lambda t, ex, on, tk: (t, 0)),
            scratch_shapes=[pltpu.VMEM((tm, D_MODEL), F32), pltpu.VMEM((tm, D_MODEL), F32)],
        ),
        out_shape=jax.ShapeDtypeStruct((n_tiles * tm, D_EXPERT), BF16),
        compiler_params=_params("arbitrary"),
        name="moe_up",
    )(tile_exp, tile_on, slot_tok, h, w1, w3)


def _moe_down_kernel(exp_ref, on_ref, tok_ref, act_ref, gate_ref, w2_ref, x_ref, m_ref, g_ref, b_ref,
                     *rest, n_tiles, ln_rows, n_ctx_ln_tiles, split):
    o_refs, (acc_scr, y0_scr, y1_scr) = rest[:-3], rest[-3:]
    t = pl.program_id(0)
    tm = y0_scr.shape[0]

    @pl.when(t == 0)
    def _():
        acc_scr[...] = jnp.zeros_like(acc_scr)

    def project(dst):
        dst[...] = _dot(act_ref[...], w2_ref[...]) * gate_ref[:, 1:2]

    def scatter_group(tile, base, ys):
        toks = [tok_ref[tile * tm + base + u] for u in range(SUBLANE)]
        rows = [acc_scr[pl.ds(toks[u], 1), :] for u in range(SUBLANE)]
        for u in range(SUBLANE):
            acc_scr[pl.ds(toks[u], 1), :] = rows[u] + ys[u:u + 1, :]

    last = n_tiles - 1
    project_now = jnp.logical_and(t < n_tiles, on_ref[jnp.minimum(t, last)] == 1)
    add_prev = jnp.logical_and(jnp.logical_and(t >= 1, t <= n_tiles), on_ref[jnp.clip(t - 1, 0, last)] == 1)
    for parity, (cur, prv) in enumerate(((y0_scr, y1_scr), (y1_scr, y0_scr))):
        mine = t % 2 == parity

        @pl.when(jnp.logical_and(mine, jnp.logical_and(project_now, add_prev)))
        def _():
            project(cur)
            for g in range(tm // SUBLANE):
                scatter_group(t - 1, g * SUBLANE, prv[g * SUBLANE:(g + 1) * SUBLANE, :])

        @pl.when(jnp.logical_and(mine, jnp.logical_and(project_now, jnp.logical_not(add_prev))))
        def _():
            project(cur)

        @pl.when(jnp.logical_and(mine, jnp.logical_and(add_prev, jnp.logical_not(project_now))))
        def _():
            def body(g, carry):
                base = pl.multiple_of(g * SUBLANE, SUBLANE)
                scatter_group(t - 1, base, prv[pl.ds(base, SUBLANE), :])
                return carry

            lax.fori_loop(0, tm // SUBLANE, body, 0)

    @pl.when(t > n_tiles)
    def _():
        r0 = pl.multiple_of((t - n_tiles - 1) * ln_rows, ln_rows)
        f = acc_scr[pl.ds(r0, ln_rows), :]
        out = _layer_norm(ALPHA * x_ref[...] + m_ref[5:6, :] * f, g_ref[...], b_ref[...])
        if split:
            is_ctx = t - n_tiles - 1 < n_ctx_ln_tiles

            @pl.when(is_ctx)
            def _():
                o_refs[0][...] = out

            @pl.when(jnp.logical_not(is_ctx))
            def _():
                o_refs[1][...] = out
        else:
            o_refs[0][...] = out


def _moe_down(layer, tables, act, x, mods, w2, ln_g, ln_b, n_ctx_rows, rows_per_latent, split):
    tile_exp, tile_on, slot_tok, slot_gate = tables
    n_tok = x.shape[0]
    tm = _MOE_SLOT_TILE
    tl = _MOE_LN_TILE
    n_tiles = tile_exp.shape[0]
    j = layer // 2
    slot = lambda t: jnp.minimum(t, n_tiles - 1)
    ln = lambda t: jnp.maximum(t - n_tiles - 1, 0)
    n_ctx_tiles, per = n_ctx_rows // tl, rows_per_latent // tl

    def mod_index(t, *_):
        i = ln(t)
        return (layer, jnp.where(i < n_ctx_tiles, 0, 1 + (i - n_ctx_tiles) // per), 0, 0)

    fixed = lambda shape, lead: pl.BlockSpec((None,) + shape, lambda *_: (lead,) + (0,) * len(shape),
                                             pipeline_mode=pl.Buffered(1))
    if split:
        out_specs = [pl.BlockSpec((tl, D_MODEL), lambda t, *_: (jnp.minimum(ln(t), n_ctx_tiles - 1), 0)),
                     pl.BlockSpec((tl, D_MODEL), lambda t, *_: (jnp.maximum(ln(t) - n_ctx_tiles, 0), 0))]
        out_shape = [jax.ShapeDtypeStruct((n_ctx_rows, D_MODEL), F32),
                     jax.ShapeDtypeStruct((n_tok - n_ctx_rows, D_MODEL), F32)]
    else:
        out_specs = pl.BlockSpec((tl, D_MODEL), lambda t, *_: (ln(t), 0))
        out_shape = jax.ShapeDtypeStruct((n_tok, D_MODEL), F32)
    return pl.pallas_call(
        functools.partial(_moe_down_kernel, n_tiles=n_tiles, ln_rows=tl, n_ctx_ln_tiles=n_ctx_tiles,
                          split=split),
        grid_spec=pltpu.PrefetchScalarGridSpec(
            num_scalar_prefetch=3,
            grid=(n_tiles + 1 + n_tok // tl,),
            in_specs=[pl.BlockSpec((tm, D_EXPERT), lambda t, ex, *_: (slot(t), 0)),
                      pl.BlockSpec((tm, LANE), lambda t, ex, *_: (slot(t), 0)),
                      pl.BlockSpec((None, None, D_EXPERT, D_MODEL), lambda t, ex, *_: (j, ex[slot(t)], 0, 0)),
                      pl.BlockSpec((tl, D_MODEL), lambda t, *_: (ln(t), 0)),
                      pl.BlockSpec((None, None, 6, D_MODEL), mod_index),
                      fixed((1, D_MODEL), layer), fixed((1, D_MODEL), layer)],
            out_specs=out_specs,
            scratch_shapes=[pltpu.VMEM((n_tok + SUBLANE, D_MODEL), F32), pltpu.VMEM((tm, D_MODEL), F32),
                            pltpu.VMEM((tm, D_MODEL), F32)],
        ),
        out_shape=out_shape,
        compiler_params=_params("arbitrary"),
        name="moe_down",
    )(tile_exp, tile_on, slot_tok, act, slot_gate, w2, x, mods, ln_g, ln_b)


def _ffn_moe(layer, x, mods, w_router, b_router, w1, w3, w2, ln_g, ln_b, n_ctx_rows, rows_per_latent,
             split=False):
    h, ids, gw, counts = _moe_route(layer, x, mods, w_router, b_router, n_ctx_rows, rows_per_latent)
    tables = _slot_tables(ids, gw, counts, x.shape[0])
    act = _moe_up(layer, tables, h, w1, w3)
    return _moe_down(layer, tables, act, x, mods, w2, ln_g, ln_b, n_ctx_rows, rows_per_latent, split)


def _rot_cols(w):
    ws = w.reshape(w.shape[:-1] + (2, 2, ROPE_AXIS // 2))
    return jnp.stack([-ws[..., 1, :], ws[..., 0, :]], axis=-2).reshape(w.shape)


def _pad_rope_cols(w):
    return jnp.pad(w, ((0, 0),) * (w.ndim - 1) + ((ROPE_OFF, HEAD_PAD - ROPE_OFF - QK_ROPE),))


def _rope_tables(rows):
    r = jnp.repeat(jnp.arange(rows, dtype=F32), GRID_W)
    col = jnp.tile(jnp.arange(GRID_W, dtype=F32), rows)
    freqs = ROPE_THETA ** (-jnp.arange(0, ROPE_AXIS, 2, dtype=F32) / ROPE_AXIS)
    ar = r[:, None] * freqs
    ac = col[:, None] * freqs
    ang = jnp.concatenate([ar, ar, ac, ac], axis=-1)
    cos, sin = jnp.cos(ang), jnp.sin(ang)
    n = cos.shape[0]
    cos_p = jnp.concatenate([jnp.ones((n, ROPE_OFF), F32), cos,
                             jnp.zeros((n, HEAD_PAD - ROPE_OFF - QK_ROPE), F32)], axis=-1)
    return cos_p, _pad_rope_cols(sin)


def _prepared_weights(w_in, rg_wa, rg_ba, rg_wx, rg_bx, w_qb, w_kvb):
    bf = lambda a: a.astype(BF16)
    w_kr = w_in[:, :, _COL_KR:_COL_KR + QK_ROPE]
    w_s = jnp.concatenate([w_in[:, :, _COL_Q:_COL_KR], _pad_rope_cols(w_kr),
                           _pad_rope_cols(_rot_cols(w_kr))], axis=-1)
    w_x = w_in[:, :, :D_RNN]
    w_ygg = jnp.concatenate([w_in[:, :, D_RNN:2 * D_RNN], w_in[:, :, _COL_GATE:]], axis=-1)

    hp = N_HEADS * HEAD_PAD

    def head_pad(w, lo):
        w = jnp.pad(w, ((0, 0),) * 3 + ((lo, HEAD_PAD - lo - w.shape[-1]),))
        return w.reshape(DEPTH, w.shape[1], hp)

    wq = w_qb.reshape(DEPTH, Q_LORA, N_HEADS, QK_NOPE + QK_ROPE)
    w_q = head_pad(wq, 0)
    w_q_rot = head_pad(_rot_cols(wq[..., QK_NOPE:]), ROPE_OFF)
    wkv = w_kvb.reshape(DEPTH, KV_LORA, N_HEADS, QK_NOPE + V_HEAD)
    w_k = head_pad(wkv[..., :QK_NOPE], 0)
    v = wkv[..., QK_NOPE:].reshape(DEPTH, KV_LORA, N_HEADS // 2, 2, V_HEAD)
    w_v = jnp.einsum('lcjpd,pq->lcjpqd', v, jnp.eye(2, dtype=F32)).reshape(DEPTH, KV_LORA, hp)

    nblk = RNN_GROUP // RNN_BW
    ng = D_RNN // RNN_GROUP
    kinds = lambda a, x: jnp.stack([a[:, 0], x[:, 0], a[:, 1], x[:, 1]], axis=1)
    wg = kinds(rg_wa, rg_wx).reshape(DEPTH, 4, ng, nblk, RNN_BW, RNN_BW)
    w_gate = jnp.einsum('ltgnjk,nm->lgnjtmk', wg, jnp.eye(nblk, dtype=F32))
    w_gate = w_gate.reshape(DEPTH, ng, RNN_GROUP, 4 * RNN_GROUP)
    b_gate = kinds(rg_ba, rg_bx).reshape(DEPTH, 4, ng, RNN_GROUP).transpose(0, 2, 1, 3)
    b_gate = b_gate.reshape(DEPTH, ng, 1, 4 * RNN_GROUP)
    return dict(w_x=bf(w_x), w_s=bf(w_s), w_ygg=bf(w_ygg), w_q=bf(w_q), w_q_rot=bf(w_q_rot), w_k=bf(w_k), w_v=bf(w_v),
                w_gate=bf(w_gate), b_gate=b_gate)


def kernel(x_prompt, x_sample, cache_ckv, cache_krope, state_rnn, c, c_ctx, w_ada, b_ada, w_in, conv_w, conv_b, rg_wa, rg_ba, rg_wx, rg_bx, rg_lambda, w_rnn_out, q_norm_g, w_qb, kv_norm_g, w_kvb, w_attn_out, w_out, ln1_g, ln1_b, ln2_g, ln2_b, w1_dense, w3_dense, w2_dense, w_router, b_router, w1_exp, w3_exp, w2_exp):
    nb, seq, _ = x_prompt.shape
    db, dseq, _ = x_sample.shape
    n_ctx = nb * seq
    n_lat = db * dseq
    assert n_ctx % _ROUTE_TILE == 0 and dseq % _ROUTE_TILE == 0 and _ROUTE_TILE % _ROW_TILE == 0 and db + 1 <= 8
    assert DEPTH % 2 == 0
    bf = lambda a: a.astype(BF16)
    vec = lambda a: a.reshape(a.shape[0], 1, a.shape[-1])

    cond8 = jnp.concatenate([c_ctx[None, :], c, jnp.zeros((8 - 1 - db, D_MODEL), F32)], axis=0)
    mods = _ada_mods(cond8, w_ada, b_ada)
    cos_p, sin_p = _rope_tables(dseq // GRID_W)
    past_kr = _pad_rope_cols(cache_krope)
    pw = _prepared_weights(w_in, rg_wa, rg_ba, rg_wx, rg_bx, w_qb, w_kvb)
    w1d, w3d, w2d = bf(w1_dense), bf(w3_dense), bf(w2_dense)
    w1e, w3e, w2e = bf(w1_exp), bf(w3_exp), bf(w2_exp)
    qg, kvg, cb = vec(q_norm_g), vec(kv_norm_g), vec(conv_b)
    l1g, l1b, l2g, l2b = vec(ln1_g), vec(ln1_b), vec(ln2_g), vec(ln2_b)
    br = vec(b_router)
    tiles = (n_ctx, dseq)

    x = (x_prompt.reshape(n_ctx, D_MODEL), x_sample.reshape(n_lat, D_MODEL))
    new_ckv, new_kr, new_h = [], [], []
    for l in range(DEPTH):
        xr, cq, ckv, kr, krr = _in_proj(l, x, mods, pw['w_x'], pw['w_s'], qg, kvg, *tiles)
        new_ckv.append(ckv[:n_ctx].reshape(nb, seq, KV_LORA))
        new_kr.append(kr[:n_ctx, ROPE_OFF:ROPE_OFF + QK_ROPE].reshape(nb, seq, QK_ROPE))

        rnn_args = (conv_w, cb, pw['w_gate'], pw['b_gate'], rg_lambda)
        hs_ctx, h_last = _rglru(l, xr, 0, nb, seq, *rnn_args, emit_last=True)
        (hs_lat,) = _rglru(l, xr, n_ctx, db, dseq, *rnn_args, h0=state_rnn)
        new_h.append(h_last)

        att_w = (pw['w_q'], pw['w_k'], pw['w_v'])
        at_ctx = _attention(l, cq, ckv, kr, 0, nb, seq, *att_w)
        at_lat = _attention(l, cq, ckv, kr, n_ctx, db, dseq, *att_w,
                            lat=(krr, cache_ckv, past_kr, cos_p, sin_p, pw['w_q_rot']))

        x = _mix(l, x, hs_ctx, hs_lat, at_ctx, at_lat, mods, pw['w_ygg'], w_rnn_out, w_attn_out, w_out,
                 l1g, l1b, *tiles)
        if l % 2 == 0:
            x = _ffn_dense(l, x, mods, w1d, w3d, w2d, l2g, l2b, *tiles)
        else:
            x = _ffn_moe(l, x, mods, w_router, br, w1e, w3e, w2e, l2g, l2b, *tiles, split=l == DEPTH - 1)

    return (x[0].reshape(nb, seq, D_MODEL), x[1].reshape(db, dseq, D_MODEL),
            jnp.stack(new_ckv, axis=1), jnp.stack(new_kr, axis=1), jnp.stack(new_h, axis=1))
```
